```python
import math
import jax, jax.numpy as jnp
from jax import lax
import numpy as np

D_MODEL = 4096
BATCH = 4
SEQ = 4096
DEPTH = 1

PLE_DIM = 256
D_FF = 11008
DIFF_HEADS = 8
DIFF_HEAD_DIM = 128
DIFF_Q_BLOCK = 128
DIL_PATTERNS = ((128, 1), (512, 4), (2048, 16))
DIL_GROUP_HEADS = 8
DIL_HEAD_DIM = 128
DIL_HALF_KEYS = 64
LN_EPS = 1e-5
SUBLN_EPS = 1e-5
NEG_INF = -1e30

DIFF_QK_W = DIFF_HEADS * 2 * DIFF_HEAD_DIM
DIFF_V_W = DIFF_HEADS * 2 * DIFF_HEAD_DIM
DIL_W = len(DIL_PATTERNS) * DIL_GROUP_HEADS * DIL_HEAD_DIM
DIL_OUT_W = DIL_GROUP_HEADS * DIL_HEAD_DIM
IN_PROJ_W = 2 * DIFF_QK_W + DIFF_V_W + 3 * DIL_W + 2 * D_MODEL

kernel_name = "hybrid_diffattn_dilated_macaron_deepnorm_layer"


def _alibi_slopes(n):
    return jnp.exp2(-8.0 * jnp.arange(1, n + 1, dtype=jnp.float32) / n)


def _layer_norm(x, g, b):
    xf = x.astype(jnp.float32)
    mu = jnp.mean(xf, axis=-1, keepdims=True)
    var = jnp.mean(jnp.square(xf - mu), axis=-1, keepdims=True)
    return ((xf - mu) * lax.rsqrt(var + LN_EPS) * g + b).astype(x.dtype)


def _swiglu(h, w_in, w_out):
    gate, up = jnp.split(h @ w_in, 2, axis=-1)
    return (jax.nn.silu(gate) * up) @ w_out


def _diff_attention(q, k, v, lam, slopes):
    B, S, H, _, dh = q.shape
    E = v.shape[-1]
    nb = S // DIFF_Q_BLOCK
    scale = dh ** -0.5
    qb = q.reshape(B, nb, DIFF_Q_BLOCK, H, 2, dh).transpose(1, 0, 2, 3, 4, 5)
    kpos = jnp.arange(S, dtype=jnp.int32)

    def block(args):
        qblk, bi = args
        s = jnp.einsum('bqhcd,bkhcd->bhcqk', qblk, k) * scale
        qpos = bi * DIFF_Q_BLOCK + jnp.arange(DIFF_Q_BLOCK, dtype=jnp.int32)
        dist = jnp.abs(qpos[:, None] - kpos[None, :]).astype(jnp.float32)
        s = s - slopes[None, :, None, None, None] * dist
        a = jax.nn.softmax(s, axis=-1)
        a = a[:, :, 0] - lam * a[:, :, 1]
        return jnp.einsum('bhqk,bkhe->bqhe', a, v)

    out = lax.map(block, (qb, jnp.arange(nb, dtype=jnp.int32)))
    return out.transpose(1, 0, 2, 3, 4).reshape(B, S, H, E)


def _dilated_window_attention(q, k, v, dil, slopes):
    B, S, H, Dh = q.shape
    L = S // dil
    W = DIL_HALF_KEYS
    nb = -(-L // W)
    Lp = nb * W
    N = B * dil

    def to_classes(t):
        return t.reshape(B, L, dil, H, Dh).transpose(0, 2, 1, 3, 4).reshape(N, L, H, Dh)

    qc = jnp.pad(to_classes(q), ((0, 0), (0, Lp - L), (0, 0), (0, 0))).reshape(N, nb, W, H, Dh)
    pad_kv = ((0, 0), (W, Lp - L + W), (0, 0), (0, 0))
    kc = jnp.pad(to_classes(k), pad_kv).reshape(N, nb + 2, W, H, Dh)
    vc = jnp.pad(to_classes(v), pad_kv).reshape(N, nb + 2, W, H, Dh)

    def band(t):
        return jnp.concatenate([t[:, :-2], t[:, 1:-1], t[:, 2:]], axis=2)

    kb, vb = band(kc), band(vc)
    s = jnp.einsum('nbqhd,nbkhd->nbhqk', qc, kb) * (Dh ** -0.5)
    rel = jnp.arange(3 * W)[None, :] - W - jnp.arange(W)[:, None]
    kidx = jnp.arange(nb)[:, None] * W + jnp.arange(3 * W)[None, :] - W
    valid = (jnp.abs(rel)[None] <= W) & ((kidx >= 0) & (kidx < L))[:, None, :]
    bias = -(slopes * dil)[:, None, None] * jnp.abs(rel).astype(jnp.float32)[None]
    s = jnp.where(valid[None, :, None], s + bias[None, None], NEG_INF)
    m = jnp.max(s, axis=-1, keepdims=True)
    e = jnp.exp(s - m)
    z = jnp.sum(e, axis=-1, keepdims=True)
    o = jnp.einsum('nbhqk,nbkhd->nbqhd', e / z, vb)
    lse = (m + jnp.log(z))[..., 0]
    o = o.reshape(B, dil, Lp, H, Dh)[:, :, :L].transpose(0, 2, 1, 3, 4).reshape(B, S, H, Dh)
    lse = lse.transpose(0, 1, 3, 2).reshape(B, dil, Lp, H)[:, :, :L].transpose(0, 2, 1, 3).reshape(B, S, H)
    return o, lse


def _hybrid_mixer(h, w_in, lam_q1, lam_k1, lam_q2, lam_k2, subln_g, w_branch_diff,
                  w_branch_dil, w_mix_out, lambda_init):
    B, S, _ = h.shape
    f32 = jnp.float32
    z = h @ w_in
    cuts = np.cumsum([DIFF_QK_W, DIFF_QK_W, DIFF_V_W, DIL_W, DIL_W, DIL_W, D_MODEL]).tolist()
    dq, dk, dv, lq, lk, lv, gate_a, gate_b = jnp.split(z, cuts, axis=-1)

    q = dq.reshape(B, S, DIFF_HEADS, 2, DIFF_HEAD_DIM).astype(f32)
    k = dk.reshape(B, S, DIFF_HEADS, 2, DIFF_HEAD_DIM).astype(f32)
    v = dv.reshape(B, S, DIFF_HEADS, 2 * DIFF_HEAD_DIM).astype(f32)
    lam = (jnp.exp(jnp.sum(lam_q1.astype(f32) * lam_k1.astype(f32)))
           - jnp.exp(jnp.sum(lam_q2.astype(f32) * lam_k2.astype(f32))) + lambda_init)
    oa = _diff_attention(q, k, v, lam, _alibi_slopes(DIFF_HEADS))
    oa = oa * lax.rsqrt(jnp.mean(jnp.square(oa), axis=-1, keepdims=True) + SUBLN_EPS)
    oa = (oa * subln_g.astype(f32) * (1.0 - lambda_init)).reshape(B, S, DIFF_V_W).astype(h.dtype)

    n_pat = len(DIL_PATTERNS)
    q = lq.reshape(B, S, n_pat, DIL_GROUP_HEADS, DIL_HEAD_DIM).astype(f32)
    k = lk.reshape(B, S, n_pat, DIL_GROUP_HEADS, DIL_HEAD_DIM).astype(f32)
    v = lv.reshape(B, S, n_pat, DIL_GROUP_HEADS, DIL_HEAD_DIM).astype(f32)
    dil_slopes = _alibi_slopes(n_pat * DIL_GROUP_HEADS).reshape(DIL_GROUP_HEADS, n_pat)
    outs, lses = [], []
    for g, (_window, dil) in enumerate(DIL_PATTERNS):
        o_g, lse_g = _dilated_window_attention(q[:, :, g], k[:, :, g], v[:, :, g], dil, dil_slopes[:, g])
        outs.append(o_g)
        lses.append(lse_g)
    wts = jax.nn.softmax(jnp.stack(lses, axis=0), axis=0)
    ob = jnp.sum(wts[..., None] * jnp.stack(outs, axis=0), axis=0)
    ob = ob.reshape(B, S, DIL_OUT_W).astype(h.dtype)

    y = jax.nn.sigmoid(gate_a) * (oa @ w_branch_diff) + jax.nn.sigmoid(gate_b) * (ob @ w_branch_dil)
    return y @ w_mix_out


def setup_inputs(seed: int = 0) -> dict:
    key = jax.random.key(seed)
    ks = jax.random.split(key, 32)
    f32 = jnp.float32
    beta = (8 * DEPTH) ** -0.25

    def nrm(k, shape, scale=1.0):
        return jax.random.normal(k, shape, f32) * scale

    col_scale = jnp.concatenate([
        jnp.ones((2 * DIFF_QK_W,), f32), jnp.full((DIFF_V_W,), beta, f32),
        jnp.ones((2 * DIL_W,), f32), jnp.full((DIL_W,), beta, f32),
        jnp.ones((2 * D_MODEL,), f32)])
    d = {}
    d["x"] = nrm(ks[0], (BATCH, SEQ, D_MODEL))
    d["p"] = nrm(ks[1], (DEPTH, BATCH, SEQ, PLE_DIM))
    d["ffn1_w_in"] = nrm(ks[2], (DEPTH, D_MODEL, 2 * D_FF), D_MODEL ** -0.5)
    d["ffn1_w_out"] = nrm(ks[3], (DEPTH, D_FF, D_MODEL), D_FF ** -0.5 * beta)
    d["ln1_g"] = 1.0 + nrm(ks[4], (DEPTH, D_MODEL), 0.02)
    d["ln1_b"] = nrm(ks[5], (DEPTH, D_MODEL), 0.02)
    d["w_in"] = nrm(ks[6], (DEPTH, D_MODEL, IN_PROJ_W), D_MODEL ** -0.5) * col_scale
    d["lam_q1"] = nrm(ks[7], (DEPTH, DIFF_HEAD_DIM), 0.1)
    d["lam_k1"] = nrm(ks[8], (DEPTH, DIFF_HEAD_DIM), 0.1)
    d["lam_q2"] = nrm(ks[9], (DEPTH, DIFF_HEAD_DIM), 0.1)
    d["lam_k2"] = nrm(ks[10], (DEPTH, DIFF_HEAD_DIM), 0.1)
    d["subln_g"] = 1.0 + nrm(ks[11], (DEPTH, 2 * DIFF_HEAD_DIM), 0.02)
    d["w_branch_diff"] = nrm(ks[12], (DEPTH, DIFF_V_W, D_MODEL), DIFF_V_W ** -0.5)
    d["w_branch_dil"] = nrm(ks[13], (DEPTH, DIL_OUT_W, D_MODEL), DIL_OUT_W ** -0.5)
    d["w_mix_out"] = nrm(ks[14], (DEPTH, D_MODEL, D_MODEL), D_MODEL ** -0.5 * beta)
    d["ln2_g"] = 1.0 + nrm(ks[15], (DEPTH, D_MODEL), 0.02)
    d["ln2_b"] = nrm(ks[16], (DEPTH, D_MODEL), 0.02)
    d["ffn2_w_in"] = nrm(ks[17], (DEPTH, D_MODEL, 2 * D_FF), D_MODEL ** -0.5)
    d["ffn2_w_out"] = nrm(ks[18], (DEPTH, D_FF, D_MODEL), D_FF ** -0.5 * beta)
    d["ln3_g"] = 1.0 + nrm(ks[19], (DEPTH, D_MODEL), 0.02)
    d["ln3_b"] = nrm(ks[20], (DEPTH, D_MODEL), 0.02)
    d["w_ple_gate"] = nrm(ks[21], (DEPTH, D_MODEL, D_MODEL), D_MODEL ** -0.5)
    d["w_ple_proj"] = nrm(ks[22], (DEPTH, PLE_DIM, D_MODEL), PLE_DIM ** -0.5 * beta)
    d["ln4_g"] = 1.0 + nrm(ks[23], (DEPTH, D_MODEL), 0.02)
    d["ln4_b"] = nrm(ks[24], (DEPTH, D_MODEL), 0.02)
    return d


def reference(x, p, ffn1_w_in, ffn1_w_out, ln1_g, ln1_b, w_in, lam_q1, lam_k1, lam_q2,
              lam_k2, subln_g, w_branch_diff, w_branch_dil, w_mix_out, ln2_g, ln2_b,
              ffn2_w_in, ffn2_w_out, ln3_g, ln3_b, w_ple_gate, w_ple_proj, ln4_g, ln4_b):
    alpha = (2 * DEPTH) ** 0.25
    h = x
    for i in range(DEPTH):
        lambda_init = 0.8 - 0.6 * math.exp(-0.3 * i)
        h = _layer_norm(alpha * h + 0.5 * _swiglu(h, ffn1_w_in[i], ffn1_w_out[i]), ln1_g[i], ln1_b[i])
        mix = _hybrid_mixer(h, w_in[i], lam_q1[i], lam_k1[i], lam_q2[i], lam_k2[i], subln_g[i],
                            w_branch_diff[i], w_branch_dil[i], w_mix_out[i], lambda_init)
        h = _layer_norm(alpha * h + mix, ln2_g[i], ln2_b[i])
        h = _layer_norm(alpha * h + 0.5 * _swiglu(h, ffn2_w_in[i], ffn2_w_out[i]), ln3_g[i], ln3_b[i])
        ple = jax.nn.sigmoid(h @ w_ple_gate[i]) * (p[i] @ w_ple_proj[i])
        h = _layer_norm(alpha * h + ple, ln4_g[i], ln4_b[i])
    return h
```

```python
import functools
import math

import jax
import jax.numpy as jnp
from jax import lax
from jax.experimental import pallas as pl
from jax.experimental.pallas import tpu as pltpu

F32 = jnp.float32
BF16 = jnp.bfloat16

D_MODEL = 4096
PLE_DIM = 256
D_FF = 11008
DIFF_HEADS = 8
DIFF_HEAD_DIM = 128
DIL_PATTERNS = ((128, 1), (512, 4), (2048, 16))
DIL_GROUP_HEADS = 8
DIL_HEAD_DIM = 128
DIL_HALF_KEYS = 64
LN_EPS = 1e-5
SUBLN_EPS = 1e-5
NEG_INF = -1e30

DIFF_QK_W = DIFF_HEADS * 2 * DIFF_HEAD_DIM
DIFF_V_W = DIFF_HEADS * 2 * DIFF_HEAD_DIM
DIL_W = len(DIL_PATTERNS) * DIL_GROUP_HEADS * DIL_HEAD_DIM
DIL_OUT_W = DIL_GROUP_HEADS * DIL_HEAD_DIM
IN_PROJ_W = 2 * DIFF_QK_W + DIFF_V_W + 3 * DIL_W + 2 * D_MODEL

OFF_DQ = 0
OFF_DK = OFF_DQ + DIFF_QK_W
OFF_DV = OFF_DK + DIFF_QK_W
OFF_LQ = OFF_DV + DIFF_V_W
OFF_LK = OFF_LQ + DIL_W
OFF_LV = OFF_LK + DIL_W
OFF_GA = OFF_LV + DIL_W
OFF_GB = OFF_GA + D_MODEL

V7X_LANES = 128
V7X_VMEM_BYTES = 64 * 1024 * 1024
V7X_VMEM_REQUEST = 56 * 1024 * 1024


def _params(semantics, vmem_bytes=V7X_VMEM_REQUEST):
    return pltpu.CompilerParams(dimension_semantics=semantics, vmem_limit_bytes=vmem_bytes)


def _ffn_kernel(x_ref, wg_ref, wu_ref, wo_ref, o_ref):
    j = pl.program_id(1)
    x = x_ref[...]
    g = jnp.dot(x, wg_ref[...], preferred_element_type=F32)
    u = jnp.dot(x, wu_ref[...], preferred_element_type=F32)
    h = (g * jax.nn.sigmoid(g) * u).astype(BF16)
    part = jnp.dot(h, wo_ref[...], preferred_element_type=F32)

    @pl.when(j == 0)
    def _():
        o_ref[...] = part

    @pl.when(j > 0)
    def _():
        o_ref[...] += part


def _ffn(x_bf, w_in_bf, w_out_bf, *, tm=512, tf=256):
    T, D = x_bf.shape
    F = w_out_bf.shape[0]
    nf = F // tf
    return pl.pallas_call(
        _ffn_kernel,
        grid=(T // tm, nf),
        in_specs=[
            pl.BlockSpec((tm, D), lambda i, j: (i, 0)),
            pl.BlockSpec((D, tf), lambda i, j: (0, j)),
            pl.BlockSpec((D, tf), lambda i, j: (0, j + nf)),
            pl.BlockSpec((tf, D), lambda i, j: (j, 0)),
        ],
        out_specs=pl.BlockSpec((tm, D), lambda i, j: (i, 0)),
        out_shape=jax.ShapeDtypeStruct((T, D), F32),
        compiler_params=_params(("parallel", "arbitrary")),
        name="ffn",
    )(x_bf, w_in_bf, w_in_bf, w_out_bf)


def _ln_kernel(h_ref, br_ref, g_ref, b_ref, o_ref, obf_ref, *, alpha, scale):
    y = alpha * h_ref[...] + scale * br_ref[...]
    mu = jnp.mean(y, axis=-1, keepdims=True)
    yc = y - mu
    var = jnp.mean(yc * yc, axis=-1, keepdims=True)
    out = yc * lax.rsqrt(var + LN_EPS) * g_ref[...] + b_ref[...]
    o_ref[...] = out
    obf_ref[...] = out.astype(BF16)


def _res_ln(h, branch, g, b, *, alpha, scale, tm=256):
    T, D = h.shape
    row = pl.BlockSpec((tm, D), lambda i: (i, 0))
    vec = pl.BlockSpec((1, D), lambda i: (0, 0))
    return pl.pallas_call(
        functools.partial(_ln_kernel, alpha=alpha, scale=scale),
        grid=(T // tm,),
        in_specs=[row, row, vec, vec],
        out_specs=[row, row],
        out_shape=[jax.ShapeDtypeStruct((T, D), F32), jax.ShapeDtypeStruct((T, D), BF16)],
        compiler_params=_params(("parallel",)),
        name="res_ln",
    )(h, branch, g.reshape(1, D), b.reshape(1, D))


def _matmul_kernel(x_ref, w_ref, o_ref):
    o_ref[...] = jnp.dot(x_ref[...], w_ref[...], preferred_element_type=F32).astype(o_ref.dtype)


def _matmul(x_bf, w_bf, out_dtype, *, tm=1024, tn=512, name="matmul"):
    T, K = x_bf.shape
    N = w_bf.shape[1]
    return pl.pallas_call(
        _matmul_kernel,
        grid=(T // tm, N // tn),
        in_specs=[
            pl.BlockSpec((tm, K), lambda i, j: (i, 0)),
            pl.BlockSpec((K, tn), lambda i, j: (0, j)),
        ],
        out_specs=pl.BlockSpec((tm, tn), lambda i, j: (i, j)),
        out_shape=jax.ShapeDtypeStruct((T, N), out_dtype),
        compiler_params=_params(("parallel", "parallel")),
        name=name,
    )(x_bf, w_bf)


def _diff_attn_kernel(slopes_ref, lam_ref, g_ref, q_ref, k_ref, v_ref, o_ref,
                      m_sc, l_sc, acc_sc, *, tq, tk, seq, lambda_init):
    h = pl.program_id(1)
    qi = pl.program_id(2)
    slope = slopes_ref[h]
    scale = DIFF_HEAD_DIM ** -0.5
    dh = DIFF_HEAD_DIM

    row = lax.broadcasted_iota(jnp.int32, (tq, tk), 0)
    col = lax.broadcasted_iota(jnp.int32, (tq, tk), 1)
    col_minus_row = (col - row).astype(F32)

    m_sc[...] = jnp.full(m_sc.shape, -jnp.inf, F32)
    l_sc[...] = jnp.zeros(l_sc.shape, F32)
    acc_sc[...] = jnp.zeros(acc_sc.shape, F32)

    def body(kc, carry):
        ks = pl.multiple_of(kc * tk, tk)
        kblk = k_ref[pl.ds(ks, tk), :]
        vblk = v_ref[pl.ds(ks, tk), :]
        offset = (ks - qi * tq).astype(F32)
        bias = -slope * jnp.abs(col_minus_row + offset)
        for c in range(2):
            q_c = q_ref[:, c * dh:(c + 1) * dh]
            k_c = kblk[:, c * dh:(c + 1) * dh]
            s = lax.dot_general(q_c, k_c, (((1,), (1,)), ((), ())),
                                preferred_element_type=F32) * scale + bias
            m_old = m_sc[c]
            m_new = jnp.maximum(m_old, jnp.max(s, axis=-1, keepdims=True))
            alpha = jnp.exp(m_old - m_new)
            p = jnp.exp(s - m_new)
            l_sc[c] = alpha * l_sc[c] + jnp.sum(p, axis=-1, keepdims=True)
            acc_sc[c] = alpha * acc_sc[c] + jnp.dot(p.astype(BF16), vblk,
                                                    preferred_element_type=F32)
            m_sc[c] = m_new
        return carry

    lax.fori_loop(0, seq // tk, body, 0)

    lam_rows = lam_ref[...]
    lam = (jnp.exp(jnp.sum(lam_rows[0:1] * lam_rows[1:2], axis=-1, keepdims=True))
           - jnp.exp(jnp.sum(lam_rows[2:3] * lam_rows[3:4], axis=-1, keepdims=True))
           + lambda_init)
    o = acc_sc[0] / l_sc[0] - lam * (acc_sc[1] / l_sc[1])
    ms = jnp.mean(o * o, axis=-1, keepdims=True)
    o = o * lax.rsqrt(ms + SUBLN_EPS) * g_ref[...] * (1.0 - lambda_init)
    o_ref[...] = o.astype(o_ref.dtype)


def _diff_attention(z, slopes, lam_rows, subln_g, *, batch, seq, lambda_init, tq=512, tk=1024):
    T = z.shape[0]
    e = 2 * DIFF_HEAD_DIM
    nq = seq // tq
    qspec = pl.BlockSpec((tq, e), lambda b, h, i: (b * nq + i, OFF_DQ // e + h))
    kspec = pl.BlockSpec((seq, e), lambda b, h, i: (b, OFF_DK // e + h))
    vspec = pl.BlockSpec((seq, e), lambda b, h, i: (b, OFF_DV // e + h))
    return pl.pallas_call(
        functools.partial(_diff_attn_kernel, tq=tq, tk=tk, seq=seq, lambda_init=lambda_init),
        grid=(batch, DIFF_HEADS, nq),
        in_specs=[
            pl.BlockSpec(memory_space=pltpu.SMEM),
            pl.BlockSpec((4, DIFF_HEAD_DIM), lambda b, h, i: (0, 0)),
            pl.BlockSpec((1, e), lambda b, h, i: (0, 0)),
            qspec, kspec, vspec,
        ],
        out_specs=pl.BlockSpec((tq, e), lambda b, h, i: (b * nq + i, h)),
        out_shape=jax.ShapeDtypeStruct((T, DIFF_V_W), BF16),
        scratch_shapes=[
            pltpu.VMEM((2, tq, 1), F32),
            pltpu.VMEM((2, tq, 1), F32),
            pltpu.VMEM((2, tq, e), F32),
        ],
        compiler_params=_params(("parallel", "parallel", "parallel")),
        name="diff_attn",
    )(slopes, lam_rows, subln_g.reshape(1, e), z, z, z)


def _dil_attn_kernel(slopes_ref, q_ref, k_ref, v_ref, o_ref, lse_ref, *, length, tq):
    h = pl.program_id(1)
    slope = slopes_ref[h]
    w = DIL_HALF_KEYS
    nk = tq + 2 * w
    scale = DIL_HEAD_DIM ** -0.5
    row = lax.broadcasted_iota(jnp.int32, (tq, nk), 0)
    col = lax.broadcasted_iota(jnp.int32, (tq, nk), 1)
    col_minus_row = col - row

    def body(i, carry):
        qs = pl.multiple_of(i * tq, tq)
        ks = pl.multiple_of(jnp.clip(qs - w, 0, length - nk), w)
        rel = jnp.abs(col_minus_row + (ks - qs))
        q = q_ref[pl.ds(qs, tq), :]
        k = k_ref[pl.ds(ks, nk), :]
        v = v_ref[pl.ds(ks, nk), :]
        s = lax.dot_general(q, k, (((1,), (1,)), ((), ())), preferred_element_type=F32) * scale
        s = jnp.where(rel <= w, s - slope * rel.astype(F32), NEG_INF)
        m = jnp.max(s, axis=-1, keepdims=True)
        e = jnp.exp(s - m)
        z = jnp.sum(e, axis=-1, keepdims=True)
        o = jnp.dot(e.astype(BF16), v, preferred_element_type=F32) / z
        o_ref[pl.ds(qs, tq), :] = o
        lse_ref[pl.ds(qs, tq), :] = jnp.broadcast_to(m + jnp.log(z), (tq, DIL_HEAD_DIM))
        return carry

    lax.fori_loop(0, length // tq, body, 0)


def _dilated_attention(z, slopes_g, *, group, dil, batch, seq, tq=128):
    length = seq // dil
    nz_blocks = IN_PROJ_W // DIL_HEAD_DIM
    z3 = z.reshape(batch, length, dil * IN_PROJ_W)

    def in_spec(off):
        base = (off + group * DIL_OUT_W) // DIL_HEAD_DIM
        return pl.BlockSpec((None, length, DIL_HEAD_DIM),
                            lambda b, h, r: (b, 0, r * nz_blocks + base + h))

    out_spec = pl.BlockSpec((None, length, DIL_HEAD_DIM),
                            lambda b, h, r: (b, 0, r * DIL_GROUP_HEADS + h))
    out_sds = jax.ShapeDtypeStruct((batch, length, dil * DIL_OUT_W), F32)
    o, lse = pl.pallas_call(
        functools.partial(_dil_attn_kernel, length=length, tq=tq),
        grid=(batch, DIL_GROUP_HEADS, dil),
        in_specs=[pl.BlockSpec(memory_space=pltpu.SMEM),
                  in_spec(OFF_LQ), in_spec(OFF_LK), in_spec(OFF_LV)],
        out_specs=[out_spec, out_spec],
        out_shape=[out_sds, out_sds],
        compiler_params=_params(("parallel", "parallel", "parallel")),
        name=f"dil_attn_{dil}",
    )(slopes_g, z3, z3, z3)
    return o.reshape(batch * seq, DIL_OUT_W), lse.reshape(batch * seq, DIL_OUT_W)


def _merge_kernel(oa_ref, o0_ref, o1_ref, o2_ref, l0_ref, l1_ref, l2_ref, ga_ref, gb_ref,
                  wa_ref, wb_ref, y_ref, ob_sc):
    @pl.when(pl.program_id(1) == 0)
    def _():
        l0, l1, l2 = l0_ref[...], l1_ref[...], l2_ref[...]
        m = jnp.maximum(jnp.maximum(l0, l1), l2)
        w0, w1, w2 = jnp.exp(l0 - m), jnp.exp(l1 - m), jnp.exp(l2 - m)
        ob = (w0 * o0_ref[...] + w1 * o1_ref[...] + w2 * o2_ref[...]) / (w0 + w1 + w2)
        ob_sc[...] = ob.astype(BF16)

    ya = jnp.dot(oa_ref[...], wa_ref[...], preferred_element_type=F32)
    yb = jnp.dot(ob_sc[...], wb_ref[...], preferred_element_type=F32)
    y = (jax.nn.sigmoid(ga_ref[...].astype(F32)) * ya
         + jax.nn.sigmoid(gb_ref[...].astype(F32)) * yb)
    y_ref[...] = y.astype(y_ref.dtype)


def _merge(oa, dil_outs, dil_lses, z, wa_bf, wb_bf, *, tm=512, tn=512):
    T = oa.shape[0]
    row_a = pl.BlockSpec((tm, DIFF_V_W), lambda i, j: (i, 0))
    row_b = pl.BlockSpec((tm, DIL_OUT_W), lambda i, j: (i, 0))
    return pl.pallas_call(
        _merge_kernel,
        grid=(T // tm, D_MODEL // tn),
        in_specs=[row_a, row_b, row_b, row_b, row_b, row_b, row_b,
                  pl.BlockSpec((tm, tn), lambda i, j: (i, OFF_GA // tn + j)),
                  pl.BlockSpec((tm, tn), lambda i, j: (i, OFF_GB // tn + j)),
                  pl.BlockSpec((DIFF_V_W, tn), lambda i, j: (0, j)),
                  pl.BlockSpec((DIL_OUT_W, tn), lambda i, j: (0, j))],
        out_specs=pl.BlockSpec((tm, tn), lambda i, j: (i, j)),
        out_shape=jax.ShapeDtypeStruct((T, D_MODEL), BF16),
        scratch_shapes=[pltpu.VMEM((tm, DIL_OUT_W), BF16)],
        compiler_params=_params(("parallel", "arbitrary")),
        name="merge",
    )(oa, *dil_outs, *dil_lses, z, z, wa_bf, wb_bf)


def _ple_kernel(x_ref, p_ref, wg_ref, wp_ref, o_ref):
    gate = jnp.dot(x_ref[...], wg_ref[...], preferred_element_type=F32)
    proj = jnp.dot(p_ref[...].astype(BF16), wp_ref[...], preferred_element_type=F32)
    o_ref[...] = jax.nn.sigmoid(gate) * proj


def _ple(x_bf, p, wg_bf, wp_bf, *, tm=1024, tn=512):
    T, D = x_bf.shape
    return pl.pallas_call(
        _ple_kernel,
        grid=(T // tm, D // tn),
        in_specs=[
            pl.BlockSpec((tm, D), lambda i, j: (i, 0)),
            pl.BlockSpec((tm, PLE_DIM), lambda i, j: (i, 0)),
            pl.BlockSpec((D, tn), lambda i, j: (0, j)),
            pl.BlockSpec((PLE_DIM, tn), lambda i, j: (0, j)),
        ],
        out_specs=pl.BlockSpec((tm, tn), lambda i, j: (i, j)),
        out_shape=jax.ShapeDtypeStruct((T, D), F32),
        compiler_params=_params(("parallel", "parallel")),
        name="ple",
    )(x_bf, p, wg_bf, wp_bf)


def _alibi_slopes(n):
    return jnp.exp2(-8.0 * jnp.arange(1, n + 1, dtype=F32) / n)


def kernel(x, p, ffn1_w_in, ffn1_w_out, ln1_g, ln1_b, w_in, lam_q1, lam_k1, lam_q2, lam_k2,
           subln_g, w_branch_diff, w_branch_dil, w_mix_out, ln2_g, ln2_b, ffn2_w_in,
           ffn2_w_out, ln3_g, ln3_b, w_ple_gate, w_ple_proj, ln4_g, ln4_b):
    batch, seq, d = x.shape
    depth = ffn1_w_in.shape[0]
    T = batch * seq
    alpha = (2 * depth) ** 0.25
    n_pat = len(DIL_PATTERNS)

    h = x.reshape(T, d)
    h_bf = h.astype(BF16)
    diff_slopes = _alibi_slopes(DIFF_HEADS)
    dil_slopes = _alibi_slopes(n_pat * DIL_GROUP_HEADS).reshape(DIL_GROUP_HEADS, n_pat)

    for i in range(depth):
        lambda_init = 0.8 - 0.6 * math.exp(-0.3 * i)

        f = _ffn(h_bf, ffn1_w_in[i].astype(BF16), ffn1_w_out[i].astype(BF16))
        h, h_bf = _res_ln(h, f, ln1_g[i], ln1_b[i], alpha=alpha, scale=0.5)

        z = _matmul(h_bf, w_in[i].astype(BF16), BF16, name="in_proj")
        lam_rows = jnp.stack([lam_q1[i], lam_k1[i], lam_q2[i], lam_k2[i]]).astype(F32)
        oa = _diff_attention(z, diff_slopes, lam_rows, subln_g[i].astype(F32),
                             batch=batch, seq=seq, lambda_init=lambda_init)
        dil_outs, dil_lses = [], []
        for g, (_window, dil) in enumerate(DIL_PATTERNS):
            o_g, lse_g = _dilated_attention(z, dil_slopes[:, g] * dil, group=g, dil=dil,
                                            batch=batch, seq=seq)
            dil_outs.append(o_g)
            dil_lses.append(lse_g)
        y = _merge(oa, dil_outs, dil_lses, z, w_branch_diff[i].astype(BF16),
                   w_branch_dil[i].astype(BF16))
        mix = _matmul(y, w_mix_out[i].astype(BF16), F32, name="mix_out")
        h, h_bf = _res_ln(h, mix, ln2_g[i], ln2_b[i], alpha=alpha, scale=1.0)

        f = _ffn(h_bf, ffn2_w_in[i].astype(BF16), ffn2_w_out[i].astype(BF16))
        h, h_bf = _res_ln(h, f, ln3_g[i], ln3_b[i], alpha=alpha, scale=0.5)

        ple = _ple(h_bf, p[i].reshape(T, PLE_DIM), w_ple_gate[i].astype(BF16),
                   w_ple_proj[i].astype(BF16))
        h, h_bf = _res_ln(h, ple, ln4_g[i], ln4_b[i], alpha=alpha, scale=1.0)

    return h.reshape(batch, seq, d)
```

```python
import functools
import math

import jax
import jax.numpy as jnp
from jax import lax
from jax.experimental import pallas as pl
from jax.experimental.pallas import tpu as pltpu

F32 = jnp.float32
BF16 = jnp.bfloat16

D_MODEL = 4096
PLE_DIM = 256
D_FF = 11008
DIFF_HEADS = 8
DIFF_HEAD_DIM = 128
DIL_PATTERNS = ((128, 1), (512, 4), (2048, 16))
DIL_GROUP_HEADS = 8
DIL_HEAD_DIM = 128
DIL_HALF_KEYS = 64
LN_EPS = 1e-5
SUBLN_EPS = 1e-5
NEG_INF = -1e30

DIFF_QK_W = DIFF_HEADS * 2 * DIFF_HEAD_DIM
DIFF_V_W = DIFF_HEADS * 2 * DIFF_HEAD_DIM
DIL_W = len(DIL_PATTERNS) * DIL_GROUP_HEADS * DIL_HEAD_DIM
DIL_OUT_W = DIL_GROUP_HEADS * DIL_HEAD_DIM
IN_PROJ_W = 2 * DIFF_QK_W + DIFF_V_W + 3 * DIL_W + 2 * D_MODEL

OFF_DQ = 0
OFF_DK = OFF_DQ + DIFF_QK_W
OFF_DV = OFF_DK + DIFF_QK_W
OFF_LQ = OFF_DV + DIFF_V_W
OFF_LK = OFF_LQ + DIL_W
OFF_LV = OFF_LK + DIL_W
OFF_GA = OFF_LV + DIL_W
OFF_GB = OFF_GA + D_MODEL

Z_DQ = 0
Z_DK = Z_DQ + DIFF_QK_W
Z_DV = Z_DK + DIFF_QK_W
Z_LQ = Z_DV + DIFF_V_W
Z_LK = Z_LQ + DIL_OUT_W
Z_LV = Z_LK + DIL_OUT_W
Z_GA = Z_LV + DIL_OUT_W
Z_GB = Z_GA + D_MODEL
Z_W = Z_GB + D_MODEL

V7X_LANES = 128
V7X_VMEM_BYTES = 64 * 1024 * 1024
V7X_VMEM_REQUEST = 56 * 1024 * 1024

FFN_CHUNK = 512
D_FF_PAD = -(-D_FF // FFN_CHUNK) * FFN_CHUNK

LOG2E = math.log2(math.e)
COEF_PIECES = 3
POS_SPLIT = 64


def _params(semantics, vmem_bytes=V7X_VMEM_REQUEST):
    return pltpu.CompilerParams(dimension_semantics=semantics, vmem_limit_bytes=vmem_bytes)


def _ffn_kernel(x_ref, wg_ref, wu_ref, wo_ref, o_ref):
    j = pl.program_id(1)
    x = x_ref[...]
    g = jnp.dot(x, wg_ref[...], preferred_element_type=F32)
    u = jnp.dot(x, wu_ref[...], preferred_element_type=F32)
    h = (g * jax.nn.sigmoid(g) * u).astype(BF16)
    part = jnp.dot(h, wo_ref[...], preferred_element_type=F32)

    @pl.when(j == 0)
    def _():
        o_ref[...] = part

    @pl.when(j > 0)
    def _():
        o_ref[...] += part


def _ffn(x_bf, w_in_bf, w_out_bf, *, tm=512, tf=FFN_CHUNK):
    T, D = x_bf.shape
    nf = w_out_bf.shape[0] // tf
    return pl.pallas_call(
        _ffn_kernel,
        grid=(T // tm, nf),
        in_specs=[
            pl.BlockSpec((tm, D), lambda i, j: (i, 0)),
            pl.BlockSpec((D, tf), lambda i, j: (0, j)),
            pl.BlockSpec((D, tf), lambda i, j: (0, j + nf)),
            pl.BlockSpec((tf, D), lambda i, j: (j, 0)),
        ],
        out_specs=pl.BlockSpec((tm, D), lambda i, j: (i, 0)),
        out_shape=jax.ShapeDtypeStruct((T, D), F32),
        compiler_params=_params(("parallel", "arbitrary")),
        name="ffn",
    )(x_bf, w_in_bf, w_in_bf, w_out_bf)


def _ffn_weights(w_in, w_out):
    pad = D_FF_PAD - D_FF
    gate = jnp.pad(w_in[:, :D_FF].astype(BF16), ((0, 0), (0, pad)))
    up = jnp.pad(w_in[:, D_FF:].astype(BF16), ((0, 0), (0, pad)))
    return jnp.concatenate([gate, up], axis=1), jnp.pad(w_out.astype(BF16), ((0, pad), (0, 0)))


def _ln_kernel(h_ref, br_ref, g_ref, b_ref, o_ref, obf_ref, *, alpha, scale):
    y = alpha * h_ref[...] + scale * br_ref[...]
    mu = jnp.mean(y, axis=-1, keepdims=True)
    yc = y - mu
    var = jnp.mean(yc * yc, axis=-1, keepdims=True)
    out = yc * lax.rsqrt(var + LN_EPS) * g_ref[...] + b_ref[...]
    o_ref[...] = out
    obf_ref[...] = out.astype(BF16)


def _res_ln(h, branch, g, b, *, alpha, scale, tm=256):
    T, D = h.shape
    row = pl.BlockSpec((tm, D), lambda i: (i, 0))
    vec = pl.BlockSpec((1, D), lambda i: (0, 0))
    return pl.pallas_call(
        functools.partial(_ln_kernel, alpha=alpha, scale=scale),
        grid=(T // tm,),
        in_specs=[row, row, vec, vec],
        out_specs=[row, row],
        out_shape=[jax.ShapeDtypeStruct((T, D), F32), jax.ShapeDtypeStruct((T, D), BF16)],
        compiler_params=_params(("parallel",)),
        name="res_ln",
    )(h, branch, g.reshape(1, D), b.reshape(1, D))


def _matmul_kernel(x_ref, w_ref, o_ref):
    o_ref[...] = jnp.dot(x_ref[...], w_ref[...], preferred_element_type=F32).astype(o_ref.dtype)


def _matmul_scaled_kernel(x_ref, w_ref, s_ref, o_ref):
    acc = jnp.dot(x_ref[...], w_ref[...], preferred_element_type=F32)
    o_ref[...] = (acc * s_ref[...]).astype(o_ref.dtype)


def _matmul(x_bf, w_bf, out_dtype, *, n_out=None, wcol=None, colscale=None, tm=1024, tn=512,
            name="matmul"):
    T, K = x_bf.shape
    n_out = w_bf.shape[1] if n_out is None else n_out
    wcol = (lambda j: j) if wcol is None else wcol
    in_specs = [
        pl.BlockSpec((tm, K), lambda i, j: (i, 0)),
        pl.BlockSpec((K, tn), lambda i, j: (0, wcol(j))),
    ]
    args = [x_bf, w_bf]
    body = _matmul_kernel
    if colscale is not None:
        in_specs.append(pl.BlockSpec((1, tn), lambda i, j: (0, j)))
        args.append(colscale.reshape(1, n_out))
        body = _matmul_scaled_kernel
    return pl.pallas_call(
        body,
        grid=(T // tm, n_out // tn),
        in_specs=in_specs,
        out_specs=pl.BlockSpec((tm, tn), lambda i, j: (i, j)),
        out_shape=jax.ShapeDtypeStruct((T, n_out), out_dtype),
        compiler_params=_params(("parallel", "parallel")),
        name=name,
    )(*args)


def _main_wcol(tn):
    per_sec = DIL_OUT_W // tn
    first = OFF_LQ // tn
    n_pat = len(DIL_PATTERNS)

    def wcol(j):
        k = j - first
        dil_tile = first + (k // per_sec) * (n_pat * per_sec) + k % per_sec
        gate_tile = j + (n_pat - 1) * 3 * per_sec
        return jnp.where(j < first, j, jnp.where(k < 3 * per_sec, dil_tile, gate_tile))

    return wcol


def _group_wcol(group, tn):
    per_sec = DIL_OUT_W // tn

    def wcol(j):
        return (OFF_LQ + group * DIL_OUT_W) // tn + (j // per_sec) * (DIL_W // tn) + j % per_sec

    return wcol


def _alibi_columns(pos, coefs, q_side):
    lane = lax.broadcasted_iota(jnp.int32, pos.shape, 1)
    piece = jnp.where(lane < 3, lane, jnp.where(lane < 6, lane - 3,
                                                jnp.where(lane < 9, lane - 6, lane - 9)))
    cv = jnp.where(piece == 0, coefs[0], jnp.where(piece == 1, coefs[1], coefs[2]))
    hi = (pos - (pos & (POS_SPLIT - 1))).astype(F32)
    lo = (pos & (POS_SPLIT - 1)).astype(F32)
    n = 2 * COEF_PIECES
    if q_side:
        cols = jnp.where(lane < COEF_PIECES, -hi, jnp.where(lane < n, -lo, cv))
    else:
        cols = jnp.where(lane < n, cv, jnp.where(lane < n + COEF_PIECES, hi, lo))
    return jnp.where(lane < 2 * n, cols, 0.0)


def _diff_attn_kernel(coef_ref, lam_ref, g_ref, q_ref, k_ref, v_ref, o_ref,
                      kaug_sc, qaug_sc, corr_sc, sa_sc, sb_sc, p_sc, m_sc, l_sc, alpha_sc, acc_sc,
                      *, tq, rb, seq, lambda_init):
    h = pl.program_id(1)
    qi = pl.program_id(2)
    dh = DIFF_HEAD_DIM
    n_chunks = seq // tq
    coefs = [coef_ref[h, t] for t in range(COEF_PIECES)]
    row_iota = lax.broadcasted_iota(jnp.int32, (tq, dh), 0)

    @pl.when(qi == 0)
    def _():
        def build(t, carry):
            r0 = pl.multiple_of(t * tq, tq)
            ak = _alibi_columns(row_iota + r0, coefs, q_side=False).astype(BF16)
            for c in range(2):
                kaug_sc[c, pl.ds(r0, tq), 0:dh] = k_ref[pl.ds(r0, tq), c * dh:(c + 1) * dh]
                kaug_sc[c, pl.ds(r0, tq), dh:2 * dh] = ak
            return carry

        lax.fori_loop(0, n_chunks, build, 0)
        row = lax.broadcasted_iota(jnp.int32, (tq, tq), 0)
        col = lax.broadcasted_iota(jnp.int32, (tq, tq), 1)
        slope2 = coefs[0] + coefs[1] + coefs[2]
        corr_sc[0] = jnp.zeros((tq, tq), F32)
        corr_sc[1] = (2.0 * slope2) * jnp.minimum(row - col, 0).astype(F32)

    aq = _alibi_columns(row_iota + qi * tq, coefs, q_side=True)
    for c in range(2):
        q_c = q_ref[:, c * dh:(c + 1) * dh]
        qaug_sc[0, c, :, 0:dh] = q_c
        qaug_sc[0, c, :, dh:2 * dh] = aq.astype(BF16)
        qaug_sc[1, c, :, 0:dh] = q_c
        qaug_sc[1, c, :, dh:2 * dh] = (-aq).astype(BF16)

    m_sc[...] = jnp.full(m_sc.shape, -jnp.inf, F32)
    l_sc[...] = jnp.zeros(l_sc.shape, F32)
    acc_sc[...] = jnp.zeros(acc_sc.shape, F32)

    n_slabs = tq // V7X_LANES
    last = n_chunks - 1

    def scores(kc, s_ref):
        ks = pl.multiple_of(kc * tq, tq)
        side = jnp.where(kc > qi, 1, 0)
        for c in range(2):
            s_ref[c] = lax.dot_general(
                qaug_sc[side, c], kaug_sc[c, pl.ds(ks, tq), :],
                (((1,), (1,)), ((), ())), preferred_element_type=F32)

    def absorb(kc, s_ref, near_diag):
        ks = pl.multiple_of(kc * tq, tq)
        on_diag = jnp.where(kc == qi, 1, 0)

        def block_slabs(c, rows):
            sb = s_ref[c, rows, :]
            if near_diag:
                sb = sb + corr_sc[on_diag, rows, :]
            return [sb[:, t * V7X_LANES:(t + 1) * V7X_LANES] for t in range(n_slabs)]

        for c in range(2):
            for b in range(tq // rb):
                rows = slice(b * rb, (b + 1) * rb)
                slabs = block_slabs(c, rows)
                mx = functools.reduce(jnp.maximum, slabs)
                m_old = m_sc[c, rows, :]
                m_new = jnp.maximum(m_old, jnp.broadcast_to(
                    jnp.max(mx, axis=-1, keepdims=True), (rb, V7X_LANES)))
                a = jnp.exp2(m_old - m_new)
                ps = [jnp.exp2(sl - m_new) for sl in slabs]
                l_sc[c, rows, :] = a * l_sc[c, rows, :] + functools.reduce(jnp.add, ps)
                p_sc[c, rows, :] = jnp.concatenate(ps, axis=-1).astype(BF16)
                alpha_sc[c, rows, :] = a
                m_sc[c, rows, :] = m_new
            alpha = alpha_sc[c]
            acc_sc[c] = jnp.concatenate([alpha, alpha], axis=-1) * acc_sc[c] + jnp.dot(
                p_sc[c], v_ref[pl.ds(ks, tq), :], preferred_element_type=F32)

    def pair_body(kc, near_diag):
        scores(kc + 1, sb_sc)
        absorb(kc, sa_sc, near_diag)
        scores(jnp.minimum(kc + 2, last), sa_sc)
        absorb(kc + 1, sb_sc, near_diag)

    def pair(t, carry):
        kc = 2 * t
        lax.cond(t == qi // 2,
                 functools.partial(pair_body, kc, True),
                 functools.partial(pair_body, kc, False))
        return carry

    scores(0, sa_sc)
    lax.fori_loop(0, n_chunks // 2, pair, 0)

    lam_rows = lam_ref[...]
    lam = (jnp.exp(jnp.sum(lam_rows[0:1] * lam_rows[1:2], axis=-1, keepdims=True))
           - jnp.exp(jnp.sum(lam_rows[2:3] * lam_rows[3:4], axis=-1, keepdims=True))
           + lambda_init)
    l0 = jnp.sum(l_sc[0], axis=-1, keepdims=True)
    l1 = jnp.sum(l_sc[1], axis=-1, keepdims=True)
    o = acc_sc[0] / l0 - lam * (acc_sc[1] / l1)
    ms = jnp.mean(o * o, axis=-1, keepdims=True)
    o = o * lax.rsqrt(ms + SUBLN_EPS) * g_ref[...] * (1.0 - lambda_init)
    o_ref[...] = o.astype(o_ref.dtype)


def _diff_attention(z, coefs, lam_rows, subln_g, *, batch, seq, lambda_init, tq=512, rb=32):
    T = z.shape[0]
    e = 2 * DIFF_HEAD_DIM
    nq = seq // tq
    qspec = pl.BlockSpec((tq, e), lambda b, h, i: (b * nq + i, Z_DQ // e + h))
    kspec = pl.BlockSpec((seq, e), lambda b, h, i: (b, Z_DK // e + h))
    vspec = pl.BlockSpec((seq, e), lambda b, h, i: (b, Z_DV // e + h))
    return pl.pallas_call(
        functools.partial(_diff_attn_kernel, tq=tq, rb=rb, seq=seq, lambda_init=lambda_init),
        grid=(batch, DIFF_HEADS, nq),
        in_specs=[
            pl.BlockSpec(memory_space=pltpu.SMEM),
            pl.BlockSpec((4, DIFF_HEAD_DIM), lambda b, h, i: (0, 0)),
            pl.BlockSpec((1, e), lambda b, h, i: (0, 0)),
            qspec, kspec, vspec,
        ],
        out_specs=pl.BlockSpec((tq, e), lambda b, h, i: (b * nq + i, h)),
        out_shape=jax.ShapeDtypeStruct((T, DIFF_V_W), BF16),
        scratch_shapes=[
            pltpu.VMEM((2, seq, e), BF16),
            pltpu.VMEM((2, 2, tq, e), BF16),
            pltpu.VMEM((2, tq, tq), F32),
            pltpu.VMEM((2, tq, tq), F32),
            pltpu.VMEM((2, tq, tq), F32),
            pltpu.VMEM((2, tq, tq), BF16),
            pltpu.VMEM((2, tq, V7X_LANES), F32),
            pltpu.VMEM((2, tq, V7X_LANES), F32),
            pltpu.VMEM((2, tq, V7X_LANES), F32),
            pltpu.VMEM((2, tq, e), F32),
        ],
        compiler_params=_params(("parallel", "parallel", "arbitrary")),
        name="diff_attn",
    )(coefs, lam_rows, subln_g.reshape(1, e), z, z, z)


def _dil_attn_kernel(slopes_ref, q_ref, k_ref, v_ref, o_ref, lse_ref, *, seq, length, tq, unroll):
    h = pl.program_id(1)
    slope = slopes_ref[h]
    w = DIL_HALF_KEYS
    nk = tq + 2 * w
    blocks_per_class = length // tq
    scale = DIL_HEAD_DIM ** -0.5
    row = lax.broadcasted_iota(jnp.int32, (tq, nk), 0)
    col = lax.broadcasted_iota(jnp.int32, (tq, nk), 1)
    col_minus_row = col - row

    def body(i, carry):
        cls_base = (i // blocks_per_class) * length
        qs_local = (i % blocks_per_class) * tq
        ks_local = jnp.clip(qs_local - w, 0, length - nk)
        qs = pl.multiple_of(cls_base + qs_local, tq)
        ks = pl.multiple_of(cls_base + ks_local, w)
        rel = jnp.abs(col_minus_row + (ks_local - qs_local))
        q = q_ref[pl.ds(qs, tq), :]
        k = k_ref[pl.ds(ks, nk), :]
        v = v_ref[pl.ds(ks, nk), :]
        s = lax.dot_general(q, k, (((1,), (1,)), ((), ())), preferred_element_type=F32) * scale
        s = jnp.where(rel <= w, s - slope * rel.astype(F32), NEG_INF)
        m = jnp.max(s, axis=-1, keepdims=True)
        e = jnp.exp(s - m)
        z = jnp.sum(e, axis=-1, keepdims=True)
        o = jnp.dot(e.astype(BF16), v, preferred_element_type=F32) / z
        o_ref[pl.ds(qs, tq), :] = o
        lse_ref[pl.ds(qs, tq), :] = jnp.broadcast_to(m + jnp.log(z), (tq, DIL_HEAD_DIM))
        return carry

    lax.fori_loop(0, seq // tq, body, 0, unroll=unroll)


def _dilated_attention(qkv, offsets, slopes_g, *, dil, batch, seq, tq=128, unroll=4):
    T = qkv.shape[0]
    hd = DIL_HEAD_DIM

    def in_spec(off):
        return pl.BlockSpec((seq, hd), lambda b, h: (b, off // hd + h))

    out_spec = pl.BlockSpec((seq, hd), lambda b, h: (b, h))
    out_sds = jax.ShapeDtypeStruct((T, DIL_OUT_W), F32)
    return pl.pallas_call(
        functools.partial(_dil_attn_kernel, seq=seq, length=seq // dil, tq=tq, unroll=unroll),
        grid=(batch, DIL_GROUP_HEADS),
        in_specs=[pl.BlockSpec(memory_space=pltpu.SMEM)] + [in_spec(o) for o in offsets],
        out_specs=[out_spec, out_spec],
        out_shape=[out_sds, out_sds],
        compiler_params=_params(("parallel", "parallel")),
        name=f"dil_attn_{dil}",
    )(slopes_g, qkv, qkv, qkv)


def _to_class_major(a, batch, seq, dil):
    c = a.shape[-1]
    return a.reshape(batch, seq // dil, dil, c).transpose(0, 2, 1, 3).reshape(batch * seq, c)


def _to_token_order(a, batch, seq, dil):
    c = a.shape[-1]
    return a.reshape(batch, dil, seq // dil, c).transpose(0, 2, 1, 3).reshape(batch * seq, c)


def _merge_kernel(oa_ref, o0_ref, o1_ref, o2_ref, l0_ref, l1_ref, l2_ref, ga_ref, gb_ref,
                  wa_ref, wb_ref, y_ref, ob_sc):
    @pl.when(pl.program_id(1) == 0)
    def _():
        l0, l1, l2 = l0_ref[...], l1_ref[...], l2_ref[...]
        m = jnp.maximum(jnp.maximum(l0, l1), l2)
        w0, w1, w2 = jnp.exp(l0 - m), jnp.exp(l1 - m), jnp.exp(l2 - m)
        ob = (w0 * o0_ref[...] + w1 * o1_ref[...] + w2 * o2_ref[...]) / (w0 + w1 + w2)
        ob_sc[...] = ob.astype(BF16)

    ya = jnp.dot(oa_ref[...], wa_ref[...], preferred_element_type=F32)
    yb = jnp.dot(ob_sc[...], wb_ref[...], preferred_element_type=F32)
    y = (jax.nn.sigmoid(ga_ref[...].astype(F32)) * ya
         + jax.nn.sigmoid(gb_ref[...].astype(F32)) * yb)
    y_ref[...] = y.astype(y_ref.dtype)


def _merge(oa, dil_outs, dil_lses, z, wa_bf, wb_bf, *, tm=512, tn=512):
    T = oa.shape[0]
    row_a = pl.BlockSpec((tm, DIFF_V_W), lambda i, j: (i, 0))
    row_b = pl.BlockSpec((tm, DIL_OUT_W), lambda i, j: (i, 0))
    return pl.pallas_call(
        _merge_kernel,
        grid=(T // tm, D_MODEL // tn),
        in_specs=[row_a, row_b, row_b, row_b, row_b, row_b, row_b,
                  pl.BlockSpec((tm, tn), lambda i, j: (i, Z_GA // tn + j)),
                  pl.BlockSpec((tm, tn), lambda i, j: (i, Z_GB // tn + j)),
                  pl.BlockSpec((DIFF_V_W, tn), lambda i, j: (0, j)),
                  pl.BlockSpec((DIL_OUT_W, tn), lambda i, j: (0, j))],
        out_specs=pl.BlockSpec((tm, tn), lambda i, j: (i, j)),
        out_shape=jax.ShapeDtypeStruct((T, D_MODEL), BF16),
        scratch_shapes=[pltpu.VMEM((tm, DIL_OUT_W), BF16)],
        compiler_params=_params(("parallel", "arbitrary")),
        name="merge",
    )(oa, *dil_outs, *dil_lses, z, z, wa_bf, wb_bf)


def _ple_kernel(x_ref, p_ref, wg_ref, wp_ref, o_ref):
    gate = jnp.dot(x_ref[...], wg_ref[...], preferred_element_type=F32)
    proj = jnp.dot(p_ref[...].astype(BF16), wp_ref[...], preferred_element_type=F32)
    o_ref[...] = jax.nn.sigmoid(gate) * proj


def _ple(x_bf, p, wg_bf, wp_bf, *, tm=1024, tn=512):
    T, D = x_bf.shape
    return pl.pallas_call(
        _ple_kernel,
        grid=(T // tm, D // tn),
        in_specs=[
            pl.BlockSpec((tm, D), lambda i, j: (i, 0)),
            pl.BlockSpec((tm, PLE_DIM), lambda i, j: (i, 0)),
            pl.BlockSpec((D, tn), lambda i, j: (0, j)),
            pl.BlockSpec((PLE_DIM, tn), lambda i, j: (0, j)),
        ],
        out_specs=pl.BlockSpec((tm, tn), lambda i, j: (i, j)),
        out_shape=jax.ShapeDtypeStruct((T, D), F32),
        compiler_params=_params(("parallel", "parallel")),
        name="ple",
    )(x_bf, p, wg_bf, wp_bf)


def _alibi_slopes(n):
    return jnp.exp2(-8.0 * jnp.arange(1, n + 1, dtype=F32) / n)


def _bf16_pieces(v):
    pieces, rest = [], v
    for _ in range(COEF_PIECES):
        piece = rest.astype(BF16).astype(F32)
        pieces.append(piece)
        rest = rest - piece
    return jnp.stack(pieces, axis=-1)


def kernel(x, p, ffn1_w_in, ffn1_w_out, ln1_g, ln1_b, w_in, lam_q1, lam_k1, lam_q2, lam_k2,
           subln_g, w_branch_diff, w_branch_dil, w_mix_out, ln2_g, ln2_b, ffn2_w_in,
           ffn2_w_out, ln3_g, ln3_b, w_ple_gate, w_ple_proj, ln4_g, ln4_b):
    batch, seq, d = x.shape
    depth = ffn1_w_in.shape[0]
    T = batch * seq
    alpha = (2 * depth) ** 0.25
    n_pat = len(DIL_PATTERNS)
    tn = 512

    h = x.reshape(T, d)
    h_bf = h.astype(BF16)
    diff_coefs = _bf16_pieces(_alibi_slopes(DIFF_HEADS) * LOG2E)
    dil_slopes = _alibi_slopes(n_pat * DIL_GROUP_HEADS).reshape(DIL_GROUP_HEADS, n_pat)
    colscale = jnp.ones((Z_W,), F32).at[Z_DQ:Z_DK].set(DIFF_HEAD_DIM ** -0.5 * LOG2E)

    for i in range(depth):
        lambda_init = 0.8 - 0.6 * math.exp(-0.3 * i)

        f = _ffn(h_bf, *_ffn_weights(ffn1_w_in[i], ffn1_w_out[i]))
        h, h_bf = _res_ln(h, f, ln1_g[i], ln1_b[i], alpha=alpha, scale=0.5)

        w_in_bf = w_in[i].astype(BF16)
        z = _matmul(h_bf, w_in_bf, BF16, n_out=Z_W, wcol=_main_wcol(tn), colscale=colscale,
                    tn=tn, name="in_proj")
        lam_rows = jnp.stack([lam_q1[i], lam_k1[i], lam_q2[i], lam_k2[i]]).astype(F32)
        oa = _diff_attention(z, diff_coefs, lam_rows, subln_g[i].astype(F32),
                             batch=batch, seq=seq, lambda_init=lambda_init)

        dil_outs, dil_lses = [], []
        for g, (_window, dil) in enumerate(DIL_PATTERNS):
            slopes_g = dil_slopes[:, g] * dil
            if dil == 1:
                o_g, lse_g = _dilated_attention(z, (Z_LQ, Z_LK, Z_LV), slopes_g, dil=dil,
                                                batch=batch, seq=seq)
            else:
                qkv = _matmul(_to_class_major(h_bf, batch, seq, dil), w_in_bf, BF16,
                              n_out=3 * DIL_OUT_W, wcol=_group_wcol(g, tn), tn=tn,
                              name=f"in_proj_dil{dil}")
                o_g, lse_g = _dilated_attention(qkv, (0, DIL_OUT_W, 2 * DIL_OUT_W), slopes_g,
                                                dil=dil, batch=batch, seq=seq)
                o_g = _to_token_order(o_g, batch, seq, dil)
                lse_g = _to_token_order(lse_g, batch, seq, dil)
            dil_outs.append(o_g)
            dil_lses.append(lse_g)

        y = _merge(oa, dil_outs, dil_lses, z, w_branch_diff[i].astype(BF16),
                   w_branch_dil[i].astype(BF16))
        mix = _matmul(y, w_mix_out[i].astype(BF16), F32, tn=tn, name="mix_out")
        h, h_bf = _res_ln(h, mix, ln2_g[i], ln2_b[i], alpha=alpha, scale=1.0)

        f = _ffn(h_bf, *_ffn_weights(ffn2_w_in[i], ffn2_w_out[i]))
        h, h_bf = _res_ln(h, f, ln3_g[i], ln3_b[i], alpha=alpha, scale=0.5)

        ple = _ple(h_bf, p[i].reshape(T, PLE_DIM), w_ple_gate[i].astype(BF16),
                   w_ple_proj[i].astype(BF16))
        h, h_bf = _res_ln(h, ple, ln4_g[i], ln4_b[i], alpha=alpha, scale=1.0)

    return h.reshape(batch, seq, d)
```

```python
import functools
import math

import jax
import jax.numpy as jnp
from jax import lax
from jax.experimental import pallas as pl
from jax.experimental.pallas import tpu as pltpu

F32 = jnp.float32
BF16 = jnp.bfloat16

D_MODEL = 4096
PLE_DIM = 256
D_FF = 11008
DIFF_HEADS = 8
DIFF_HEAD_DIM = 128
DIL_PATTERNS = ((128, 1), (512, 4), (2048, 16))
DIL_GROUP_HEADS = 8
DIL_HEAD_DIM = 128
DIL_HALF_KEYS = 64
LN_EPS = 1e-5
SUBLN_EPS = 1e-5
NEG_INF = -1e30

DIFF_QK_W = DIFF_HEADS * 2 * DIFF_HEAD_DIM
DIFF_V_W = DIFF_HEADS * 2 * DIFF_HEAD_DIM
DIL_W = len(DIL_PATTERNS) * DIL_GROUP_HEADS * DIL_HEAD_DIM
DIL_OUT_W = DIL_GROUP_HEADS * DIL_HEAD_DIM
IN_PROJ_W = 2 * DIFF_QK_W + DIFF_V_W + 3 * DIL_W + 2 * D_MODEL

OFF_DQ = 0
OFF_DK = OFF_DQ + DIFF_QK_W
OFF_DV = OFF_DK + DIFF_QK_W
OFF_LQ = OFF_DV + DIFF_V_W
OFF_LK = OFF_LQ + DIL_W
OFF_LV = OFF_LK + DIL_W
OFF_GA = OFF_LV + DIL_W
OFF_GB = OFF_GA + D_MODEL

Z_DQ = 0
Z_DK = Z_DQ + DIFF_QK_W
Z_DV = Z_DK + DIFF_QK_W
Z_LQ = Z_DV + DIFF_V_W
Z_LK = Z_LQ + DIL_OUT_W
Z_LV = Z_LK + DIL_OUT_W
Z_GA = Z_LV + DIL_OUT_W
Z_GB = Z_GA + D_MODEL
Z_W = Z_GB + D_MODEL

V7X_LANES = 128
V7X_VMEM_BYTES = 64 * 1024 * 1024
V7X_VMEM_REQUEST = 56 * 1024 * 1024
V7X_VMEM_REQUEST_ROWS = 60 * 1024 * 1024

FFN_BLOCK = 256

LOG2E = math.log2(math.e)
COEF_PIECES = 3
POS_SPLIT = 64


def _params(semantics, vmem_bytes=V7X_VMEM_REQUEST):
    return pltpu.CompilerParams(dimension_semantics=semantics, vmem_limit_bytes=vmem_bytes)


def _ffn_kernel(x_ref, wga_ref, wgb_ref, wua_ref, wub_ref, woa_ref, wob_ref, o_ref, *, n_blocks):
    j = pl.program_id(1)

    @pl.when(j == 0)
    def _():
        o_ref[...] = jnp.zeros(o_ref.shape, F32)

    x = x_ref[...]

    def hidden(wg_ref, wu_ref):
        g = jnp.dot(x, wg_ref[...], preferred_element_type=F32)
        u = jnp.dot(x, wu_ref[...], preferred_element_type=F32)
        return g * jax.nn.sigmoid(g) * u

    ha = hidden(wga_ref, wua_ref).astype(BF16)
    hb = jnp.where(2 * j + 1 < n_blocks, hidden(wgb_ref, wub_ref), 0.0).astype(BF16)
    part = jnp.dot(ha, woa_ref[...], preferred_element_type=F32)
    part += jnp.dot(hb, wob_ref[...], preferred_element_type=F32)
    o_ref[...] += part


def _ffn(x_bf, w_in_bf, w_out_bf, *, tm=512, tb=FFN_BLOCK):
    T, D = x_bf.shape
    n_blocks = w_out_bf.shape[0] // tb
    steps = -(-n_blocks // 2)

    def first(j):
        return 2 * j

    def second(j):
        return jnp.minimum(2 * j + 1, n_blocks - 1)

    def col_spec(blk, off):
        return pl.BlockSpec((D, tb), lambda i, j: (0, off + blk(j)))

    def row_spec(blk):
        return pl.BlockSpec((tb, D), lambda i, j: (blk(j), 0))

    return pl.pallas_call(
        functools.partial(_ffn_kernel, n_blocks=n_blocks),
        grid=(T // tm, steps),
        in_specs=[
            pl.BlockSpec((tm, D), lambda i, j: (i, 0)),
            col_spec(first, 0), col_spec(second, 0),
            col_spec(first, n_blocks), col_spec(second, n_blocks),
            row_spec(first), row_spec(second),
        ],
        out_specs=pl.BlockSpec((tm, D), lambda i, j: (i, 0)),
        out_shape=jax.ShapeDtypeStruct((T, D), F32),
        compiler_params=_params(("parallel", "arbitrary")),
        name="ffn",
    )(x_bf, w_in_bf, w_in_bf, w_in_bf, w_in_bf, w_out_bf, w_out_bf)


def _layer_norm_rows(y, g_ref, b_ref):
    mu = jnp.mean(y, axis=-1, keepdims=True)
    yc = y - mu
    var = jnp.mean(yc * yc, axis=-1, keepdims=True)
    return yc * lax.rsqrt(var + LN_EPS) * g_ref[...] + b_ref[...]


def _ln_kernel(h_ref, br_ref, g_ref, b_ref, o_ref, obf_ref, *, alpha, scale):
    out = _layer_norm_rows(alpha * h_ref[...] + scale * br_ref[...], g_ref, b_ref)
    o_ref[...] = out
    obf_ref[...] = out.astype(BF16)


def _res_ln(h, branch, g, b, *, alpha, scale, tm=256):
    T, D = h.shape
    row = pl.BlockSpec((tm, D), lambda i: (i, 0))
    vec = pl.BlockSpec((1, D), lambda i: (0, 0))
    return pl.pallas_call(
        functools.partial(_ln_kernel, alpha=alpha, scale=scale),
        grid=(T // tm,),
        in_specs=[row, row, vec, vec],
        out_specs=[row, row],
        out_shape=[jax.ShapeDtypeStruct((T, D), F32), jax.ShapeDtypeStruct((T, D), BF16)],
        compiler_params=_params(("parallel",)),
        name="res_ln",
    )(h, branch, g.reshape(1, D), b.reshape(1, D))


def _matmul_kernel(x_ref, w_ref, o_ref):
    acc = jnp.dot(x_ref[...], w_ref[...].astype(BF16), preferred_element_type=F32)
    o_ref[...] = acc.astype(o_ref.dtype)


def _matmul_scaled_kernel(x_ref, w_ref, s_ref, o_ref):
    acc = jnp.dot(x_ref[...], w_ref[...].astype(BF16), preferred_element_type=F32)
    o_ref[...] = (acc * s_ref[...]).astype(o_ref.dtype)


def _matmul(x_bf, w, out_dtype, *, n_out, wcol, colscale=None, tm=1024, tn=512, name="matmul"):
    T, K = x_bf.shape
    in_specs = [
        pl.BlockSpec((tm, K), lambda i, j: (i, 0)),
        pl.BlockSpec((K, tn), lambda i, j: (0, wcol(j))),
    ]
    args = [x_bf, w]
    body = _matmul_kernel
    if colscale is not None:
        in_specs.append(pl.BlockSpec((1, tn), lambda i, j: (0, j)))
        args.append(colscale.reshape(1, n_out))
        body = _matmul_scaled_kernel
    return pl.pallas_call(
        body,
        grid=(T // tm, n_out // tn),
        in_specs=in_specs,
        out_specs=pl.BlockSpec((tm, tn), lambda i, j: (i, j)),
        out_shape=jax.ShapeDtypeStruct((T, n_out), out_dtype),
        compiler_params=_params(("parallel", "parallel")),
        name=name,
    )(*args)


def _main_wcol(tn):
    per_sec = DIL_OUT_W // tn
    first = OFF_LQ // tn
    n_pat = len(DIL_PATTERNS)

    def wcol(j):
        k = j - first
        dil_tile = first + (k // per_sec) * (n_pat * per_sec) + k % per_sec
        gate_tile = j + (n_pat - 1) * 3 * per_sec
        return jnp.where(j < first, j, jnp.where(k < 3 * per_sec, dil_tile, gate_tile))

    return wcol


def _group_wcol(group, tn):
    per_sec = DIL_OUT_W // tn

    def wcol(j):
        return (OFF_LQ + group * DIL_OUT_W) // tn + (j // per_sec) * (DIL_W // tn) + j % per_sec

    return wcol


def _alibi_columns(pos, coefs, q_side):
    lane = lax.broadcasted_iota(jnp.int32, pos.shape, 1)
    piece = jnp.where(lane < 3, lane, jnp.where(lane < 6, lane - 3,
                                                jnp.where(lane < 9, lane - 6, lane - 9)))
    cv = jnp.where(piece == 0, coefs[0], jnp.where(piece == 1, coefs[1], coefs[2]))
    hi = (pos - (pos & (POS_SPLIT - 1))).astype(F32)
    lo = (pos & (POS_SPLIT - 1)).astype(F32)
    n = 2 * COEF_PIECES
    if q_side:
        cols = jnp.where(lane < COEF_PIECES, -hi, jnp.where(lane < n, -lo, cv))
    else:
        cols = jnp.where(lane < n, cv, jnp.where(lane < n + COEF_PIECES, hi, lo))
    return jnp.where(lane < 2 * n, cols, 0.0)


def _diff_attn_kernel(coef_ref, lam_ref, g_ref, q_ref, k_ref, v_ref, o_ref,
                      kaug_sc, qaug_sc, corr_sc, sa_sc, sb_sc, p_sc, m_sc, l_sc, alpha_sc, acc_sc,
                      *, tq, rb, seq, lambda_init):
    h = pl.program_id(1)
    qi = pl.program_id(2)
    dh = DIFF_HEAD_DIM
    n_chunks = seq // tq
    coefs = [coef_ref[h, t] for t in range(COEF_PIECES)]
    row_iota = lax.broadcasted_iota(jnp.int32, (tq, dh), 0)

    @pl.when(qi == 0)
    def _():
        def build(t, carry):
            r0 = pl.multiple_of(t * tq, tq)
            ak = _alibi_columns(row_iota + r0, coefs, q_side=False).astype(BF16)
            for c in range(2):
                kaug_sc[c, pl.ds(r0, tq), 0:dh] = k_ref[pl.ds(r0, tq), c * dh:(c + 1) * dh]
                kaug_sc[c, pl.ds(r0, tq), dh:2 * dh] = ak
            return carry

        lax.fori_loop(0, n_chunks, build, 0)
        row = lax.broadcasted_iota(jnp.int32, (tq, tq), 0)
        col = lax.broadcasted_iota(jnp.int32, (tq, tq), 1)
        slope2 = coefs[0] + coefs[1] + coefs[2]
        corr_sc[0] = jnp.zeros((tq, tq), F32)
        corr_sc[1] = (2.0 * slope2) * jnp.minimum(row - col, 0).astype(F32)

    aq = _alibi_columns(row_iota + qi * tq, coefs, q_side=True)
    for c in range(2):
        q_c = q_ref[:, c * dh:(c + 1) * dh]
        qaug_sc[0, c, :, 0:dh] = q_c
        qaug_sc[0, c, :, dh:2 * dh] = aq.astype(BF16)
        qaug_sc[1, c, :, 0:dh] = q_c
        qaug_sc[1, c, :, dh:2 * dh] = (-aq).astype(BF16)

    m_sc[...] = jnp.full(m_sc.shape, -jnp.inf, F32)
    l_sc[...] = jnp.zeros(l_sc.shape, F32)
    acc_sc[...] = jnp.zeros(acc_sc.shape, F32)

    n_slabs = tq // V7X_LANES
    last = n_chunks - 1

    def scores(kc, s_ref):
        ks = pl.multiple_of(kc * tq, tq)
        side = jnp.where(kc > qi, 1, 0)
        for c in range(2):
            s_ref[c] = lax.dot_general(
                qaug_sc[side, c], kaug_sc[c, pl.ds(ks, tq), :],
                (((1,), (1,)), ((), ())), preferred_element_type=F32)

    def absorb(kc, s_ref, near_diag):
        ks = pl.multiple_of(kc * tq, tq)
        on_diag = jnp.where(kc == qi, 1, 0)

        def block_slabs(c, rows):
            sb = s_ref[c, rows, :]
            if near_diag:
                sb = sb + corr_sc[on_diag, rows, :]
            return [sb[:, t * V7X_LANES:(t + 1) * V7X_LANES] for t in range(n_slabs)]

        for c in range(2):
            for b in range(tq // rb):
                rows = slice(b * rb, (b + 1) * rb)
                slabs = block_slabs(c, rows)
                mx = functools.reduce(jnp.maximum, slabs)
                m_old = m_sc[c, rows, :]
                m_new = jnp.maximum(m_old, jnp.broadcast_to(
                    jnp.max(mx, axis=-1, keepdims=True), (rb, V7X_LANES)))
                a = jnp.exp2(m_old - m_new)
                ps = [jnp.exp2(sl - m_new) for sl in slabs]
                l_sc[c, rows, :] = a * l_sc[c, rows, :] + functools.reduce(jnp.add, ps)
                p_sc[c, rows, :] = jnp.concatenate(ps, axis=-1).astype(BF16)
                alpha_sc[c, rows, :] = a
                m_sc[c, rows, :] = m_new
            alpha = alpha_sc[c]
            acc_sc[c] = jnp.concatenate([alpha, alpha], axis=-1) * acc_sc[c] + jnp.dot(
                p_sc[c], v_ref[pl.ds(ks, tq), :], preferred_element_type=F32)

    def pair_body(kc, near_diag):
        scores(kc + 1, sb_sc)
        absorb(kc, sa_sc, near_diag)
        scores(jnp.minimum(kc + 2, last), sa_sc)
        absorb(kc + 1, sb_sc, near_diag)

    def pair(t, carry):
        kc = 2 * t
        lax.cond(t == qi // 2,
                 functools.partial(pair_body, kc, True),
                 functools.partial(pair_body, kc, False))
        return carry

    scores(0, sa_sc)
    lax.fori_loop(0, n_chunks // 2, pair, 0)

    lam_rows = lam_ref[...]
    lam = (jnp.exp(jnp.sum(lam_rows[0:1] * lam_rows[1:2], axis=-1, keepdims=True))
           - jnp.exp(jnp.sum(lam_rows[2:3] * lam_rows[3:4], axis=-1, keepdims=True))
           + lambda_init)
    l0 = jnp.sum(l_sc[0], axis=-1, keepdims=True)
    l1 = jnp.sum(l_sc[1], axis=-1, keepdims=True)
    o = acc_sc[0] / l0 - lam * (acc_sc[1] / l1)
    ms = jnp.mean(o * o, axis=-1, keepdims=True)
    o = o * lax.rsqrt(ms + SUBLN_EPS) * g_ref[...] * (1.0 - lambda_init)
    o_ref[...] = o.astype(o_ref.dtype)


def _diff_attention(z, coefs, lam_rows, subln_g, *, batch, seq, lambda_init, tq=512, rb=32):
    T = z.shape[0]
    e = 2 * DIFF_HEAD_DIM
    nq = seq // tq
    qspec = pl.BlockSpec((tq, e), lambda b, h, i: (b * nq + i, Z_DQ // e + h))
    kspec = pl.BlockSpec((seq, e), lambda b, h, i: (b, Z_DK // e + h))
    vspec = pl.BlockSpec((seq, e), lambda b, h, i: (b, Z_DV // e + h))
    return pl.pallas_call(
        functools.partial(_diff_attn_kernel, tq=tq, rb=rb, seq=seq, lambda_init=lambda_init),
        grid=(batch, DIFF_HEADS, nq),
        in_specs=[
            pl.BlockSpec(memory_space=pltpu.SMEM),
            pl.BlockSpec((4, DIFF_HEAD_DIM), lambda b, h, i: (0, 0)),
            pl.BlockSpec((1, e), lambda b, h, i: (0, 0)),
            qspec, kspec, vspec,
        ],
        out_specs=pl.BlockSpec((tq, e), lambda b, h, i: (b * nq + i, h)),
        out_shape=jax.ShapeDtypeStruct((T, DIFF_V_W), BF16),
        scratch_shapes=[
            pltpu.VMEM((2, seq, e), BF16),
            pltpu.VMEM((2, 2, tq, e), BF16),
            pltpu.VMEM((2, tq, tq), F32),
            pltpu.VMEM((2, tq, tq), F32),
            pltpu.VMEM((2, tq, tq), F32),
            pltpu.VMEM((2, tq, tq), BF16),
            pltpu.VMEM((2, tq, V7X_LANES), F32),
            pltpu.VMEM((2, tq, V7X_LANES), F32),
            pltpu.VMEM((2, tq, V7X_LANES), F32),
            pltpu.VMEM((2, tq, e), F32),
        ],
        compiler_params=_params(("parallel", "parallel", "arbitrary")),
        name="diff_attn",
    )(coefs, lam_rows, subln_g.reshape(1, e), z, z, z)


def _dil_attn_kernel(slopes_ref, q_ref, k_ref, v_ref, o_ref, lse_ref, *, seq, length, tq, unroll):
    h = pl.program_id(1)
    slope = slopes_ref[h]
    w = DIL_HALF_KEYS
    nk = tq + 2 * w
    blocks_per_class = length // tq
    scale = DIL_HEAD_DIM ** -0.5
    row = lax.broadcasted_iota(jnp.int32, (tq, nk), 0)
    col = lax.broadcasted_iota(jnp.int32, (tq, nk), 1)
    col_minus_row = col - row

    def body(i, carry):
        cls_base = (i // blocks_per_class) * length
        qs_local = (i % blocks_per_class) * tq
        ks_local = jnp.clip(qs_local - w, 0, length - nk)
        qs = pl.multiple_of(cls_base + qs_local, tq)
        ks = pl.multiple_of(cls_base + ks_local, w)
        rel = jnp.abs(col_minus_row + (ks_local - qs_local))
        q = q_ref[pl.ds(qs, tq), :]
        k = k_ref[pl.ds(ks, nk), :]
        v = v_ref[pl.ds(ks, nk), :]
        s = lax.dot_general(q, k, (((1,), (1,)), ((), ())), preferred_element_type=F32) * scale
        s = jnp.where(rel <= w, s - slope * rel.astype(F32), NEG_INF)
        m = jnp.max(s, axis=-1, keepdims=True)
        e = jnp.exp(s - m)
        z = jnp.sum(e, axis=-1, keepdims=True)
        o = jnp.dot(e.astype(BF16), v, preferred_element_type=F32) / z
        o_ref[pl.ds(qs, tq), :] = o
        lse_ref[pl.ds(qs, tq), :] = jnp.broadcast_to(m + jnp.log(z), (tq, DIL_HEAD_DIM))
        return carry

    lax.fori_loop(0, seq // tq, body, 0, unroll=unroll)


def _dilated_attention(qkv, offsets, slopes_g, *, dil, batch, seq, tq=128, unroll=4):
    T = qkv.shape[0]
    hd = DIL_HEAD_DIM

    def in_spec(off):
        return pl.BlockSpec((seq, hd), lambda b, h: (b, off // hd + h))

    out_spec = pl.BlockSpec((seq, hd), lambda b, h: (b, h))
    out_sds = jax.ShapeDtypeStruct((T, DIL_OUT_W), F32)
    return pl.pallas_call(
        functools.partial(_dil_attn_kernel, seq=seq, length=seq // dil, tq=tq, unroll=unroll),
        grid=(batch, DIL_GROUP_HEADS),
        in_specs=[pl.BlockSpec(memory_space=pltpu.SMEM)] + [in_spec(o) for o in offsets],
        out_specs=[out_spec, out_spec],
        out_shape=[out_sds, out_sds],
        compiler_params=_params(("parallel", "parallel")),
        name=f"dil_attn_{dil}",
    )(slopes_g, qkv, qkv, qkv)


def _to_class_major(a, batch, seq, dil):
    c = a.shape[-1]
    return a.reshape(batch, seq // dil, dil, c).transpose(0, 2, 1, 3).reshape(batch * seq, c)


def _to_token_order(a, batch, seq, dil):
    c = a.shape[-1]
    return a.reshape(batch, dil, seq // dil, c).transpose(0, 2, 1, 3).reshape(batch * seq, c)


def _merge_kernel(oa_ref, o0_ref, o1_ref, o2_ref, l0_ref, l1_ref, l2_ref, ga_ref, gb_ref,
                  wa_ref, wb_ref, y_ref, ob_sc):
    @pl.when(pl.program_id(1) == 0)
    def _():
        l0, l1, l2 = l0_ref[...], l1_ref[...], l2_ref[...]
        m = jnp.maximum(jnp.maximum(l0, l1), l2)
        w0, w1, w2 = jnp.exp(l0 - m), jnp.exp(l1 - m), jnp.exp(l2 - m)
        ob = (w0 * o0_ref[...] + w1 * o1_ref[...] + w2 * o2_ref[...]) / (w0 + w1 + w2)
        ob_sc[...] = ob.astype(BF16)

    ya = jnp.dot(oa_ref[...], wa_ref[...], preferred_element_type=F32)
    yb = jnp.dot(ob_sc[...], wb_ref[...], preferred_element_type=F32)
    y = (jax.nn.sigmoid(ga_ref[...].astype(F32)) * ya
         + jax.nn.sigmoid(gb_ref[...].astype(F32)) * yb)
    y_ref[...] = y.astype(y_ref.dtype)


def _merge(oa, dil_outs, dil_lses, z, wa_bf, wb_bf, *, tm=512, tn=512):
    T = oa.shape[0]
    row_a = pl.BlockSpec((tm, DIFF_V_W), lambda i, j: (i, 0))
    row_b = pl.BlockSpec((tm, DIL_OUT_W), lambda i, j: (i, 0))
    return pl.pallas_call(
        _merge_kernel,
        grid=(T // tm, D_MODEL // tn),
        in_specs=[row_a, row_b, row_b, row_b, row_b, row_b, row_b,
                  pl.BlockSpec((tm, tn), lambda i, j: (i, Z_GA // tn + j)),
                  pl.BlockSpec((tm, tn), lambda i, j: (i, Z_GB // tn + j)),
                  pl.BlockSpec((DIFF_V_W, tn), lambda i, j: (0, j)),
                  pl.BlockSpec((DIL_OUT_W, tn), lambda i, j: (0, j))],
        out_specs=pl.BlockSpec((tm, tn), lambda i, j: (i, j)),
        out_shape=jax.ShapeDtypeStruct((T, D_MODEL), BF16),
        scratch_shapes=[pltpu.VMEM((tm, DIL_OUT_W), BF16)],
        compiler_params=_params(("parallel", "arbitrary")),
        name="merge",
    )(oa, *dil_outs, *dil_lses, z, z, wa_bf, wb_bf)


def _ple_ln_kernel(x_ref, p_ref, wg_ref, wp_ref, h_ref, g_ref, b_ref, o_ref, *, alpha, tn):
    j = pl.program_id(1)
    gate = jnp.dot(x_ref[...], wg_ref[...], preferred_element_type=F32)
    proj = jnp.dot(p_ref[...].astype(BF16), wp_ref[...], preferred_element_type=F32)
    o_ref[:, pl.ds(pl.multiple_of(j * tn, tn), tn)] = jax.nn.sigmoid(gate) * proj

    @pl.when(j == pl.num_programs(1) - 1)
    def _():
        o_ref[...] = _layer_norm_rows(alpha * h_ref[...] + o_ref[...], g_ref, b_ref)


def _ple_ln(x_bf, p, wg_bf, wp_bf, h, g, b, *, alpha, tm=512, tn=512):
    T, D = x_bf.shape
    row = pl.BlockSpec((tm, D), lambda i, j: (i, 0))
    vec = pl.BlockSpec((1, D), lambda i, j: (0, 0))
    resid = pl.BlockSpec((tm, D), lambda i, j: (i, 0), pipeline_mode=pl.Buffered(1))
    return pl.pallas_call(
        functools.partial(_ple_ln_kernel, alpha=alpha, tn=tn),
        grid=(T // tm, D // tn),
        in_specs=[
            row,
            pl.BlockSpec((tm, PLE_DIM), lambda i, j: (i, 0)),
            pl.BlockSpec((D, tn), lambda i, j: (0, j)),
            pl.BlockSpec((PLE_DIM, tn), lambda i, j: (0, j)),
            resid, vec, vec,
        ],
        out_specs=row,
        out_shape=jax.ShapeDtypeStruct((T, D), F32),
        compiler_params=_params(("parallel", "arbitrary")),
        name="ple_ln",
    )(x_bf, p, wg_bf, wp_bf, h, g.reshape(1, D), b.reshape(1, D))


def _proj_ln_kernel(x_ref, w_ref, h_ref, g_ref, b_ref, o_ref, obf_ref, *, alpha, tn):
    j = pl.program_id(1)
    o_ref[:, pl.ds(pl.multiple_of(j * tn, tn), tn)] = jnp.dot(
        x_ref[...], w_ref[...], preferred_element_type=F32)

    @pl.when(j == pl.num_programs(1) - 1)
    def _():
        out = _layer_norm_rows(alpha * h_ref[...] + o_ref[...], g_ref, b_ref)
        o_ref[...] = out
        obf_ref[...] = out.astype(BF16)


def _proj_ln(x_bf, w_bf, h, g, b, *, alpha, tm=512, tn=512):
    T, D = h.shape
    K = x_bf.shape[1]
    row = pl.BlockSpec((tm, D), lambda i, j: (i, 0))
    vec = pl.BlockSpec((1, D), lambda i, j: (0, 0))
    resid = pl.BlockSpec((tm, D), lambda i, j: (i, 0), pipeline_mode=pl.Buffered(1))
    return pl.pallas_call(
        functools.partial(_proj_ln_kernel, alpha=alpha, tn=tn),
        grid=(T // tm, D // tn),
        in_specs=[
            pl.BlockSpec((tm, K), lambda i, j: (i, 0)),
            pl.BlockSpec((K, tn), lambda i, j: (0, j)),
            resid, vec, vec,
        ],
        out_specs=[row, row],
        out_shape=[jax.ShapeDtypeStruct((T, D), F32), jax.ShapeDtypeStruct((T, D), BF16)],
        compiler_params=_params(("parallel", "arbitrary"), V7X_VMEM_REQUEST_ROWS),
        name="mix_out_ln",
    )(x_bf, w_bf, h, g.reshape(1, D), b.reshape(1, D))


def _alibi_slopes(n):
    return jnp.exp2(-8.0 * jnp.arange(1, n + 1, dtype=F32) / n)


def _bf16_pieces(v):
    pieces, rest = [], v
    for _ in range(COEF_PIECES):
        piece = rest.astype(BF16).astype(F32)
        pieces.append(piece)
        rest = rest - piece
    return jnp.stack(pieces, axis=-1)


def kernel(x, p, ffn1_w_in, ffn1_w_out, ln1_g, ln1_b, w_in, lam_q1, lam_k1, lam_q2, lam_k2,
           subln_g, w_branch_diff, w_branch_dil, w_mix_out, ln2_g, ln2_b, ffn2_w_in,
           ffn2_w_out, ln3_g, ln3_b, w_ple_gate, w_ple_proj, ln4_g, ln4_b):
    batch, seq, d = x.shape
    depth = ffn1_w_in.shape[0]
    T = batch * seq
    alpha = (2 * depth) ** 0.25
    n_pat = len(DIL_PATTERNS)
    tn = 512

    h = x.reshape(T, d)
    h_bf = h.astype(BF16)
    diff_coefs = _bf16_pieces(_alibi_slopes(DIFF_HEADS) * LOG2E)
    dil_slopes = _alibi_slopes(n_pat * DIL_GROUP_HEADS).reshape(DIL_GROUP_HEADS, n_pat)
    colscale = jnp.ones((Z_W,), F32).at[Z_DQ:Z_DK].set(DIFF_HEAD_DIM ** -0.5 * LOG2E)

    for i in range(depth):
        lambda_init = 0.8 - 0.6 * math.exp(-0.3 * i)

        f = _ffn(h_bf, ffn1_w_in[i].astype(BF16), ffn1_w_out[i].astype(BF16))
        h, h_bf = _res_ln(h, f, ln1_g[i], ln1_b[i], alpha=alpha, scale=0.5)

        z = _matmul(h_bf, w_in[i], BF16, n_out=Z_W, wcol=_main_wcol(tn), colscale=colscale,
                    tn=tn, name="in_proj")
        lam_rows = jnp.stack([lam_q1[i], lam_k1[i], lam_q2[i], lam_k2[i]]).astype(F32)
        oa = _diff_attention(z, diff_coefs, lam_rows, subln_g[i].astype(F32),
                             batch=batch, seq=seq, lambda_init=lambda_init)

        dil_outs, dil_lses = [], []
        for g, (_window, dil) in enumerate(DIL_PATTERNS):
            slopes_g = dil_slopes[:, g] * dil
            if dil == 1:
                o_g, lse_g = _dilated_attention(z, (Z_LQ, Z_LK, Z_LV), slopes_g, dil=dil,
                                                batch=batch, seq=seq)
            else:
                qkv = _matmul(_to_class_major(h_bf, batch, seq, dil), w_in[i], BF16,
                              n_out=3 * DIL_OUT_W, wcol=_group_wcol(g, tn), tn=tn,
                              name=f"in_proj_dil{dil}")
                o_g, lse_g = _dilated_attention(qkv, (0, DIL_OUT_W, 2 * DIL_OUT_W), slopes_g,
                                                dil=dil, batch=batch, seq=seq)
                o_g = _to_token_order(o_g, batch, seq, dil)
                lse_g = _to_token_order(lse_g, batch, seq, dil)
            dil_outs.append(o_g)
            dil_lses.append(lse_g)

        y = _merge(oa, dil_outs, dil_lses, z, w_branch_diff[i].astype(BF16),
                   w_branch_dil[i].astype(BF16))
        h, h_bf = _proj_ln(y, w_mix_out[i].astype(BF16), h, ln2_g[i], ln2_b[i], alpha=alpha)

        f = _ffn(h_bf, ffn2_w_in[i].astype(BF16), ffn2_w_out[i].astype(BF16))
        h, h_bf = _res_ln(h, f, ln3_g[i], ln3_b[i], alpha=alpha, scale=0.5)

        h = _ple_ln(h_bf, p[i].reshape(T, PLE_DIM), w_ple_gate[i].astype(BF16),
                    w_ple_proj[i].astype(BF16), h, ln4_g[i], ln4_b[i], alpha=alpha)
        if i + 1 < depth:
            h_bf = h.astype(BF16)

    return h.reshape(batch, seq, d)
```

```python
import functools
import math

import jax
import jax.numpy as jnp
from jax import lax
from jax.experimental import pallas as pl
from jax.experimental.pallas import tpu as pltpu

F32 = jnp.float32
BF16 = jnp.bfloat16

D_MODEL = 4096
PLE_DIM = 256
D_FF = 11008
DIFF_HEADS = 8
DIFF_HEAD_DIM = 128
DIL_PATTERNS = ((128, 1), (512, 4), (2048, 16))
DIL_GROUP_HEADS = 8
DIL_HEAD_DIM = 128
DIL_HALF_KEYS = 64
LN_EPS = 1e-5
SUBLN_EPS = 1e-5
NEG_INF = -1e30

DIFF_QK_W = DIFF_HEADS * 2 * DIFF_HEAD_DIM
DIFF_V_W = DIFF_HEADS * 2 * DIFF_HEAD_DIM
DIL_W = len(DIL_PATTERNS) * DIL_GROUP_HEADS * DIL_HEAD_DIM
DIL_OUT_W = DIL_GROUP_HEADS * DIL_HEAD_DIM
IN_PROJ_W = 2 * DIFF_QK_W + DIFF_V_W + 3 * DIL_W + 2 * D_MODEL

OFF_DQ = 0
OFF_DK = OFF_DQ + DIFF_QK_W
OFF_DV = OFF_DK + DIFF_QK_W
OFF_LQ = OFF_DV + DIFF_V_W
OFF_LK = OFF_LQ + DIL_W
OFF_LV = OFF_LK + DIL_W
OFF_GA = OFF_LV + DIL_W
OFF_GB = OFF_GA + D_MODEL

Z_DQ = 0
Z_DK = Z_DQ + DIFF_QK_W
Z_DV = Z_DK + DIFF_QK_W
Z_LQ = Z_DV + DIFF_V_W
Z_LK = Z_LQ + DIL_OUT_W
Z_LV = Z_LK + DIL_OUT_W
Z_GA = Z_LV + DIL_OUT_W
Z_GB = Z_GA + D_MODEL
Z_W = Z_GB + D_MODEL

V7X_LANES = 128
V7X_VMEM_BYTES = 64 * 1024 * 1024
V7X_VMEM_REQUEST = 56 * 1024 * 1024

FFN_BLOCK = 256

LOG2E = math.log2(math.e)
COEF_PIECES = 3
POS_SPLIT = 64


def _params(semantics, vmem_bytes=V7X_VMEM_REQUEST):
    return pltpu.CompilerParams(dimension_semantics=semantics, vmem_limit_bytes=vmem_bytes)


def _ffn_kernel(x_ref, wga_ref, wgb_ref, wua_ref, wub_ref, woa_ref, wob_ref, o_ref, *, n_blocks):
    j = pl.program_id(1)

    @pl.when(j == 0)
    def _():
        o_ref[...] = jnp.zeros(o_ref.shape, F32)

    x = x_ref[...]

    def hidden(wg_ref, wu_ref):
        g = jnp.dot(x, wg_ref[...], preferred_element_type=F32)
        u = jnp.dot(x, wu_ref[...], preferred_element_type=F32)
        return g * jax.nn.sigmoid(g) * u

    ha = hidden(wga_ref, wua_ref).astype(BF16)
    hb = jnp.where(2 * j + 1 < n_blocks, hidden(wgb_ref, wub_ref), 0.0).astype(BF16)
    part = jnp.dot(ha, woa_ref[...], preferred_element_type=F32)
    part += jnp.dot(hb, wob_ref[...], preferred_element_type=F32)
    o_ref[...] += part


def _ffn(x_bf, w_in_bf, w_out_bf, *, tm=512, tb=FFN_BLOCK):
    T, D = x_bf.shape
    n_blocks = w_out_bf.shape[0] // tb
    steps = -(-n_blocks // 2)

    def first(j):
        return 2 * j

    def second(j):
        return jnp.minimum(2 * j + 1, n_blocks - 1)

    def col_spec(blk, off):
        return pl.BlockSpec((D, tb), lambda i, j: (0, off + blk(j)))

    def row_spec(blk):
        return pl.BlockSpec((tb, D), lambda i, j: (blk(j), 0))

    return pl.pallas_call(
        functools.partial(_ffn_kernel, n_blocks=n_blocks),
        grid=(T // tm, steps),
        in_specs=[
            pl.BlockSpec((tm, D), lambda i, j: (i, 0)),
            col_spec(first, 0), col_spec(second, 0),
            col_spec(first, n_blocks), col_spec(second, n_blocks),
            row_spec(first), row_spec(second),
        ],
        out_specs=pl.BlockSpec((tm, D), lambda i, j: (i, 0)),
        out_shape=jax.ShapeDtypeStruct((T, D), F32),
        compiler_params=_params(("parallel", "arbitrary")),
        name="ffn",
    )(x_bf, w_in_bf, w_in_bf, w_in_bf, w_in_bf, w_out_bf, w_out_bf)


def _layer_norm_rows(y, g_ref, b_ref):
    mu = jnp.mean(y, axis=-1, keepdims=True)
    yc = y - mu
    var = jnp.mean(yc * yc, axis=-1, keepdims=True)
    return yc * lax.rsqrt(var + LN_EPS) * g_ref[...] + b_ref[...]


def _ln_kernel(h_ref, br_ref, g_ref, b_ref, o_ref, obf_ref, *, alpha, scale):
    out = _layer_norm_rows(alpha * h_ref[...] + scale * br_ref[...], g_ref, b_ref)
    o_ref[...] = out
    obf_ref[...] = out.astype(BF16)


def _ln_class_major_kernel(h_ref, br_ref, g_ref, b_ref, o_ref, obf_ref, *rest, alpha, scale, dils):
    cm_refs, slab_sc = rest[:-1], rest[-1]
    out = _layer_norm_rows(alpha * h_ref[...] + scale * br_ref[...], g_ref, b_ref)
    o_ref[...] = out
    obf_ref[...] = out.astype(BF16)
    tm, d = out.shape
    for s in range(d // V7X_LANES):
        slab_sc[s] = out[:, s * V7X_LANES:(s + 1) * V7X_LANES]
    for dil, cm_ref in zip(dils, cm_refs):
        for r in range(dil):
            for s in range(d // V7X_LANES):
                cm_ref[r, :, s * V7X_LANES:(s + 1) * V7X_LANES] = slab_sc[
                    s, pl.ds(r, tm // dil, stride=dil), :].astype(BF16)


def _res_ln_class_major(h, branch, g, b, *, alpha, scale, batch, seq, dils, tm=256):
    T, D = h.shape
    tiles = seq // tm
    row = pl.BlockSpec((tm, D), lambda i: (i, 0))
    vec = pl.BlockSpec((1, D), lambda i: (0, 0))
    cm_specs = [pl.BlockSpec((None, dil, tm // dil, D), lambda i: (i // tiles, 0, i % tiles, 0))
                for dil in dils]
    cm_shapes = [jax.ShapeDtypeStruct((batch, dil, seq // dil, D), BF16) for dil in dils]
    outs = pl.pallas_call(
        functools.partial(_ln_class_major_kernel, alpha=alpha, scale=scale, dils=dils),
        grid=(T // tm,),
        in_specs=[row, row, vec, vec],
        out_specs=[row, row] + cm_specs,
        out_shape=[jax.ShapeDtypeStruct((T, D), F32), jax.ShapeDtypeStruct((T, D), BF16)]
        + cm_shapes,
        scratch_shapes=[pltpu.VMEM((D // V7X_LANES, tm, V7X_LANES), F32)],
        compiler_params=_params(("parallel",)),
        name="res_ln_class_major",
    )(h, branch, g.reshape(1, D), b.reshape(1, D))
    return outs[0], outs[1], [cm.reshape(T, D) for cm in outs[2:]]


def _res_ln(h, branch, g, b, *, alpha, scale, tm=256):
    T, D = h.shape
    row = pl.BlockSpec((tm, D), lambda i: (i, 0))
    vec = pl.BlockSpec((1, D), lambda i: (0, 0))
    return pl.pallas_call(
        functools.partial(_ln_kernel, alpha=alpha, scale=scale),
        grid=(T // tm,),
        in_specs=[row, row, vec, vec],
        out_specs=[row, row],
        out_shape=[jax.ShapeDtypeStruct((T, D), F32), jax.ShapeDtypeStruct((T, D), BF16)],
        compiler_params=_params(("parallel",)),
        name="res_ln",
    )(h, branch, g.reshape(1, D), b.reshape(1, D))


def _matmul_kernel(x_ref, w_ref, o_ref):
    acc = jnp.dot(x_ref[...], w_ref[...].astype(BF16), preferred_element_type=F32)
    o_ref[...] = acc.astype(o_ref.dtype)


def _matmul_scaled_kernel(x_ref, w_ref, s_ref, o_ref):
    acc = jnp.dot(x_ref[...], w_ref[...].astype(BF16), preferred_element_type=F32)
    o_ref[...] = (acc * s_ref[...]).astype(o_ref.dtype)


def _matmul(x_bf, w, out_dtype, *, n_out, wcol, colscale=None, tm=1024, tn=512, name="matmul"):
    T, K = x_bf.shape
    in_specs = [
        pl.BlockSpec((tm, K), lambda i, j: (i, 0)),
        pl.BlockSpec((K, tn), lambda i, j: (0, wcol(j))),
    ]
    args = [x_bf, w]
    body = _matmul_kernel
    if colscale is not None:
        in_specs.append(pl.BlockSpec((1, tn), lambda i, j: (0, j)))
        args.append(colscale.reshape(1, n_out))
        body = _matmul_scaled_kernel
    return pl.pallas_call(
        body,
        grid=(T // tm, n_out // tn),
        in_specs=in_specs,
        out_specs=pl.BlockSpec((tm, tn), lambda i, j: (i, j)),
        out_shape=jax.ShapeDtypeStruct((T, n_out), out_dtype),
        compiler_params=_params(("parallel", "parallel")),
        name=name,
    )(*args)


def _main_wcol(tn):
    per_sec = DIL_OUT_W // tn
    first = OFF_LQ // tn
    n_pat = len(DIL_PATTERNS)

    def wcol(j):
        k = j - first
        dil_tile = first + (k // per_sec) * (n_pat * per_sec) + k % per_sec
        gate_tile = j + (n_pat - 1) * 3 * per_sec
        return jnp.where(j < first, j, jnp.where(k < 3 * per_sec, dil_tile, gate_tile))

    return wcol


def _group_wcol(group, tn):
    per_sec = DIL_OUT_W // tn

    def wcol(j):
        return (OFF_LQ + group * DIL_OUT_W) // tn + (j // per_sec) * (DIL_W // tn) + j % per_sec

    return wcol


def _alibi_columns(pos, coefs, q_side):
    lane = lax.broadcasted_iota(jnp.int32, pos.shape, 1)
    piece = jnp.where(lane < 3, lane, jnp.where(lane < 6, lane - 3,
                                                jnp.where(lane < 9, lane - 6, lane - 9)))
    cv = jnp.where(piece == 0, coefs[0], jnp.where(piece == 1, coefs[1], coefs[2]))
    hi = (pos - (pos & (POS_SPLIT - 1))).astype(F32)
    lo = (pos & (POS_SPLIT - 1)).astype(F32)
    n = 2 * COEF_PIECES
    if q_side:
        cols = jnp.where(lane < COEF_PIECES, -hi, jnp.where(lane < n, -lo, cv))
    else:
        cols = jnp.where(lane < n, cv, jnp.where(lane < n + COEF_PIECES, hi, lo))
    return jnp.where(lane < 2 * n, cols, 0.0)


def _diff_attn_kernel(coef_ref, lam_ref, g_ref, q_ref, k_ref, v_ref, o_ref,
                      kaug_sc, qaug_sc, corr_sc, sa_sc, sb_sc, p_sc, m_sc, l_sc, alpha_sc, acc_sc,
                      *, tq, rb, seq, lambda_init):
    h = pl.program_id(1)
    qi = pl.program_id(2)
    dh = DIFF_HEAD_DIM
    n_chunks = seq // tq
    coefs = [coef_ref[h, t] for t in range(COEF_PIECES)]
    row_iota = lax.broadcasted_iota(jnp.int32, (tq, dh), 0)

    @pl.when(qi == 0)
    def _():
        def build(t, carry):
            r0 = pl.multiple_of(t * tq, tq)
            ak = _alibi_columns(row_iota + r0, coefs, q_side=False).astype(BF16)
            for c in range(2):
                kaug_sc[c, pl.ds(r0, tq), 0:dh] = k_ref[pl.ds(r0, tq), c * dh:(c + 1) * dh]
                kaug_sc[c, pl.ds(r0, tq), dh:2 * dh] = ak
            return carry

        lax.fori_loop(0, n_chunks, build, 0)
        row = lax.broadcasted_iota(jnp.int32, (tq, tq), 0)
        col = lax.broadcasted_iota(jnp.int32, (tq, tq), 1)
        slope2 = coefs[0] + coefs[1] + coefs[2]
        corr_sc[0] = jnp.zeros((tq, tq), F32)
        corr_sc[1] = (2.0 * slope2) * jnp.minimum(row - col, 0).astype(F32)

    aq = _alibi_columns(row_iota + qi * tq, coefs, q_side=True)
    for c in range(2):
        q_c = q_ref[:, c * dh:(c + 1) * dh]
        qaug_sc[0, c, :, 0:dh] = q_c
        qaug_sc[0, c, :, dh:2 * dh] = aq.astype(BF16)
        qaug_sc[1, c, :, 0:dh] = q_c
        qaug_sc[1, c, :, dh:2 * dh] = (-aq).astype(BF16)

    m_sc[...] = jnp.full(m_sc.shape, -jnp.inf, F32)
    l_sc[...] = jnp.zeros(l_sc.shape, F32)
    acc_sc[...] = jnp.zeros(acc_sc.shape, F32)

    n_slabs = tq // V7X_LANES
    last = n_chunks - 1

    def scores(kc, s_ref):
        ks = pl.multiple_of(kc * tq, tq)
        side = jnp.where(kc > qi, 1, 0)
        for c in range(2):
            s_ref[c] = lax.dot_general(
                qaug_sc[side, c], kaug_sc[c, pl.ds(ks, tq), :],
                (((1,), (1,)), ((), ())), preferred_element_type=F32)

    def absorb(kc, s_ref, near_diag):
        ks = pl.multiple_of(kc * tq, tq)
        on_diag = jnp.where(kc == qi, 1, 0)

        def block_slabs(c, rows):
            sb = s_ref[c, rows, :]
            if near_diag:
                sb = sb + corr_sc[on_diag, rows, :]
            return [sb[:, t * V7X_LANES:(t + 1) * V7X_LANES] for t in range(n_slabs)]

        for c in range(2):
            for b in range(tq // rb):
                rows = slice(b * rb, (b + 1) * rb)
                slabs = block_slabs(c, rows)
                mx = functools.reduce(jnp.maximum, slabs)
                m_old = m_sc[c, rows, :]
                m_new = jnp.maximum(m_old, jnp.broadcast_to(
                    jnp.max(mx, axis=-1, keepdims=True), (rb, V7X_LANES)))
                a = jnp.exp2(m_old - m_new)
                ps = [jnp.exp2(sl - m_new) for sl in slabs]
                l_sc[c, rows, :] = a * l_sc[c, rows, :] + functools.reduce(jnp.add, ps)
                p_sc[c, rows, :] = jnp.concatenate(ps, axis=-1).astype(BF16)
                alpha_sc[c, rows, :] = a
                m_sc[c, rows, :] = m_new
            alpha = alpha_sc[c]
            acc_sc[c] = jnp.concatenate([alpha, alpha], axis=-1) * acc_sc[c] + jnp.dot(
                p_sc[c], v_ref[pl.ds(ks, tq), :], preferred_element_type=F32)

    def pair_body(kc, near_diag):
        scores(kc + 1, sb_sc)
        absorb(kc, sa_sc, near_diag)
        scores(jnp.minimum(kc + 2, last), sa_sc)
        absorb(kc + 1, sb_sc, near_diag)

    def pair(t, carry):
        kc = 2 * t
        lax.cond(t == qi // 2,
                 functools.partial(pair_body, kc, True),
                 functools.partial(pair_body, kc, False))
        return carry

    scores(0, sa_sc)
    lax.fori_loop(0, n_chunks // 2, pair, 0)

    lam_rows = lam_ref[...]
    lam = (jnp.exp(jnp.sum(lam_rows[0:1] * lam_rows[1:2], axis=-1, keepdims=True))
           - jnp.exp(jnp.sum(lam_rows[2:3] * lam_rows[3:4], axis=-1, keepdims=True))
           + lambda_init)
    l0 = jnp.sum(l_sc[0], axis=-1, keepdims=True)
    l1 = jnp.sum(l_sc[1], axis=-1, keepdims=True)
    o = acc_sc[0] / l0 - lam * (acc_sc[1] / l1)
    ms = jnp.mean(o * o, axis=-1, keepdims=True)
    o = o * lax.rsqrt(ms + SUBLN_EPS) * g_ref[...] * (1.0 - lambda_init)
    o_ref[...] = o.astype(o_ref.dtype)


def _diff_attention(z, coefs, lam_rows, subln_g, *, batch, seq, lambda_init, tq=512, rb=32):
    T = z.shape[0]
    e = 2 * DIFF_HEAD_DIM
    nq = seq // tq
    qspec = pl.BlockSpec((tq, e), lambda b, h, i: (b * nq + i, Z_DQ // e + h))
    kspec = pl.BlockSpec((seq, e), lambda b, h, i: (b, Z_DK // e + h))
    vspec = pl.BlockSpec((seq, e), lambda b, h, i: (b, Z_DV // e + h))
    return pl.pallas_call(
        functools.partial(_diff_attn_kernel, tq=tq, rb=rb, seq=seq, lambda_init=lambda_init),
        grid=(batch, DIFF_HEADS, nq),
        in_specs=[
            pl.BlockSpec(memory_space=pltpu.SMEM),
            pl.BlockSpec((4, DIFF_HEAD_DIM), lambda b, h, i: (0, 0)),
            pl.BlockSpec((1, e), lambda b, h, i: (0, 0)),
            qspec, kspec, vspec,
        ],
        out_specs=pl.BlockSpec((tq, e), lambda b, h, i: (b * nq + i, h)),
        out_shape=jax.ShapeDtypeStruct((T, DIFF_V_W), BF16),
        scratch_shapes=[
            pltpu.VMEM((2, seq, e), BF16),
            pltpu.VMEM((2, 2, tq, e), BF16),
            pltpu.VMEM((2, tq, tq), F32),
            pltpu.VMEM((2, tq, tq), F32),
            pltpu.VMEM((2, tq, tq), F32),
            pltpu.VMEM((2, tq, tq), BF16),
            pltpu.VMEM((2, tq, V7X_LANES), F32),
            pltpu.VMEM((2, tq, V7X_LANES), F32),
            pltpu.VMEM((2, tq, V7X_LANES), F32),
            pltpu.VMEM((2, tq, e), F32),
        ],
        compiler_params=_params(("parallel", "parallel", "arbitrary")),
        name="diff_attn",
    )(coefs, lam_rows, subln_g.reshape(1, e), z, z, z)


def _dil_attn_kernel(slopes_ref, q_ref, k_ref, v_ref, o_ref, lse_ref, *, seq, dil, tq, unroll):
    h = pl.program_id(1)
    slope = slopes_ref[h]
    w = DIL_HALF_KEYS
    nk = tq + 2 * w
    length = seq // dil
    blocks_per_class = length // tq
    scale = DIL_HEAD_DIM ** -0.5
    row = lax.broadcasted_iota(jnp.int32, (tq, nk), 0)
    col = lax.broadcasted_iota(jnp.int32, (tq, nk), 1)
    col_minus_row = col - row

    def body(i, carry):
        cls = i // blocks_per_class
        cls_base = cls * length
        qs_local = (i % blocks_per_class) * tq
        ks_local = jnp.clip(qs_local - w, 0, length - nk)
        qs = pl.multiple_of(cls_base + qs_local, tq)
        ks = pl.multiple_of(cls_base + ks_local, w)
        rel = jnp.abs(col_minus_row + (ks_local - qs_local))
        q = q_ref[pl.ds(qs, tq), :]
        k = k_ref[pl.ds(ks, nk), :]
        v = v_ref[pl.ds(ks, nk), :]
        s = lax.dot_general(q, k, (((1,), (1,)), ((), ())), preferred_element_type=F32) * scale
        s = jnp.where(rel <= w, s - slope * rel.astype(F32), NEG_INF)
        m = jnp.max(s, axis=-1, keepdims=True)
        e = jnp.exp(s - m)
        z = jnp.sum(e, axis=-1, keepdims=True)
        o = jnp.dot(e.astype(BF16), v, preferred_element_type=F32) / z
        rows = pl.ds(qs, tq) if dil == 1 else pl.ds(qs_local * dil + cls, tq, stride=dil)
        o_ref[rows, :] = o
        lse_ref[rows, :] = jnp.broadcast_to(m + jnp.log(z), (tq, DIL_HEAD_DIM))
        return carry

    lax.fori_loop(0, seq // tq, body, 0, unroll=unroll)


def _dilated_attention(qkv, offsets, slopes_g, *, dil, batch, seq, tq=128, unroll=8):
    T = qkv.shape[0]
    hd = DIL_HEAD_DIM

    def in_spec(off):
        return pl.BlockSpec((seq, hd), lambda b, h: (b, off // hd + h))

    out_spec = pl.BlockSpec((seq, hd), lambda b, h: (b, h))
    out_sds = jax.ShapeDtypeStruct((T, DIL_OUT_W), F32)
    return pl.pallas_call(
        functools.partial(_dil_attn_kernel, seq=seq, dil=dil, tq=tq, unroll=unroll),
        grid=(batch, DIL_GROUP_HEADS),
        in_specs=[pl.BlockSpec(memory_space=pltpu.SMEM)] + [in_spec(o) for o in offsets],
        out_specs=[out_spec, out_spec],
        out_shape=[out_sds, out_sds],
        compiler_params=_params(("parallel", "parallel")),
        name=f"dil_attn_{dil}",
    )(slopes_g, qkv, qkv, qkv)


def _merge_kernel(oa_ref, o0_ref, o1_ref, o2_ref, l0_ref, l1_ref, l2_ref, ga_ref, gb_ref,
                  wa_ref, wb_ref, y_ref, ob_sc):
    @pl.when(pl.program_id(1) == 0)
    def _():
        l0, l1, l2 = l0_ref[...], l1_ref[...], l2_ref[...]
        m = jnp.maximum(jnp.maximum(l0, l1), l2)
        w0, w1, w2 = jnp.exp(l0 - m), jnp.exp(l1 - m), jnp.exp(l2 - m)
        ob = (w0 * o0_ref[...] + w1 * o1_ref[...] + w2 * o2_ref[...]) / (w0 + w1 + w2)
        ob_sc[...] = ob.astype(BF16)

    ya = jnp.dot(oa_ref[...], wa_ref[...], preferred_element_type=F32)
    yb = jnp.dot(ob_sc[...], wb_ref[...], preferred_element_type=F32)
    y = (jax.nn.sigmoid(ga_ref[...].astype(F32)) * ya
         + jax.nn.sigmoid(gb_ref[...].astype(F32)) * yb)
    y_ref[...] = y.astype(y_ref.dtype)


def _merge(oa, dil_outs, dil_lses, z, wa_bf, wb_bf, *, tm=512, tn=512):
    T = oa.shape[0]
    row_a = pl.BlockSpec((tm, DIFF_V_W), lambda i, j: (i, 0))
    row_b = pl.BlockSpec((tm, DIL_OUT_W), lambda i, j: (i, 0))
    return pl.pallas_call(
        _merge_kernel,
        grid=(T // tm, D_MODEL // tn),
        in_specs=[row_a, row_b, row_b, row_b, row_b, row_b, row_b,
                  pl.BlockSpec((tm, tn), lambda i, j: (i, Z_GA // tn + j)),
                  pl.BlockSpec((tm, tn), lambda i, j: (i, Z_GB // tn + j)),
                  pl.BlockSpec((DIFF_V_W, tn), lambda i, j: (0, j)),
                  pl.BlockSpec((DIL_OUT_W, tn), lambda i, j: (0, j))],
        out_specs=pl.BlockSpec((tm, tn), lambda i, j: (i, j)),
        out_shape=jax.ShapeDtypeStruct((T, D_MODEL), BF16),
        scratch_shapes=[pltpu.VMEM((tm, DIL_OUT_W), BF16)],
        compiler_params=_params(("parallel", "arbitrary")),
        name="merge",
    )(oa, *dil_outs, *dil_lses, z, z, wa_bf, wb_bf)


def _ple_kernel(x_ref, p_ref, wg_ref, wp_ref, o_ref):
    gate = jnp.dot(x_ref[...], wg_ref[...].astype(BF16), preferred_element_type=F32)
    proj = jnp.dot(p_ref[...].astype(BF16), wp_ref[...].astype(BF16),
                   preferred_element_type=F32)
    o_ref[...] = jax.nn.sigmoid(gate) * proj


def _ple(x_bf, p, wg, wp, *, tm=1024, tn=512):
    T, D = x_bf.shape
    return pl.pallas_call(
        _ple_kernel,
        grid=(T // tm, D // tn),
        in_specs=[
            pl.BlockSpec((tm, D), lambda i, j: (i, 0)),
            pl.BlockSpec((tm, PLE_DIM), lambda i, j: (i, 0)),
            pl.BlockSpec((D, tn), lambda i, j: (0, j)),
            pl.BlockSpec((PLE_DIM, tn), lambda i, j: (0, j)),
        ],
        out_specs=pl.BlockSpec((tm, tn), lambda i, j: (i, j)),
        out_shape=jax.ShapeDtypeStruct((T, D), F32),
        compiler_params=_params(("parallel", "parallel")),
        name="ple",
    )(x_bf, p, wg, wp)


def _alibi_slopes(n):
    return jnp.exp2(-8.0 * jnp.arange(1, n + 1, dtype=F32) / n)


def _bf16_pieces(v):
    pieces, rest = [], v
    for _ in range(COEF_PIECES):
        piece = rest.astype(BF16).astype(F32)
        pieces.append(piece)
        rest = rest - piece
    return jnp.stack(pieces, axis=-1)


def kernel(x, p, ffn1_w_in, ffn1_w_out, ln1_g, ln1_b, w_in, lam_q1, lam_k1, lam_q2, lam_k2,
           subln_g, w_branch_diff, w_branch_dil, w_mix_out, ln2_g, ln2_b, ffn2_w_in,
           ffn2_w_out, ln3_g, ln3_b, w_ple_gate, w_ple_proj, ln4_g, ln4_b):
    batch, seq, d = x.shape
    depth = ffn1_w_in.shape[0]
    T = batch * seq
    alpha = (2 * depth) ** 0.25
    n_pat = len(DIL_PATTERNS)
    tn = 512

    h = x.reshape(T, d)
    h_bf = h.astype(BF16)
    diff_coefs = _bf16_pieces(_alibi_slopes(DIFF_HEADS) * LOG2E)
    dil_slopes = _alibi_slopes(n_pat * DIL_GROUP_HEADS).reshape(DIL_GROUP_HEADS, n_pat)
    colscale = jnp.ones((Z_W,), F32).at[Z_DQ:Z_DK].set(DIFF_HEAD_DIM ** -0.5 * LOG2E)

    for i in range(depth):
        lambda_init = 0.8 - 0.6 * math.exp(-0.3 * i)

        f = _ffn(h_bf, ffn1_w_in[i].astype(BF16), ffn1_w_out[i].astype(BF16))
        dils = tuple(dil for _window, dil in DIL_PATTERNS if dil > 1)
        h, h_bf, h_cm = _res_ln_class_major(h, f, ln1_g[i], ln1_b[i], alpha=alpha, scale=0.5,
                                            batch=batch, seq=seq, dils=dils)
        h_class_major = dict(zip(dils, h_cm))

        z = _matmul(h_bf, w_in[i], BF16, n_out=Z_W, wcol=_main_wcol(tn), colscale=colscale,
                    tn=tn, name="in_proj")
        lam_rows = jnp.stack([lam_q1[i], lam_k1[i], lam_q2[i], lam_k2[i]]).astype(F32)
        oa = _diff_attention(z, diff_coefs, lam_rows, subln_g[i].astype(F32),
                             batch=batch, seq=seq, lambda_init=lambda_init)

        dil_outs, dil_lses = [], []
        for g, (_window, dil) in enumerate(DIL_PATTERNS):
            slopes_g = dil_slopes[:, g] * dil
            if dil == 1:
                o_g, lse_g = _dilated_attention(z, (Z_LQ, Z_LK, Z_LV), slopes_g, dil=dil,
                                                batch=batch, seq=seq)
            else:
                qkv = _matmul(h_class_major[dil], w_in[i], BF16,
                              n_out=3 * DIL_OUT_W, wcol=_group_wcol(g, tn), tn=tn,
                              name=f"in_proj_dil{dil}")
                o_g, lse_g = _dilated_attention(qkv, (0, DIL_OUT_W, 2 * DIL_OUT_W), slopes_g,
                                                dil=dil, batch=batch, seq=seq)
            dil_outs.append(o_g)
            dil_lses.append(lse_g)

        y = _merge(oa, dil_outs, dil_lses, z, w_branch_diff[i].astype(BF16),
                   w_branch_dil[i].astype(BF16))
        mix = _matmul(y, w_mix_out[i], F32, n_out=d, wcol=lambda j: j, tn=tn, name="mix_out")
        h, h_bf = _res_ln(h, mix, ln2_g[i], ln2_b[i], alpha=alpha, scale=1.0)

        f = _ffn(h_bf, ffn2_w_in[i].astype(BF16), ffn2_w_out[i].astype(BF16))
        h, h_bf = _res_ln(h, f, ln3_g[i], ln3_b[i], alpha=alpha, scale=0.5)

        ple = _ple(h_bf, p[i].reshape(T, PLE_DIM), w_ple_gate[i], w_ple_proj[i])
        h, h_bf = _res_ln(h, ple, ln4_g[i], ln4_b[i], alpha=alpha, scale=1.0)

    return h.reshape(batch, seq, d)
```

```python
import functools
import math

import jax
import jax.numpy as jnp
from jax import lax
from jax.experimental import pallas as pl
from jax.experimental.pallas import tpu as pltpu

F32 = jnp.float32
BF16 = jnp.bfloat16

D_MODEL = 4096
PLE_DIM = 256
D_FF = 11008
DIFF_HEADS = 8
DIFF_HEAD_DIM = 128
DIL_PATTERNS = ((128, 1), (512, 4), (2048, 16))
DIL_GROUP_HEADS = 8
DIL_HEAD_DIM = 128
DIL_HALF_KEYS = 64
LN_EPS = 1e-5
SUBLN_EPS = 1e-5
NEG_INF = -1e30

DIFF_QK_W = DIFF_HEADS * 2 * DIFF_HEAD_DIM
DIFF_V_W = DIFF_HEADS * 2 * DIFF_HEAD_DIM
DIL_W = len(DIL_PATTERNS) * DIL_GROUP_HEADS * DIL_HEAD_DIM
DIL_OUT_W = DIL_GROUP_HEADS * DIL_HEAD_DIM
IN_PROJ_W = 2 * DIFF_QK_W + DIFF_V_W + 3 * DIL_W + 2 * D_MODEL

OFF_DQ = 0
OFF_DK = OFF_DQ + DIFF_QK_W
OFF_DV = OFF_DK + DIFF_QK_W
OFF_LQ = OFF_DV + DIFF_V_W
OFF_LK = OFF_LQ + DIL_W
OFF_LV = OFF_LK + DIL_W
OFF_GA = OFF_LV + DIL_W
OFF_GB = OFF_GA + D_MODEL

Z_DQ = 0
Z_DK = Z_DQ + DIFF_QK_W
Z_DV = Z_DK + DIFF_QK_W
Z_LQ = Z_DV + DIFF_V_W
Z_LK = Z_LQ + DIL_OUT_W
Z_LV = Z_LK + DIL_OUT_W
Z_GA = Z_LV + DIL_OUT_W
Z_GB = Z_GA + D_MODEL
Z_W = Z_GB + D_MODEL

V7X_LANES = 128
V7X_VMEM_BYTES = 64 * 1024 * 1024
V7X_VMEM_REQUEST = 56 * 1024 * 1024

FFN_BLOCK = 256

LOG2E = math.log2(math.e)
COEF_PIECES = 3
POS_SPLIT = 64
RIDER_ROWS = 64


def _params(semantics, vmem_bytes=V7X_VMEM_REQUEST):
    return pltpu.CompilerParams(dimension_semantics=semantics, vmem_limit_bytes=vmem_bytes)


def _ffn_kernel(x_ref, wga_ref, wgb_ref, wua_ref, wub_ref, woa_ref, wob_ref, o_ref, *, n_blocks):
    j = pl.program_id(1)

    @pl.when(j == 0)
    def _():
        o_ref[...] = jnp.zeros(o_ref.shape, F32)

    x = x_ref[...]

    def hidden(wg_ref, wu_ref):
        g = jnp.dot(x, wg_ref[...], preferred_element_type=F32)
        u = jnp.dot(x, wu_ref[...], preferred_element_type=F32)
        return g * jax.nn.sigmoid(g) * u

    ha = hidden(wga_ref, wua_ref).astype(BF16)
    hb = jnp.where(2 * j + 1 < n_blocks, hidden(wgb_ref, wub_ref), 0.0).astype(BF16)
    part = jnp.dot(ha, woa_ref[...], preferred_element_type=F32)
    part += jnp.dot(hb, wob_ref[...], preferred_element_type=F32)
    o_ref[...] += part


def _ffn(x_bf, w_in_bf, w_out_bf, *, tm=512, tb=FFN_BLOCK):
    T, D = x_bf.shape
    n_blocks = w_out_bf.shape[0] // tb
    steps = -(-n_blocks // 2)

    def first(j):
        return 2 * j

    def second(j):
        return jnp.minimum(2 * j + 1, n_blocks - 1)

    def col_spec(blk, off):
        return pl.BlockSpec((D, tb), lambda i, j: (0, off + blk(j)))

    def row_spec(blk):
        return pl.BlockSpec((tb, D), lambda i, j: (blk(j), 0))

    return pl.pallas_call(
        functools.partial(_ffn_kernel, n_blocks=n_blocks),
        grid=(T // tm, steps),
        in_specs=[
            pl.BlockSpec((tm, D), lambda i, j: (i, 0)),
            col_spec(first, 0), col_spec(second, 0),
            col_spec(first, n_blocks), col_spec(second, n_blocks),
            row_spec(first), row_spec(second),
        ],
        out_specs=pl.BlockSpec((tm, D), lambda i, j: (i, 0)),
        out_shape=jax.ShapeDtypeStruct((T, D), F32),
        compiler_params=_params(("parallel", "arbitrary")),
        name="ffn",
    )(x_bf, w_in_bf, w_in_bf, w_in_bf, w_in_bf, w_out_bf, w_out_bf)


def _layer_norm_rows(y, g_ref, b_ref):
    mu = jnp.mean(y, axis=-1, keepdims=True)
    yc = y - mu
    var = jnp.mean(yc * yc, axis=-1, keepdims=True)
    return yc * lax.rsqrt(var + LN_EPS) * g_ref[...] + b_ref[...]


def _ln_kernel(h_ref, br_ref, g_ref, b_ref, o_ref, obf_ref, *, alpha, scale):
    out = _layer_norm_rows(alpha * h_ref[...] + scale * br_ref[...], g_ref, b_ref)
    o_ref[...] = out
    obf_ref[...] = out.astype(BF16)


def _ln_class_major_kernel(h_ref, br_ref, g_ref, b_ref, o_ref, obf_ref, *rest, alpha, scale, dils):
    cm_refs, slab_sc = rest[:-1], rest[-1]
    out = _layer_norm_rows(alpha * h_ref[...] + scale * br_ref[...], g_ref, b_ref)
    o_ref[...] = out
    obf_ref[...] = out.astype(BF16)
    tm, d = out.shape
    for s in range(d // V7X_LANES):
        slab_sc[s] = out[:, s * V7X_LANES:(s + 1) * V7X_LANES]
    for dil, cm_ref in zip(dils, cm_refs):
        for r in range(dil):
            for s in range(d // V7X_LANES):
                cm_ref[r, :, s * V7X_LANES:(s + 1) * V7X_LANES] = slab_sc[
                    s, pl.ds(r, tm // dil, stride=dil), :].astype(BF16)


def _res_ln_class_major(h, branch, g, b, *, alpha, scale, batch, seq, dils, tm=256):
    T, D = h.shape
    tiles = seq // tm
    row = pl.BlockSpec((tm, D), lambda i: (i, 0))
    vec = pl.BlockSpec((1, D), lambda i: (0, 0))
    cm_specs = [pl.BlockSpec((None, dil, tm // dil, D), lambda i: (i // tiles, 0, i % tiles, 0))
                for dil in dils]
    cm_shapes = [jax.ShapeDtypeStruct((batch, dil, seq // dil, D), BF16) for dil in dils]
    outs = pl.pallas_call(
        functools.partial(_ln_class_major_kernel, alpha=alpha, scale=scale, dils=dils),
        grid=(T // tm,),
        in_specs=[row, row, vec, vec],
        out_specs=[row, row] + cm_specs,
        out_shape=[jax.ShapeDtypeStruct((T, D), F32), jax.ShapeDtypeStruct((T, D), BF16)]
        + cm_shapes,
        scratch_shapes=[pltpu.VMEM((D // V7X_LANES, tm, V7X_LANES), F32)],
        compiler_params=_params(("parallel",)),
        name="res_ln_class_major",
    )(h, branch, g.reshape(1, D), b.reshape(1, D))
    return outs[0], outs[1], [cm.reshape(T, D) for cm in outs[2:]]


def _ln_f32_kernel(h_ref, br_ref, g_ref, b_ref, o_ref, *, alpha, scale):
    o_ref[...] = _layer_norm_rows(alpha * h_ref[...] + scale * br_ref[...], g_ref, b_ref)


def _res_ln(h, branch, g, b, *, alpha, scale, with_bf16=True, tm=256):
    T, D = h.shape
    row = pl.BlockSpec((tm, D), lambda i: (i, 0))
    vec = pl.BlockSpec((1, D), lambda i: (0, 0))
    f32_out = jax.ShapeDtypeStruct((T, D), F32)
    return pl.pallas_call(
        functools.partial(_ln_kernel if with_bf16 else _ln_f32_kernel, alpha=alpha, scale=scale),
        grid=(T // tm,),
        in_specs=[row, row, vec, vec],
        out_specs=[row, row] if with_bf16 else row,
        out_shape=[f32_out, jax.ShapeDtypeStruct((T, D), BF16)] if with_bf16 else f32_out,
        compiler_params=_params(("parallel",)),
        name="res_ln",
    )(h, branch, g.reshape(1, D), b.reshape(1, D))


def _matmul_kernel(*refs, scaled, rider_blocks, n_col_tiles):
    n_riders = len(rider_blocks)
    x_ref, w_ref = refs[:2]
    s_ref = refs[2] if scaled else None
    rider_in = refs[2 + scaled:2 + scaled + n_riders]
    o_ref = refs[2 + scaled + n_riders]
    rider_out = refs[3 + scaled + n_riders:]
    acc = jnp.dot(x_ref[...], w_ref[...].astype(BF16), preferred_element_type=F32)
    if scaled:
        acc = acc * s_ref[...]
    o_ref[...] = acc.astype(o_ref.dtype)
    step = pl.program_id(0) * n_col_tiles + pl.program_id(1)
    for src, dst, n_blocks in zip(rider_in, rider_out, rider_blocks):
        @pl.when(step < n_blocks)
        def _(src=src, dst=dst):
            dst[...] = src[...].astype(BF16)


def _matmul(x_bf, w, out_dtype, *, n_out, wcol, colscale=None, riders=(), tm=1024, tn=512,
            name="matmul"):
    T, K = x_bf.shape
    n_col_tiles = n_out // tn
    in_specs = [
        pl.BlockSpec((tm, K), lambda i, j: (i, 0)),
        pl.BlockSpec((K, tn), lambda i, j: (0, wcol(j))),
    ]
    args = [x_bf, w]
    if colscale is not None:
        in_specs.append(pl.BlockSpec((1, tn), lambda i, j: (0, j)))
        args.append(colscale.reshape(1, n_out))
    rider_specs, rider_blocks = [], []
    for arr, block, axis in riders:
        n_blocks = arr.shape[axis] // block[axis]
        assert n_blocks <= (T // tm) * n_col_tiles

        def index_map(i, j, axis=axis, n_blocks=n_blocks):
            blk = jnp.minimum(i * n_col_tiles + j, n_blocks - 1)
            return tuple(blk if a == axis else 0 for a in range(2))

        rider_specs.append(pl.BlockSpec(block, index_map))
        rider_blocks.append(n_blocks)
        args.append(arr)
    outs = pl.pallas_call(
        functools.partial(_matmul_kernel, scaled=colscale is not None,
                          rider_blocks=tuple(rider_blocks), n_col_tiles=n_col_tiles),
        grid=(T // tm, n_col_tiles),
        in_specs=in_specs + rider_specs,
        out_specs=[pl.BlockSpec((tm, tn), lambda i, j: (i, j))] + rider_specs,
        out_shape=[jax.ShapeDtypeStruct((T, n_out), out_dtype)]
        + [jax.ShapeDtypeStruct(arr.shape, BF16) for arr, _, _ in riders],
        compiler_params=_params(("arbitrary", "arbitrary")),
        name=name,
    )(*args)
    return outs if riders else outs[0]


def _main_wcol(tn):
    per_sec = DIL_OUT_W // tn
    first = OFF_LQ // tn
    n_pat = len(DIL_PATTERNS)

    def wcol(j):
        k = j - first
        dil_tile = first + (k // per_sec) * (n_pat * per_sec) + k % per_sec
        gate_tile = j + (n_pat - 1) * 3 * per_sec
        return jnp.where(j < first, j, jnp.where(k < 3 * per_sec, dil_tile, gate_tile))

    return wcol


def _group_wcol(group, tn):
    per_sec = DIL_OUT_W // tn

    def wcol(j):
        return (OFF_LQ + group * DIL_OUT_W) // tn + (j // per_sec) * (DIL_W // tn) + j % per_sec

    return wcol


def _alibi_columns(pos, coefs, q_side):
    lane = lax.broadcasted_iota(jnp.int32, pos.shape, 1)
    piece = jnp.where(lane < 3, lane, jnp.where(lane < 6, lane - 3,
                                                jnp.where(lane < 9, lane - 6, lane - 9)))
    cv = jnp.where(piece == 0, coefs[0], jnp.where(piece == 1, coefs[1], coefs[2]))
    hi = (pos - (pos & (POS_SPLIT - 1))).astype(F32)
    lo = (pos & (POS_SPLIT - 1)).astype(F32)
    n = 2 * COEF_PIECES
    if q_side:
        cols = jnp.where(lane < COEF_PIECES, -hi, jnp.where(lane < n, -lo, cv))
    else:
        cols = jnp.where(lane < n, cv, jnp.where(lane < n + COEF_PIECES, hi, lo))
    return jnp.where(lane < 2 * n, cols, 0.0)


def _diff_attn_kernel(coef_ref, lam_ref, g_ref, q_ref, k_ref, v_ref, o_ref,
                      kaug_sc, qaug_sc, corr_sc, sa_sc, sb_sc, p_sc, m_sc, l_sc, alpha_sc, acc_sc,
                      *, tq, rb, seq, lambda_init):
    h = pl.program_id(1)
    qi = pl.program_id(2)
    dh = DIFF_HEAD_DIM
    n_chunks = seq // tq
    coefs = [coef_ref[h, t] for t in range(COEF_PIECES)]
    row_iota = lax.broadcasted_iota(jnp.int32, (tq, dh), 0)

    @pl.when(qi == 0)
    def _():
        def build(t, carry):
            r0 = pl.multiple_of(t * tq, tq)
            ak = _alibi_columns(row_iota + r0, coefs, q_side=False).astype(BF16)
            for c in range(2):
                kaug_sc[c, pl.ds(r0, tq), 0:dh] = k_ref[pl.ds(r0, tq), c * dh:(c + 1) * dh]
                kaug_sc[c, pl.ds(r0, tq), dh:2 * dh] = ak
            return carry

        lax.fori_loop(0, n_chunks, build, 0)
        row = lax.broadcasted_iota(jnp.int32, (tq, tq), 0)
        col = lax.broadcasted_iota(jnp.int32, (tq, tq), 1)
        slope2 = coefs[0] + coefs[1] + coefs[2]
        corr_sc[0] = jnp.zeros((tq, tq), F32)
        corr_sc[1] = (2.0 * slope2) * jnp.minimum(row - col, 0).astype(F32)

    aq = _alibi_columns(row_iota + qi * tq, coefs, q_side=True)
    for c in range(2):
        q_c = q_ref[:, c * dh:(c + 1) * dh]
        qaug_sc[0, c, :, 0:dh] = q_c
        qaug_sc[0, c, :, dh:2 * dh] = aq.astype(BF16)
        qaug_sc[1, c, :, 0:dh] = q_c
        qaug_sc[1, c, :, dh:2 * dh] = (-aq).astype(BF16)

    m_sc[...] = jnp.full(m_sc.shape, -jnp.inf, F32)
    l_sc[...] = jnp.zeros(l_sc.shape, F32)
    acc_sc[...] = jnp.zeros(acc_sc.shape, F32)

    n_slabs = tq // V7X_LANES
    last = n_chunks - 1

    def scores(kc, s_ref):
        ks = pl.multiple_of(kc * tq, tq)
        side = jnp.where(kc > qi, 1, 0)
        for c in range(2):
            s_ref[c] = lax.dot_general(
                qaug_sc[side, c], kaug_sc[c, pl.ds(ks, tq), :],
                (((1,), (1,)), ((), ())), preferred_element_type=F32)

    def absorb(kc, s_ref, near_diag):
        ks = pl.multiple_of(kc * tq, tq)
        on_diag = jnp.where(kc == qi, 1, 0)

        def block_slabs(c, rows):
            sb = s_ref[c, rows, :]
            if near_diag:
                sb = sb + corr_sc[on_diag, rows, :]
            return [sb[:, t * V7X_LANES:(t + 1) * V7X_LANES] for t in range(n_slabs)]

        for c in range(2):
            for b in range(tq // rb):
                rows = slice(b * rb, (b + 1) * rb)
                slabs = block_slabs(c, rows)
                mx = functools.reduce(jnp.maximum, slabs)
                m_old = m_sc[c, rows, :]
                m_new = jnp.maximum(m_old, jnp.broadcast_to(
                    jnp.max(mx, axis=-1, keepdims=True), (rb, V7X_LANES)))
                a = jnp.exp2(m_old - m_new)
                ps = [jnp.exp2(sl - m_new) for sl in slabs]
                l_sc[c, rows, :] = a * l_sc[c, rows, :] + functools.reduce(jnp.add, ps)
                p_sc[c, rows, :] = jnp.concatenate(ps, axis=-1).astype(BF16)
                alpha_sc[c, rows, :] = a
                m_sc[c, rows, :] = m_new
            alpha = alpha_sc[c]
            acc_sc[c] = jnp.concatenate([alpha, alpha], axis=-1) * acc_sc[c] + jnp.dot(
                p_sc[c], v_ref[pl.ds(ks, tq), :], preferred_element_type=F32)

    def pair_body(kc, near_diag):
        scores(kc + 1, sb_sc)
        absorb(kc, sa_sc, near_diag)
        scores(jnp.minimum(kc + 2, last), sa_sc)
        absorb(kc + 1, sb_sc, near_diag)

    def pair(t, carry):
        kc = 2 * t
        lax.cond(t == qi // 2,
                 functools.partial(pair_body, kc, True),
                 functools.partial(pair_body, kc, False))
        return carry

    scores(0, sa_sc)
    lax.fori_loop(0, n_chunks // 2, pair, 0)

    lam_rows = lam_ref[...]
    lam = (jnp.exp(jnp.sum(lam_rows[0:1] * lam_rows[1:2], axis=-1, keepdims=True))
           - jnp.exp(jnp.sum(lam_rows[2:3] * lam_rows[3:4], axis=-1, keepdims=True))
           + lambda_init)
    l0 = jnp.sum(l_sc[0], axis=-1, keepdims=True)
    l1 = jnp.sum(l_sc[1], axis=-1, keepdims=True)
    o = acc_sc[0] / l0 - lam * (acc_sc[1] / l1)
    ms = jnp.mean(o * o, axis=-1, keepdims=True)
    o = o * lax.rsqrt(ms + SUBLN_EPS) * g_ref[...] * (1.0 - lambda_init)
    o_ref[...] = o.astype(o_ref.dtype)


def _diff_attention(z, coefs, lam_rows, subln_g, *, batch, seq, lambda_init, tq=512, rb=32):
    T = z.shape[0]
    e = 2 * DIFF_HEAD_DIM
    nq = seq // tq
    qspec = pl.BlockSpec((tq, e), lambda b, h, i: (b * nq + i, Z_DQ // e + h))
    kspec = pl.BlockSpec((seq, e), lambda b, h, i: (b, Z_DK // e + h))
    vspec = pl.BlockSpec((seq, e), lambda b, h, i: (b, Z_DV // e + h))
    return pl.pallas_call(
        functools.partial(_diff_attn_kernel, tq=tq, rb=rb, seq=seq, lambda_init=lambda_init),
        grid=(batch, DIFF_HEADS, nq),
        in_specs=[
            pl.BlockSpec(memory_space=pltpu.SMEM),
            pl.BlockSpec((4, DIFF_HEAD_DIM), lambda b, h, i: (0, 0)),
            pl.BlockSpec((1, e), lambda b, h, i: (0, 0)),
            qspec, kspec, vspec,
        ],
        out_specs=pl.BlockSpec((tq, e), lambda b, h, i: (b * nq + i, h)),
        out_shape=jax.ShapeDtypeStruct((T, DIFF_V_W), BF16),
        scratch_shapes=[
            pltpu.VMEM((2, seq, e), BF16),
            pltpu.VMEM((2, 2, tq, e), BF16),
            pltpu.VMEM((2, tq, tq), F32),
            pltpu.VMEM((2, tq, tq), F32),
            pltpu.VMEM((2, tq, tq), F32),
            pltpu.VMEM((2, tq, tq), BF16),
            pltpu.VMEM((2, tq, V7X_LANES), F32),
            pltpu.VMEM((2, tq, V7X_LANES), F32),
            pltpu.VMEM((2, tq, V7X_LANES), F32),
            pltpu.VMEM((2, tq, e), F32),
        ],
        compiler_params=_params(("parallel", "parallel", "arbitrary")),
        name="diff_attn",
    )(coefs, lam_rows, subln_g.reshape(1, e), z, z, z)


def _dil_attn_kernel(slopes_ref, q_ref, k_ref, v_ref, o_ref, lse_ref, *, seq, dil, tq, unroll):
    h = pl.program_id(1)
    slope = slopes_ref[h]
    w = DIL_HALF_KEYS
    nk = tq + 2 * w
    length = seq // dil
    blocks_per_class = length // tq
    scale = DIL_HEAD_DIM ** -0.5
    row = lax.broadcasted_iota(jnp.int32, (tq, nk), 0)
    col = lax.broadcasted_iota(jnp.int32, (tq, nk), 1)
    col_minus_row = col - row

    def body(i, carry):
        cls = i // blocks_per_class
        cls_base = cls * length
        qs_local = (i % blocks_per_class) * tq
        ks_local = jnp.clip(qs_local - w, 0, length - nk)
        qs = pl.multiple_of(cls_base + qs_local, tq)
        ks = pl.multiple_of(cls_base + ks_local, w)
        rel = jnp.abs(col_minus_row + (ks_local - qs_local))
        q = q_ref[pl.ds(qs, tq), :]
        k = k_ref[pl.ds(ks, nk), :]
        v = v_ref[pl.ds(ks, nk), :]
        s = lax.dot_general(q, k, (((1,), (1,)), ((), ())), preferred_element_type=F32) * scale
        s = jnp.where(rel <= w, s - slope * rel.astype(F32), NEG_INF)
        m = jnp.max(s, axis=-1, keepdims=True)
        e = jnp.exp(s - m)
        z = jnp.sum(e, axis=-1, keepdims=True)
        o = jnp.dot(e.astype(BF16), v, preferred_element_type=F32) / z
        rows = pl.ds(qs, tq) if dil == 1 else pl.ds(qs_local * dil + cls, tq, stride=dil)
        o_ref[rows, :] = o
        lse_ref[rows, :] = jnp.broadcast_to(m + jnp.log(z), (tq, DIL_HEAD_DIM))
        return carry

    lax.fori_loop(0, seq // tq, body, 0, unroll=unroll)


def _dilated_attention(qkv, offsets, slopes_g, *, dil, batch, seq, tq=128, unroll=8):
    T = qkv.shape[0]
    hd = DIL_HEAD_DIM

    def in_spec(off):
        return pl.BlockSpec((seq, hd), lambda b, h: (b, off // hd + h))

    out_spec = pl.BlockSpec((seq, hd), lambda b, h: (b, h))
    out_sds = jax.ShapeDtypeStruct((T, DIL_OUT_W), F32)
    return pl.pallas_call(
        functools.partial(_dil_attn_kernel, seq=seq, dil=dil, tq=tq, unroll=unroll),
        grid=(batch, DIL_GROUP_HEADS),
        in_specs=[pl.BlockSpec(memory_space=pltpu.SMEM)] + [in_spec(o) for o in offsets],
        out_specs=[out_spec, out_spec],
        out_shape=[out_sds, out_sds],
        compiler_params=_params(("parallel", "parallel")),
        name=f"dil_attn_{dil}",
    )(slopes_g, qkv, qkv, qkv)


def _merge_kernel(oa_ref, o0_ref, o1_ref, o2_ref, l0_ref, l1_ref, l2_ref, ga_ref, gb_ref,
                  wa_ref, wb_ref, y_ref, ob_sc):
    @pl.when(pl.program_id(1) == 0)
    def _():
        l0, l1, l2 = l0_ref[...], l1_ref[...], l2_ref[...]
        m = jnp.maximum(jnp.maximum(l0, l1), l2)
        w0, w1, w2 = jnp.exp(l0 - m), jnp.exp(l1 - m), jnp.exp(l2 - m)
        ob = (w0 * o0_ref[...] + w1 * o1_ref[...] + w2 * o2_ref[...]) / (w0 + w1 + w2)
        ob_sc[...] = ob.astype(BF16)

    ya = jnp.dot(oa_ref[...], wa_ref[...], preferred_element_type=F32)
    yb = jnp.dot(ob_sc[...], wb_ref[...], preferred_element_type=F32)
    y = (jax.nn.sigmoid(ga_ref[...].astype(F32)) * ya
         + jax.nn.sigmoid(gb_ref[...].astype(F32)) * yb)
    y_ref[...] = y.astype(y_ref.dtype)


def _merge(oa, dil_outs, dil_lses, z, wa_bf, wb_bf, *, tm=512, tn=512):
    T = oa.shape[0]
    row_a = pl.BlockSpec((tm, DIFF_V_W), lambda i, j: (i, 0))
    row_b = pl.BlockSpec((tm, DIL_OUT_W), lambda i, j: (i, 0))
    return pl.pallas_call(
        _merge_kernel,
        grid=(T // tm, D_MODEL // tn),
        in_specs=[row_a, row_b, row_b, row_b, row_b, row_b, row_b,
                  pl.BlockSpec((tm, tn), lambda i, j: (i, Z_GA // tn + j)),
                  pl.BlockSpec((tm, tn), lambda i, j: (i, Z_GB // tn + j)),
                  pl.BlockSpec((DIFF_V_W, tn), lambda i, j: (0, j)),
                  pl.BlockSpec((DIL_OUT_W, tn), lambda i, j: (0, j))],
        out_specs=pl.BlockSpec((tm, tn), lambda i, j: (i, j)),
        out_shape=jax.ShapeDtypeStruct((T, D_MODEL), BF16),
        scratch_shapes=[pltpu.VMEM((tm, DIL_OUT_W), BF16)],
        compiler_params=_params(("parallel", "arbitrary")),
        name="merge",
    )(oa, *dil_outs, *dil_lses, z, z, wa_bf, wb_bf)


def _ple_kernel(x_ref, p_ref, wg_ref, wp_ref, o_ref):
    gate = jnp.dot(x_ref[...], wg_ref[...].astype(BF16), preferred_element_type=F32)
    proj = jnp.dot(p_ref[...].astype(BF16), wp_ref[...].astype(BF16),
                   preferred_element_type=F32)
    o_ref[...] = jax.nn.sigmoid(gate) * proj


def _ple(x_bf, p, wg, wp, *, tm=1024, tn=512):
    T, D = x_bf.shape
    return pl.pallas_call(
        _ple_kernel,
        grid=(T // tm, D // tn),
        in_specs=[
            pl.BlockSpec((tm, D), lambda i, j: (i, 0)),
            pl.BlockSpec((tm, PLE_DIM), lambda i, j: (i, 0)),
            pl.BlockSpec((D, tn), lambda i, j: (0, j)),
            pl.BlockSpec((PLE_DIM, tn), lambda i, j: (0, j)),
        ],
        out_specs=pl.BlockSpec((tm, tn), lambda i, j: (i, j)),
        out_shape=jax.ShapeDtypeStruct((T, D), F32),
        compiler_params=_params(("parallel", "parallel")),
        name="ple",
    )(x_bf, p, wg, wp)


def _alibi_slopes(n):
    return jnp.exp2(-8.0 * jnp.arange(1, n + 1, dtype=F32) / n)


def _bf16_pieces(v):
    pieces, rest = [], v
    for _ in range(COEF_PIECES):
        piece = rest.astype(BF16).astype(F32)
        pieces.append(piece)
        rest = rest - piece
    return jnp.stack(pieces, axis=-1)


def kernel(x, p, ffn1_w_in, ffn1_w_out, ln1_g, ln1_b, w_in, lam_q1, lam_k1, lam_q2, lam_k2,
           subln_g, w_branch_diff, w_branch_dil, w_mix_out, ln2_g, ln2_b, ffn2_w_in,
           ffn2_w_out, ln3_g, ln3_b, w_ple_gate, w_ple_proj, ln4_g, ln4_b):
    batch, seq, d = x.shape
    depth = ffn1_w_in.shape[0]
    T = batch * seq
    alpha = (2 * depth) ** 0.25
    n_pat = len(DIL_PATTERNS)
    tn = 512

    h = x.reshape(T, d)
    h_bf = h.astype(BF16)
    diff_coefs = _bf16_pieces(_alibi_slopes(DIFF_HEADS) * LOG2E)
    dil_slopes = _alibi_slopes(n_pat * DIL_GROUP_HEADS).reshape(DIL_GROUP_HEADS, n_pat)
    colscale = jnp.ones((Z_W,), F32).at[Z_DQ:Z_DK].set(DIFF_HEAD_DIM ** -0.5 * LOG2E)

    for i in range(depth):
        lambda_init = 0.8 - 0.6 * math.exp(-0.3 * i)

        f = _ffn(h_bf, ffn1_w_in[i].astype(BF16), ffn1_w_out[i].astype(BF16))
        dils = tuple(dil for _window, dil in DIL_PATTERNS if dil > 1)
        h, h_bf, h_cm = _res_ln_class_major(h, f, ln1_g[i], ln1_b[i], alpha=alpha, scale=0.5,
                                            batch=batch, seq=seq, dils=dils)
        h_class_major = dict(zip(dils, h_cm))

        z, ffn2_w_in_bf, ffn2_w_out_bf = _matmul(
            h_bf, w_in[i], BF16, n_out=Z_W, wcol=_main_wcol(tn), colscale=colscale, tn=tn,
            riders=((ffn2_w_in[i], (d, V7X_LANES), 1), (ffn2_w_out[i], (RIDER_ROWS, d), 0)),
            name="in_proj")
        lam_rows = jnp.stack([lam_q1[i], lam_k1[i], lam_q2[i], lam_k2[i]]).astype(F32)
        oa = _diff_attention(z, diff_coefs, lam_rows, subln_g[i].astype(F32),
                             batch=batch, seq=seq, lambda_init=lambda_init)

        dil_outs, dil_lses = [], []
        for g, (_window, dil) in enumerate(DIL_PATTERNS):
            slopes_g = dil_slopes[:, g] * dil
            if dil == 1:
                o_g, lse_g = _dilated_attention(z, (Z_LQ, Z_LK, Z_LV), slopes_g, dil=dil,
                                                batch=batch, seq=seq)
            else:
                qkv = _matmul(h_class_major[dil], w_in[i], BF16,
                              n_out=3 * DIL_OUT_W, wcol=_group_wcol(g, tn), tn=tn,
                              name=f"in_proj_dil{dil}")
                o_g, lse_g = _dilated_attention(qkv, (0, DIL_OUT_W, 2 * DIL_OUT_W), slopes_g,
                                                dil=dil, batch=batch, seq=seq)
            dil_outs.append(o_g)
            dil_lses.append(lse_g)

        y = _merge(oa, dil_outs, dil_lses, z, w_branch_diff[i].astype(BF16),
                   w_branch_dil[i].astype(BF16))
        mix = _matmul(y, w_mix_out[i], F32, n_out=d, wcol=lambda j: j, tn=tn, name="mix_out")
        h, h_bf = _res_ln(h, mix, ln2_g[i], ln2_b[i], alpha=alpha, scale=1.0)

        f = _ffn(h_bf, ffn2_w_in_bf, ffn2_w_out_bf)
        h, h_bf = _res_ln(h, f, ln3_g[i], ln3_b[i], alpha=alpha, scale=0.5)

        ple = _ple(h_bf, p[i].reshape(T, PLE_DIM), w_ple_gate[i], w_ple_proj[i])
        if i + 1 < depth:
            h, h_bf = _res_ln(h, ple, ln4_g[i], ln4_b[i], alpha=alpha, scale=1.0)
        else:
            h = _res_ln(h, ple, ln4_g[i], ln4_b[i], alpha=alpha, scale=1.0, with_bf16=False)

    return h.reshape(batch, seq, d)
```

```python
import functools
import math

import jax
import jax.numpy as jnp
from jax import lax
from jax.experimental import pallas as pl
from jax.experimental.pallas import tpu as pltpu

F32 = jnp.float32
BF16 = jnp.bfloat16

D_MODEL = 4096
PLE_DIM = 256
D_FF = 11008
DIFF_HEADS = 8
DIFF_HEAD_DIM = 128
DIL_PATTERNS = ((128, 1), (512, 4), (2048, 16))
DIL_GROUP_HEADS = 8
DIL_HEAD_DIM = 128
DIL_HALF_KEYS = 64
LN_EPS = 1e-5
SUBLN_EPS = 1e-5
NEG_INF = -1e30

DIFF_QK_W = DIFF_HEADS * 2 * DIFF_HEAD_DIM
DIFF_V_W = DIFF_HEADS * 2 * DIFF_HEAD_DIM
DIL_W = len(DIL_PATTERNS) * DIL_GROUP_HEADS * DIL_HEAD_DIM
DIL_OUT_W = DIL_GROUP_HEADS * DIL_HEAD_DIM
IN_PROJ_W = 2 * DIFF_QK_W + DIFF_V_W + 3 * DIL_W + 2 * D_MODEL

OFF_DQ = 0
OFF_DK = OFF_DQ + DIFF_QK_W
OFF_DV = OFF_DK + DIFF_QK_W
OFF_LQ = OFF_DV + DIFF_V_W
OFF_LK = OFF_LQ + DIL_W
OFF_LV = OFF_LK + DIL_W
OFF_GA = OFF_LV + DIL_W
OFF_GB = OFF_GA + D_MODEL

Z_DQ = 0
Z_DK = Z_DQ + DIFF_QK_W
Z_DV = Z_DK + DIFF_QK_W
Z_LQ = Z_DV + DIFF_V_W
Z_LK = Z_LQ + DIL_OUT_W
Z_LV = Z_LK + DIL_OUT_W
Z_GA = Z_LV + DIL_OUT_W
Z_GB = Z_GA + D_MODEL
Z_W = Z_GB + D_MODEL

V7X_LANES = 128
V7X_VMEM_BYTES = 64 * 1024 * 1024
V7X_VMEM_REQUEST = 56 * 1024 * 1024

FFN_BLOCK = 256

LOG2E = math.log2(math.e)
COEF_PIECES = 3
POS_SPLIT = 64
RIDER_ROWS = 64


def _params(semantics, vmem_bytes=V7X_VMEM_REQUEST):
    return pltpu.CompilerParams(dimension_semantics=semantics, vmem_limit_bytes=vmem_bytes)


def _ffn_kernel(x_ref, wga_ref, wgb_ref, wua_ref, wub_ref, woa_ref, wob_ref, o_ref, *, n_blocks):
    j = pl.program_id(1)

    @pl.when(j == 0)
    def _():
        o_ref[...] = jnp.zeros(o_ref.shape, F32)

    x = x_ref[...]

    def hidden(wg_ref, wu_ref):
        g = jnp.dot(x, wg_ref[...], preferred_element_type=F32)
        u = jnp.dot(x, wu_ref[...], preferred_element_type=F32)
        return g * jax.nn.sigmoid(g) * u

    ha = hidden(wga_ref, wua_ref).astype(BF16)
    hb = jnp.where(2 * j + 1 < n_blocks, hidden(wgb_ref, wub_ref), 0.0).astype(BF16)
    part = jnp.dot(ha, woa_ref[...], preferred_element_type=F32)
    part += jnp.dot(hb, wob_ref[...], preferred_element_type=F32)
    o_ref[...] += part


def _ffn(x_bf, w_in_bf, w_out_bf, *, tm=512, tb=FFN_BLOCK):
    T, D = x_bf.shape
    n_blocks = w_out_bf.shape[0] // tb
    steps = -(-n_blocks // 2)

    def first(j):
        return 2 * j

    def second(j):
        return jnp.minimum(2 * j + 1, n_blocks - 1)

    def col_spec(blk, off):
        return pl.BlockSpec((D, tb), lambda i, j: (0, off + blk(j)))

    def row_spec(blk):
        return pl.BlockSpec((tb, D), lambda i, j: (blk(j), 0))

    return pl.pallas_call(
        functools.partial(_ffn_kernel, n_blocks=n_blocks),
        grid=(T // tm, steps),
        in_specs=[
            pl.BlockSpec((tm, D), lambda i, j: (i, 0)),
            col_spec(first, 0), col_spec(second, 0),
            col_spec(first, n_blocks), col_spec(second, n_blocks),
            row_spec(first), row_spec(second),
        ],
        out_specs=pl.BlockSpec((tm, D), lambda i, j: (i, 0)),
        out_shape=jax.ShapeDtypeStruct((T, D), F32),
        compiler_params=_params(("parallel", "arbitrary")),
        name="ffn",
    )(x_bf, w_in_bf, w_in_bf, w_in_bf, w_in_bf, w_out_bf, w_out_bf)


def _layer_norm_rows(y, g_ref, b_ref):
    mu = jnp.mean(y, axis=-1, keepdims=True)
    yc = y - mu
    var = jnp.mean(yc * yc, axis=-1, keepdims=True)
    return yc * lax.rsqrt(var + LN_EPS) * g_ref[...] + b_ref[...]


def _ln_kernel(h_ref, br_ref, g_ref, b_ref, o_ref, obf_ref, *, alpha, scale):
    out = _layer_norm_rows(alpha * h_ref[...] + scale * br_ref[...], g_ref, b_ref)
    o_ref[...] = out
    obf_ref[...] = out.astype(BF16)


def _ln_class_major_kernel(h_ref, br_ref, g_ref, b_ref, o_ref, obf_ref, *rest, alpha, scale, dils):
    cm_refs, slab_sc = rest[:-1], rest[-1]
    out = _layer_norm_rows(alpha * h_ref[...] + scale * br_ref[...], g_ref, b_ref)
    o_ref[...] = out
    obf_ref[...] = out.astype(BF16)
    tm, d = out.shape
    for s in range(d // V7X_LANES):
        slab_sc[s] = out[:, s * V7X_LANES:(s + 1) * V7X_LANES]
    for dil, cm_ref in zip(dils, cm_refs):
        for r in range(dil):
            for s in range(d // V7X_LANES):
                cm_ref[r, :, s * V7X_LANES:(s + 1) * V7X_LANES] = slab_sc[
                    s, pl.ds(r, tm // dil, stride=dil), :].astype(BF16)


def _res_ln_class_major(h, branch, g, b, *, alpha, scale, batch, seq, dils, tm=256):
    T, D = h.shape
    tiles = seq // tm
    row = pl.BlockSpec((tm, D), lambda i: (i, 0))
    vec = pl.BlockSpec((1, D), lambda i: (0, 0))
    cm_specs = [pl.BlockSpec((None, dil, tm // dil, D), lambda i: (i // tiles, 0, i % tiles, 0))
                for dil in dils]
    cm_shapes = [jax.ShapeDtypeStruct((batch, dil, seq // dil, D), BF16) for dil in dils]
    outs = pl.pallas_call(
        functools.partial(_ln_class_major_kernel, alpha=alpha, scale=scale, dils=dils),
        grid=(T // tm,),
        in_specs=[row, row, vec, vec],
        out_specs=[row, row] + cm_specs,
        out_shape=[jax.ShapeDtypeStruct((T, D), F32), jax.ShapeDtypeStruct((T, D), BF16)]
        + cm_shapes,
        scratch_shapes=[pltpu.VMEM((D // V7X_LANES, tm, V7X_LANES), F32)],
        compiler_params=_params(("parallel",)),
        name="res_ln_class_major",
    )(h, branch, g.reshape(1, D), b.reshape(1, D))
    return outs[0], outs[1], [cm.reshape(T, D) for cm in outs[2:]]


def _ln_f32_kernel(h_ref, br_ref, g_ref, b_ref, o_ref, *, alpha, scale):
    o_ref[...] = _layer_norm_rows(alpha * h_ref[...] + scale * br_ref[...], g_ref, b_ref)


def _res_ln(h, branch, g, b, *, alpha, scale, with_bf16=True, tm=256):
    T, D = h.shape
    row = pl.BlockSpec((tm, D), lambda i: (i, 0))
    vec = pl.BlockSpec((1, D), lambda i: (0, 0))
    f32_out = jax.ShapeDtypeStruct((T, D), F32)
    return pl.pallas_call(
        functools.partial(_ln_kernel if with_bf16 else _ln_f32_kernel, alpha=alpha, scale=scale),
        grid=(T // tm,),
        in_specs=[row, row, vec, vec],
        out_specs=[row, row] if with_bf16 else row,
        out_shape=[f32_out, jax.ShapeDtypeStruct((T, D), BF16)] if with_bf16 else f32_out,
        compiler_params=_params(("parallel",)),
        name="res_ln",
    )(h, branch, g.reshape(1, D), b.reshape(1, D))


def _matmul_kernel(*refs, scaled, rider_blocks, n_col_tiles):
    n_riders = len(rider_blocks)
    x_ref, w_ref = refs[:2]
    s_ref = refs[2] if scaled else None
    rider_in = refs[2 + scaled:2 + scaled + n_riders]
    o_ref = refs[2 + scaled + n_riders]
    rider_out = refs[3 + scaled + n_riders:]
    acc = jnp.dot(x_ref[...], w_ref[...].astype(BF16), preferred_element_type=F32)
    if scaled:
        acc = acc * s_ref[...]
    o_ref[...] = acc.astype(o_ref.dtype)
    step = pl.program_id(0) * n_col_tiles + pl.program_id(1)
    for src, dst, n_blocks in zip(rider_in, rider_out, rider_blocks):
        @pl.when(step < n_blocks)
        def _(src=src, dst=dst):
            dst[...] = src[...].astype(BF16)


def _matmul(x_bf, w, out_dtype, *, n_out, wcol, colscale=None, riders=(), tm=1024, tn=512,
            name="matmul"):
    T, K = x_bf.shape
    n_col_tiles = n_out // tn
    in_specs = [
        pl.BlockSpec((tm, K), lambda i, j: (i, 0)),
        pl.BlockSpec((K, tn), lambda i, j: (0, wcol(j))),
    ]
    args = [x_bf, w]
    if colscale is not None:
        in_specs.append(pl.BlockSpec((1, tn), lambda i, j: (0, j)))
        args.append(colscale.reshape(1, n_out))
    rider_specs, rider_blocks = [], []
    for arr, block, axis in riders:
        n_blocks = arr.shape[axis] // block[axis]
        assert n_blocks <= (T // tm) * n_col_tiles

        def index_map(i, j, axis=axis, n_blocks=n_blocks):
            blk = jnp.minimum(i * n_col_tiles + j, n_blocks - 1)
            return tuple(blk if a == axis else 0 for a in range(2))

        rider_specs.append(pl.BlockSpec(block, index_map))
        rider_blocks.append(n_blocks)
        args.append(arr)
    outs = pl.pallas_call(
        functools.partial(_matmul_kernel, scaled=colscale is not None,
                          rider_blocks=tuple(rider_blocks), n_col_tiles=n_col_tiles),
        grid=(T // tm, n_col_tiles),
        in_specs=in_specs + rider_specs,
        out_specs=[pl.BlockSpec((tm, tn), lambda i, j: (i, j))] + rider_specs,
        out_shape=[jax.ShapeDtypeStruct((T, n_out), out_dtype)]
        + [jax.ShapeDtypeStruct(arr.shape, BF16) for arr, _, _ in riders],
        compiler_params=_params(("arbitrary", "arbitrary")),
        name=name,
    )(*args)
    return outs if riders else outs[0]


def _main_wcol(tn):
    per_sec = DIL_OUT_W // tn
    first = OFF_LQ // tn
    n_pat = len(DIL_PATTERNS)

    def wcol(j):
        k = j - first
        dil_tile = first + (k // per_sec) * (n_pat * per_sec) + k % per_sec
        gate_tile = j + (n_pat - 1) * 3 * per_sec
        return jnp.where(j < first, j, jnp.where(k < 3 * per_sec, dil_tile, gate_tile))

    return wcol


def _group_wcol(group, tn):
    per_sec = DIL_OUT_W // tn

    def wcol(j):
        return (OFF_LQ + group * DIL_OUT_W) // tn + (j // per_sec) * (DIL_W // tn) + j % per_sec

    return wcol


def _alibi_columns(pos, coefs, q_side):
    lane = lax.broadcasted_iota(jnp.int32, pos.shape, 1)
    piece = jnp.where(lane < 3, lane, jnp.where(lane < 6, lane - 3,
                                                jnp.where(lane < 9, lane - 6, lane - 9)))
    cv = jnp.where(piece == 0, coefs[0], jnp.where(piece == 1, coefs[1], coefs[2]))
    hi = (pos - (pos & (POS_SPLIT - 1))).astype(F32)
    lo = (pos & (POS_SPLIT - 1)).astype(F32)
    n = 2 * COEF_PIECES
    if q_side:
        cols = jnp.where(lane < COEF_PIECES, -hi, jnp.where(lane < n, -lo, cv))
    else:
        cols = jnp.where(lane < n, cv, jnp.where(lane < n + COEF_PIECES, hi, lo))
    return jnp.where(lane < 2 * n, cols, 0.0)


def _diff_attn_kernel(coef_ref, lam_ref, g_ref, q_ref, k_ref, v_ref, o_ref,
                      kaug_sc, qaug_sc, corr_sc, sa_sc, sb_sc, p_sc, m_sc, l_sc, alpha_sc, acc_sc,
                      *, tq, rb, seq, lambda_init):
    h = pl.program_id(1)
    qi = pl.program_id(2)
    dh = DIFF_HEAD_DIM
    n_chunks = seq // tq
    coefs = [coef_ref[h, t] for t in range(COEF_PIECES)]
    row_iota = lax.broadcasted_iota(jnp.int32, (tq, dh), 0)

    @pl.when(qi == 0)
    def _():
        def build(t, carry):
            r0 = pl.multiple_of(t * tq, tq)
            ak = _alibi_columns(row_iota + r0, coefs, q_side=False).astype(BF16)
            for c in range(2):
                kaug_sc[c, pl.ds(r0, tq), 0:dh] = k_ref[pl.ds(r0, tq), c * dh:(c + 1) * dh]
                kaug_sc[c, pl.ds(r0, tq), dh:2 * dh] = ak
            return carry

        lax.fori_loop(0, n_chunks, build, 0)
        row = lax.broadcasted_iota(jnp.int32, (tq, tq), 0)
        col = lax.broadcasted_iota(jnp.int32, (tq, tq), 1)
        slope2 = coefs[0] + coefs[1] + coefs[2]
        corr_sc[0] = jnp.zeros((tq, tq), F32)
        corr_sc[1] = (2.0 * slope2) * jnp.minimum(row - col, 0).astype(F32)

    aq = _alibi_columns(row_iota + qi * tq, coefs, q_side=True)
    for c in range(2):
        q_c = q_ref[:, c * dh:(c + 1) * dh]
        qaug_sc[0, c, :, 0:dh] = q_c
        qaug_sc[0, c, :, dh:2 * dh] = aq.astype(BF16)
        qaug_sc[1, c, :, 0:dh] = q_c
        qaug_sc[1, c, :, dh:2 * dh] = (-aq).astype(BF16)

    m_sc[...] = jnp.full(m_sc.shape, -jnp.inf, F32)
    l_sc[...] = jnp.zeros(l_sc.shape, F32)
    acc_sc[...] = jnp.zeros(acc_sc.shape, F32)

    n_slabs = tq // V7X_LANES
    last = n_chunks - 1

    def scores(kc, s_ref):
        ks = pl.multiple_of(kc * tq, tq)
        side = jnp.where(kc > qi, 1, 0)
        for c in range(2):
            s_ref[c] = lax.dot_general(
                qaug_sc[side, c], kaug_sc[c, pl.ds(ks, tq), :],
                (((1,), (1,)), ((), ())), preferred_element_type=F32)

    def absorb(kc, s_ref, near_diag):
        ks = pl.multiple_of(kc * tq, tq)
        on_diag = jnp.where(kc == qi, 1, 0)

        def block_slabs(c, rows):
            sb = s_ref[c, rows, :]
            if near_diag:
                sb = sb + corr_sc[on_diag, rows, :]
            return [sb[:, t * V7X_LANES:(t + 1) * V7X_LANES] for t in range(n_slabs)]

        for c in range(2):
            for b in range(tq // rb):
                rows = slice(b * rb, (b + 1) * rb)
                slabs = block_slabs(c, rows)
                mx = functools.reduce(jnp.maximum, slabs)
                m_old = m_sc[c, rows, :]
                m_new = jnp.maximum(m_old, jnp.broadcast_to(
                    jnp.max(mx, axis=-1, keepdims=True), (rb, V7X_LANES)))
                a = jnp.exp2(m_old - m_new)
                ps = [jnp.exp2(sl - m_new) for sl in slabs]
                l_sc[c, rows, :] = a * l_sc[c, rows, :] + functools.reduce(jnp.add, ps)
                p_sc[c, rows, :] = jnp.concatenate(ps, axis=-1).astype(BF16)
                alpha_sc[c, rows, :] = a
                m_sc[c, rows, :] = m_new
            alpha = alpha_sc[c]
            acc_sc[c] = jnp.concatenate([alpha, alpha], axis=-1) * acc_sc[c] + jnp.dot(
                p_sc[c], v_ref[pl.ds(ks, tq), :], preferred_element_type=F32)

    def pair_body(kc, near_diag):
        scores(kc + 1, sb_sc)
        absorb(kc, sa_sc, near_diag)
        scores(jnp.minimum(kc + 2, last), sa_sc)
        absorb(kc + 1, sb_sc, near_diag)

    def pair(t, carry):
        kc = 2 * t
        lax.cond(t == qi // 2,
                 functools.partial(pair_body, kc, True),
                 functools.partial(pair_body, kc, False))
        return carry

    scores(0, sa_sc)
    lax.fori_loop(0, n_chunks // 2, pair, 0)

    lam_rows = lam_ref[...]
    lam = (jnp.exp(jnp.sum(lam_rows[0:1] * lam_rows[1:2], axis=-1, keepdims=True))
           - jnp.exp(jnp.sum(lam_rows[2:3] * lam_rows[3:4], axis=-1, keepdims=True))
           + lambda_init)
    l0 = jnp.sum(l_sc[0], axis=-1, keepdims=True)
    l1 = jnp.sum(l_sc[1], axis=-1, keepdims=True)
    o = acc_sc[0] / l0 - lam * (acc_sc[1] / l1)
    ms = jnp.mean(o * o, axis=-1, keepdims=True)
    o = o * lax.rsqrt(ms + SUBLN_EPS) * g_ref[...] * (1.0 - lambda_init)
    o_ref[...] = o.astype(o_ref.dtype)


def _diff_attention(z, coefs, lam_rows, subln_g, *, batch, seq, lambda_init, tq=512, rb=32):
    T = z.shape[0]
    e = 2 * DIFF_HEAD_DIM
    nq = seq // tq
    qspec = pl.BlockSpec((tq, e), lambda b, h, i: (b * nq + i, Z_DQ // e + h))
    kspec = pl.BlockSpec((seq, e), lambda b, h, i: (b, Z_DK // e + h))
    vspec = pl.BlockSpec((seq, e), lambda b, h, i: (b, Z_DV // e + h))
    return pl.pallas_call(
        functools.partial(_diff_attn_kernel, tq=tq, rb=rb, seq=seq, lambda_init=lambda_init),
        grid=(batch, DIFF_HEADS, nq),
        in_specs=[
            pl.BlockSpec(memory_space=pltpu.SMEM),
            pl.BlockSpec((4, DIFF_HEAD_DIM), lambda b, h, i: (0, 0)),
            pl.BlockSpec((1, e), lambda b, h, i: (0, 0)),
            qspec, kspec, vspec,
        ],
        out_specs=pl.BlockSpec((tq, e), lambda b, h, i: (b * nq + i, h)),
        out_shape=jax.ShapeDtypeStruct((T, DIFF_V_W), BF16),
        scratch_shapes=[
            pltpu.VMEM((2, seq, e), BF16),
            pltpu.VMEM((2, 2, tq, e), BF16),
            pltpu.VMEM((2, tq, tq), F32),
            pltpu.VMEM((2, tq, tq), F32),
            pltpu.VMEM((2, tq, tq), F32),
            pltpu.VMEM((2, tq, tq), BF16),
            pltpu.VMEM((2, tq, V7X_LANES), F32),
            pltpu.VMEM((2, tq, V7X_LANES), F32),
            pltpu.VMEM((2, tq, V7X_LANES), F32),
            pltpu.VMEM((2, tq, e), F32),
        ],
        compiler_params=_params(("parallel", "parallel", "arbitrary")),
        name="diff_attn",
    )(coefs, lam_rows, subln_g.reshape(1, e), z, z, z)


def _dil_attn_kernel(slopes_ref, q_ref, k_ref, v_ref, o_ref, lse_ref, *, seq, dil, tq, unroll):
    h = pl.program_id(1)
    slope = slopes_ref[h]
    w = DIL_HALF_KEYS
    nk = tq + 2 * w
    length = seq // dil
    blocks_per_class = length // tq
    scale = DIL_HEAD_DIM ** -0.5
    row = lax.broadcasted_iota(jnp.int32, (tq, nk), 0)
    col = lax.broadcasted_iota(jnp.int32, (tq, nk), 1)
    col_minus_row = col - row

    def body(i, carry):
        cls = i // blocks_per_class
        cls_base = cls * length
        qs_local = (i % blocks_per_class) * tq
        ks_local = jnp.clip(qs_local - w, 0, length - nk)
        qs = pl.multiple_of(cls_base + qs_local, tq)
        ks = pl.multiple_of(cls_base + ks_local, w)
        rel = jnp.abs(col_minus_row + (ks_local - qs_local))
        q = q_ref[pl.ds(qs, tq), :]
        k = k_ref[pl.ds(ks, nk), :]
        v = v_ref[pl.ds(ks, nk), :]
        s = lax.dot_general(q, k, (((1,), (1,)), ((), ())), preferred_element_type=F32) * scale
        s = jnp.where(rel <= w, s - slope * rel.astype(F32), NEG_INF)
        m = jnp.max(s, axis=-1, keepdims=True)
        e = jnp.exp(s - m)
        z = jnp.sum(e, axis=-1, keepdims=True)
        o = jnp.dot(e.astype(BF16), v, preferred_element_type=F32) / z
        rows = pl.ds(qs, tq) if dil == 1 else pl.ds(qs_local * dil + cls, tq, stride=dil)
        o_ref[rows, :] = o
        lse_ref[rows, :] = jnp.broadcast_to(m + jnp.log(z), (tq, DIL_HEAD_DIM))
        return carry

    lax.fori_loop(0, seq // tq, body, 0, unroll=unroll)


def _dilated_attention(qkv, offsets, slopes_g, *, dil, batch, seq, tq=128, unroll=8):
    T = qkv.shape[0]
    hd = DIL_HEAD_DIM

    def in_spec(off):
        return pl.BlockSpec((seq, hd), lambda b, h: (b, off // hd + h))

    out_spec = pl.BlockSpec((seq, hd), lambda b, h: (b, h))
    out_sds = jax.ShapeDtypeStruct((T, DIL_OUT_W), F32)
    return pl.pallas_call(
        functools.partial(_dil_attn_kernel, seq=seq, dil=dil, tq=tq, unroll=unroll),
        grid=(batch, DIL_GROUP_HEADS),
        in_specs=[pl.BlockSpec(memory_space=pltpu.SMEM)] + [in_spec(o) for o in offsets],
        out_specs=[out_spec, out_spec],
        out_shape=[out_sds, out_sds],
        compiler_params=_params(("parallel", "parallel")),
        name=f"dil_attn_{dil}",
    )(slopes_g, qkv, qkv, qkv)


def _dil_mix_kernel(o0_ref, o1_ref, o2_ref, l0_ref, l1_ref, l2_ref, ob_ref):
    l0, l1, l2 = l0_ref[...], l1_ref[...], l2_ref[...]
    m = jnp.maximum(jnp.maximum(l0, l1), l2)
    w0, w1, w2 = jnp.exp(l0 - m), jnp.exp(l1 - m), jnp.exp(l2 - m)
    ob = (w0 * o0_ref[...] + w1 * o1_ref[...] + w2 * o2_ref[...]) / (w0 + w1 + w2)
    ob_ref[...] = ob.astype(ob_ref.dtype)


def _dil_mix(dil_outs, dil_lses, *, tm=512):
    T, W = dil_outs[0].shape
    row = pl.BlockSpec((tm, W), lambda i: (i, 0))
    return pl.pallas_call(
        _dil_mix_kernel,
        grid=(T // tm,),
        in_specs=[row] * 6,
        out_specs=row,
        out_shape=jax.ShapeDtypeStruct((T, W), BF16),
        compiler_params=_params(("parallel",)),
        name="dil_mix",
    )(*dil_outs, *dil_lses)


def _merge_kernel(oa_ref, ob_ref, ga_ref, gb_ref, wa_ref, wb_ref, y_ref):
    ya = jnp.dot(oa_ref[...], wa_ref[...].astype(BF16), preferred_element_type=F32)
    yb = jnp.dot(ob_ref[...], wb_ref[...].astype(BF16), preferred_element_type=F32)
    y = (jax.nn.sigmoid(ga_ref[...].astype(F32)) * ya
         + jax.nn.sigmoid(gb_ref[...].astype(F32)) * yb)
    y_ref[...] = y.astype(y_ref.dtype)


def _merge(oa, ob, z, wa, wb, *, tm=1024, tn=512):
    T = oa.shape[0]
    return pl.pallas_call(
        _merge_kernel,
        grid=(T // tm, D_MODEL // tn),
        in_specs=[pl.BlockSpec((tm, DIFF_V_W), lambda i, j: (i, 0)),
                  pl.BlockSpec((tm, DIL_OUT_W), lambda i, j: (i, 0)),
                  pl.BlockSpec((tm, tn), lambda i, j: (i, Z_GA // tn + j)),
                  pl.BlockSpec((tm, tn), lambda i, j: (i, Z_GB // tn + j)),
                  pl.BlockSpec((DIFF_V_W, tn), lambda i, j: (0, j)),
                  pl.BlockSpec((DIL_OUT_W, tn), lambda i, j: (0, j))],
        out_specs=pl.BlockSpec((tm, tn), lambda i, j: (i, j)),
        out_shape=jax.ShapeDtypeStruct((T, D_MODEL), BF16),
        compiler_params=_params(("parallel", "parallel")),
        name="merge",
    )(oa, ob, z, z, wa, wb)


def _ple_kernel(x_ref, p_ref, wg_ref, wp_ref, o_ref):
    gate = jnp.dot(x_ref[...], wg_ref[...].astype(BF16), preferred_element_type=F32)
    proj = jnp.dot(p_ref[...].astype(BF16), wp_ref[...].astype(BF16),
                   preferred_element_type=F32)
    o_ref[...] = jax.nn.sigmoid(gate) * proj


def _ple(x_bf, p, wg, wp, *, tm=1024, tn=512):
    T, D = x_bf.shape
    return pl.pallas_call(
        _ple_kernel,
        grid=(T // tm, D // tn),
        in_specs=[
            pl.BlockSpec((tm, D), lambda i, j: (i, 0)),
            pl.BlockSpec((tm, PLE_DIM), lambda i, j: (i, 0)),
            pl.BlockSpec((D, tn), lambda i, j: (0, j)),
            pl.BlockSpec((PLE_DIM, tn), lambda i, j: (0, j)),
        ],
        out_specs=pl.BlockSpec((tm, tn), lambda i, j: (i, j)),
        out_shape=jax.ShapeDtypeStruct((T, D), F32),
        compiler_params=_params(("parallel", "parallel")),
        name="ple",
    )(x_bf, p, wg, wp)


def _alibi_slopes(n):
    return jnp.exp2(-8.0 * jnp.arange(1, n + 1, dtype=F32) / n)


def _bf16_pieces(v):
    pieces, rest = [], v
    for _ in range(COEF_PIECES):
        piece = rest.astype(BF16).astype(F32)
        pieces.append(piece)
        rest = rest - piece
    return jnp.stack(pieces, axis=-1)


def kernel(x, p, ffn1_w_in, ffn1_w_out, ln1_g, ln1_b, w_in, lam_q1, lam_k1, lam_q2, lam_k2,
           subln_g, w_branch_diff, w_branch_dil, w_mix_out, ln2_g, ln2_b, ffn2_w_in,
           ffn2_w_out, ln3_g, ln3_b, w_ple_gate, w_ple_proj, ln4_g, ln4_b):
    batch, seq, d = x.shape
    depth = ffn1_w_in.shape[0]
    T = batch * seq
    alpha = (2 * depth) ** 0.25
    n_pat = len(DIL_PATTERNS)
    tn = 512

    h = x.reshape(T, d)
    h_bf = h.astype(BF16)
    diff_coefs = _bf16_pieces(_alibi_slopes(DIFF_HEADS) * LOG2E)
    dil_slopes = _alibi_slopes(n_pat * DIL_GROUP_HEADS).reshape(DIL_GROUP_HEADS, n_pat)
    colscale = jnp.ones((Z_W,), F32).at[Z_DQ:Z_DK].set(DIFF_HEAD_DIM ** -0.5 * LOG2E)

    for i in range(depth):
        lambda_init = 0.8 - 0.6 * math.exp(-0.3 * i)

        f = _ffn(h_bf, ffn1_w_in[i].astype(BF16), ffn1_w_out[i].astype(BF16))
        dils = tuple(dil for _window, dil in DIL_PATTERNS if dil > 1)
        h, h_bf, h_cm = _res_ln_class_major(h, f, ln1_g[i], ln1_b[i], alpha=alpha, scale=0.5,
                                            batch=batch, seq=seq, dils=dils)
        h_class_major = dict(zip(dils, h_cm))

        z, ffn2_w_in_bf, ffn2_w_out_bf = _matmul(
            h_bf, w_in[i], BF16, n_out=Z_W, wcol=_main_wcol(tn), colscale=colscale, tn=tn,
            riders=((ffn2_w_in[i], (d, V7X_LANES), 1), (ffn2_w_out[i], (RIDER_ROWS, d), 0)),
            name="in_proj")
        lam_rows = jnp.stack([lam_q1[i], lam_k1[i], lam_q2[i], lam_k2[i]]).astype(F32)
        oa = _diff_attention(z, diff_coefs, lam_rows, subln_g[i].astype(F32),
                             batch=batch, seq=seq, lambda_init=lambda_init)

        dil_outs, dil_lses = [], []
        for g, (_window, dil) in enumerate(DIL_PATTERNS):
            slopes_g = dil_slopes[:, g] * dil
            if dil == 1:
                o_g, lse_g = _dilated_attention(z, (Z_LQ, Z_LK, Z_LV), slopes_g, dil=dil,
                                                batch=batch, seq=seq)
            else:
                qkv = _matmul(h_class_major[dil], w_in[i], BF16,
                              n_out=3 * DIL_OUT_W, wcol=_group_wcol(g, tn), tn=tn,
                              name=f"in_proj_dil{dil}")
                o_g, lse_g = _dilated_attention(qkv, (0, DIL_OUT_W, 2 * DIL_OUT_W), slopes_g,
                                                dil=dil, batch=batch, seq=seq)
            dil_outs.append(o_g)
            dil_lses.append(lse_g)

        y = _merge(oa, _dil_mix(dil_outs, dil_lses), z, w_branch_diff[i], w_branch_dil[i])
        mix = _matmul(y, w_mix_out[i], F32, n_out=d, wcol=lambda j: j, tn=tn, name="mix_out")
        h, h_bf = _res_ln(h, mix, ln2_g[i], ln2_b[i], alpha=alpha, scale=1.0)

        f = _ffn(h_bf, ffn2_w_in_bf, ffn2_w_out_bf)
        h, h_bf = _res_ln(h, f, ln3_g[i], ln3_b[i], alpha=alpha, scale=0.5)

        ple = _ple(h_bf, p[i].reshape(T, PLE_DIM), w_ple_gate[i], w_ple_proj[i])
        if i + 1 < depth:
            h, h_bf = _res_ln(h, ple, ln4_g[i], ln4_b[i], alpha=alpha, scale=1.0)
        else:
            h = _res_ln(h, ple, ln4_g[i], ln4_b[i], alpha=alpha, scale=1.0, with_bf16=False)

    return h.reshape(batch, seq, d)
```

```python
import functools
import math

import jax
import jax.numpy as jnp
from jax import lax
from jax.experimental import pallas as pl
from jax.experimental.pallas import tpu as pltpu

F32 = jnp.float32
BF16 = jnp.bfloat16

D_MODEL = 4096
PLE_DIM = 256
D_FF = 11008
DIFF_HEADS = 8
DIFF_HEAD_DIM = 128
DIL_PATTERNS = ((128, 1), (512, 4), (2048, 16))
DIL_GROUP_HEADS = 8
DIL_HEAD_DIM = 128
DIL_HALF_KEYS = 64
LN_EPS = 1e-5
SUBLN_EPS = 1e-5
NEG_INF = -1e30

DIFF_QK_W = DIFF_HEADS * 2 * DIFF_HEAD_DIM
DIFF_V_W = DIFF_HEADS * 2 * DIFF_HEAD_DIM
DIL_W = len(DIL_PATTERNS) * DIL_GROUP_HEADS * DIL_HEAD_DIM
DIL_OUT_W = DIL_GROUP_HEADS * DIL_HEAD_DIM
IN_PROJ_W = 2 * DIFF_QK_W + DIFF_V_W + 3 * DIL_W + 2 * D_MODEL

OFF_DQ = 0
OFF_DK = OFF_DQ + DIFF_QK_W
OFF_DV = OFF_DK + DIFF_QK_W
OFF_LQ = OFF_DV + DIFF_V_W
OFF_LK = OFF_LQ + DIL_W
OFF_LV = OFF_LK + DIL_W
OFF_GA = OFF_LV + DIL_W
OFF_GB = OFF_GA + D_MODEL

Z_DQ = 0
Z_DK = Z_DQ + DIFF_QK_W
Z_DV = Z_DK + DIFF_QK_W
Z_LQ = Z_DV + DIFF_V_W
Z_LK = Z_LQ + DIL_OUT_W
Z_LV = Z_LK + DIL_OUT_W
Z_GA = Z_LV + DIL_OUT_W
Z_GB = Z_GA + D_MODEL
Z_W = Z_GB + D_MODEL

V7X_LANES = 128
V7X_VMEM_BYTES = 64 * 1024 * 1024
V7X_VMEM_REQUEST = 56 * 1024 * 1024

FFN_BLOCK = 256

LOG2E = math.log2(math.e)
COEF_PIECES = 3
POS_SPLIT = 64
RIDER_ROWS = 64


def _params(semantics, vmem_bytes=V7X_VMEM_REQUEST):
    return pltpu.CompilerParams(dimension_semantics=semantics, vmem_limit_bytes=vmem_bytes)


def _ffn_kernel(x_ref, wga_ref, wgb_ref, wua_ref, wub_ref, woa_ref, wob_ref, o_ref, *, n_blocks):
    j = pl.program_id(1)

    @pl.when(j == 0)
    def _():
        o_ref[...] = jnp.zeros(o_ref.shape, F32)

    x = x_ref[...]

    def hidden(wg_ref, wu_ref):
        g = jnp.dot(x, wg_ref[...], preferred_element_type=F32)
        u = jnp.dot(x, wu_ref[...], preferred_element_type=F32)
        return g * jax.nn.sigmoid(g) * u

    ha = hidden(wga_ref, wua_ref).astype(BF16)
    hb = jnp.where(2 * j + 1 < n_blocks, hidden(wgb_ref, wub_ref), 0.0).astype(BF16)
    part = jnp.dot(ha, woa_ref[...], preferred_element_type=F32)
    part += jnp.dot(hb, wob_ref[...], preferred_element_type=F32)
    o_ref[...] += part


def _ffn(x_bf, w_in_bf, w_out_bf, *, tm=512, tb=FFN_BLOCK):
    T, D = x_bf.shape
    n_blocks = w_out_bf.shape[0] // tb
    steps = -(-n_blocks // 2)

    def first(j):
        return 2 * j

    def second(j):
        return jnp.minimum(2 * j + 1, n_blocks - 1)

    def col_spec(blk, off):
        return pl.BlockSpec((D, tb), lambda i, j: (0, off + blk(j)))

    def row_spec(blk):
        return pl.BlockSpec((tb, D), lambda i, j: (blk(j), 0))

    return pl.pallas_call(
        functools.partial(_ffn_kernel, n_blocks=n_blocks),
        grid=(T // tm, steps),
        in_specs=[
            pl.BlockSpec((tm, D), lambda i, j: (i, 0)),
            col_spec(first, 0), col_spec(second, 0),
            col_spec(first, n_blocks), col_spec(second, n_blocks),
            row_spec(first), row_spec(second),
        ],
        out_specs=pl.BlockSpec((tm, D), lambda i, j: (i, 0)),
        out_shape=jax.ShapeDtypeStruct((T, D), F32),
        compiler_params=_params(("parallel", "arbitrary")),
        name="ffn",
    )(x_bf, w_in_bf, w_in_bf, w_in_bf, w_in_bf, w_out_bf, w_out_bf)


def _layer_norm_rows(y, g_ref, b_ref):
    mu = jnp.mean(y, axis=-1, keepdims=True)
    yc = y - mu
    var = jnp.mean(yc * yc, axis=-1, keepdims=True)
    return yc * lax.rsqrt(var + LN_EPS) * g_ref[...] + b_ref[...]


def _ln_kernel(h_ref, br_ref, g_ref, b_ref, o_ref, obf_ref, *, alpha, scale):
    out = _layer_norm_rows(alpha * h_ref[...] + scale * br_ref[...], g_ref, b_ref)
    o_ref[...] = out
    obf_ref[...] = out.astype(BF16)


def _ln_class_major_kernel(h_ref, br_ref, g_ref, b_ref, o_ref, obf_ref, *rest, alpha, scale, dils):
    cm_refs, slab_sc = rest[:-1], rest[-1]
    out = _layer_norm_rows(alpha * h_ref[...] + scale * br_ref[...], g_ref, b_ref)
    o_ref[...] = out
    obf_ref[...] = out.astype(BF16)
    tm, d = out.shape
    for s in range(d // V7X_LANES):
        slab_sc[s] = out[:, s * V7X_LANES:(s + 1) * V7X_LANES]
    for dil, cm_ref in zip(dils, cm_refs):
        for r in range(dil):
            for s in range(d // V7X_LANES):
                cm_ref[r, :, s * V7X_LANES:(s + 1) * V7X_LANES] = slab_sc[
                    s, pl.ds(r, tm // dil, stride=dil), :].astype(BF16)


def _res_ln_class_major(h, branch, g, b, *, alpha, scale, batch, seq, dils, tm=256):
    T, D = h.shape
    tiles = seq // tm
    row = pl.BlockSpec((tm, D), lambda i: (i, 0))
    vec = pl.BlockSpec((1, D), lambda i: (0, 0))
    cm_specs = [pl.BlockSpec((None, dil, tm // dil, D), lambda i: (i // tiles, 0, i % tiles, 0))
                for dil in dils]
    cm_shapes = [jax.ShapeDtypeStruct((batch, dil, seq // dil, D), BF16) for dil in dils]
    outs = pl.pallas_call(
        functools.partial(_ln_class_major_kernel, alpha=alpha, scale=scale, dils=dils),
        grid=(T // tm,),
        in_specs=[row, row, vec, vec],
        out_specs=[row, row] + cm_specs,
        out_shape=[jax.ShapeDtypeStruct((T, D), F32), jax.ShapeDtypeStruct((T, D), BF16)]
        + cm_shapes,
        scratch_shapes=[pltpu.VMEM((D // V7X_LANES, tm, V7X_LANES), F32)],
        compiler_params=_params(("parallel",)),
        name="res_ln_class_major",
    )(h, branch, g.reshape(1, D), b.reshape(1, D))
    return outs[0], outs[1], [cm.reshape(T, D) for cm in outs[2:]]


def _ln_f32_kernel(h_ref, br_ref, g_ref, b_ref, o_ref, *, alpha, scale):
    o_ref[...] = _layer_norm_rows(alpha * h_ref[...] + scale * br_ref[...], g_ref, b_ref)


def _res_ln(h, branch, g, b, *, alpha, scale, with_bf16=True, tm=256):
    T, D = h.shape
    row = pl.BlockSpec((tm, D), lambda i: (i, 0))
    vec = pl.BlockSpec((1, D), lambda i: (0, 0))
    f32_out = jax.ShapeDtypeStruct((T, D), F32)
    return pl.pallas_call(
        functools.partial(_ln_kernel if with_bf16 else _ln_f32_kernel, alpha=alpha, scale=scale),
        grid=(T // tm,),
        in_specs=[row, row, vec, vec],
        out_specs=[row, row] if with_bf16 else row,
        out_shape=[f32_out, jax.ShapeDtypeStruct((T, D), BF16)] if with_bf16 else f32_out,
        compiler_params=_params(("parallel",)),
        name="res_ln",
    )(h, branch, g.reshape(1, D), b.reshape(1, D))


def _matmul_kernel(*refs, scaled, rider_blocks, n_col_tiles):
    n_riders = len(rider_blocks)
    x_ref, w_ref = refs[:2]
    s_ref = refs[2] if scaled else None
    rider_in = refs[2 + scaled:2 + scaled + n_riders]
    o_ref = refs[2 + scaled + n_riders]
    rider_out = refs[3 + scaled + n_riders:]
    acc = jnp.dot(x_ref[...], w_ref[...].astype(BF16), preferred_element_type=F32)
    if scaled:
        acc = acc * s_ref[...]
    o_ref[...] = acc.astype(o_ref.dtype)
    step = pl.program_id(0) * n_col_tiles + pl.program_id(1)
    for src, dst, n_blocks in zip(rider_in, rider_out, rider_blocks):
        @pl.when(step < n_blocks)
        def _(src=src, dst=dst):
            dst[...] = src[...].astype(BF16)


def _matmul(x_bf, w, out_dtype, *, n_out, wcol, colscale=None, riders=(), tm=1024, tn=512,
            name="matmul"):
    T, K = x_bf.shape
    n_col_tiles = n_out // tn
    in_specs = [
        pl.BlockSpec((tm, K), lambda i, j: (i, 0)),
        pl.BlockSpec((K, tn), lambda i, j: (0, wcol(j))),
    ]
    args = [x_bf, w]
    if colscale is not None:
        in_specs.append(pl.BlockSpec((1, tn), lambda i, j: (0, j)))
        args.append(colscale.reshape(1, n_out))
    rider_specs, rider_blocks = [], []
    for arr, block, axis in riders:
        n_blocks = arr.shape[axis] // block[axis]
        assert n_blocks <= (T // tm) * n_col_tiles

        def index_map(i, j, axis=axis, n_blocks=n_blocks):
            blk = jnp.minimum(i * n_col_tiles + j, n_blocks - 1)
            return tuple(blk if a == axis else 0 for a in range(2))

        rider_specs.append(pl.BlockSpec(block, index_map))
        rider_blocks.append(n_blocks)
        args.append(arr)
    outs = pl.pallas_call(
        functools.partial(_matmul_kernel, scaled=colscale is not None,
                          rider_blocks=tuple(rider_blocks), n_col_tiles=n_col_tiles),
        grid=(T // tm, n_col_tiles),
        in_specs=in_specs + rider_specs,
        out_specs=[pl.BlockSpec((tm, tn), lambda i, j: (i, j))] + rider_specs,
        out_shape=[jax.ShapeDtypeStruct((T, n_out), out_dtype)]
        + [jax.ShapeDtypeStruct(arr.shape, BF16) for arr, _, _ in riders],
        compiler_params=_params(("arbitrary", "arbitrary")),
        name=name,
    )(*args)
    return outs if riders else outs[0]


def _main_wcol(tn):
    per_sec = DIL_OUT_W // tn
    first = OFF_LQ // tn
    n_pat = len(DIL_PATTERNS)

    def wcol(j):
        k = j - first
        dil_tile = first + (k // per_sec) * (n_pat * per_sec) + k % per_sec
        gate_tile = j + (n_pat - 1) * 3 * per_sec
        return jnp.where(j < first, j, jnp.where(k < 3 * per_sec, dil_tile, gate_tile))

    return wcol


def _group_wcol(group, tn):
    per_sec = DIL_OUT_W // tn

    def wcol(j):
        return (OFF_LQ + group * DIL_OUT_W) // tn + (j // per_sec) * (DIL_W // tn) + j % per_sec

    return wcol


def _alibi_columns(pos, coefs, q_side):
    lane = lax.broadcasted_iota(jnp.int32, pos.shape, 1)
    piece = jnp.where(lane < 3, lane, jnp.where(lane < 6, lane - 3,
                                                jnp.where(lane < 9, lane - 6, lane - 9)))
    cv = jnp.where(piece == 0, coefs[0], jnp.where(piece == 1, coefs[1], coefs[2]))
    hi = (pos - (pos & (POS_SPLIT - 1))).astype(F32)
    lo = (pos & (POS_SPLIT - 1)).astype(F32)
    n = 2 * COEF_PIECES
    if q_side:
        cols = jnp.where(lane < COEF_PIECES, -hi, jnp.where(lane < n, -lo, cv))
    else:
        cols = jnp.where(lane < n, cv, jnp.where(lane < n + COEF_PIECES, hi, lo))
    return jnp.where(lane < 2 * n, cols, 0.0)


def _diff_attn_kernel(coef_ref, lam_ref, g_ref, q_ref, k_ref, v_ref, o_ref,
                      kaug_sc, qaug_sc, corr_sc, sa_sc, sb_sc, pa_sc, pb_sc, m_sc, l_sc,
                      alphaa_sc, alphab_sc, acc_sc, *, tq, rb, seq, lambda_init):
    h = pl.program_id(1)
    qi = pl.program_id(2)
    dh = DIFF_HEAD_DIM
    n_chunks = seq // tq
    coefs = [coef_ref[h, t] for t in range(COEF_PIECES)]
    row_iota = lax.broadcasted_iota(jnp.int32, (tq, dh), 0)

    @pl.when(qi == 0)
    def _():
        def build(t, carry):
            r0 = pl.multiple_of(t * tq, tq)
            ak = _alibi_columns(row_iota + r0, coefs, q_side=False).astype(BF16)
            for c in range(2):
                kaug_sc[c, pl.ds(r0, tq), 0:dh] = k_ref[pl.ds(r0, tq), c * dh:(c + 1) * dh]
                kaug_sc[c, pl.ds(r0, tq), dh:2 * dh] = ak
            return carry

        lax.fori_loop(0, n_chunks, build, 0)
        row = lax.broadcasted_iota(jnp.int32, (tq, tq), 0)
        col = lax.broadcasted_iota(jnp.int32, (tq, tq), 1)
        slope2 = coefs[0] + coefs[1] + coefs[2]
        corr_sc[0] = jnp.zeros((tq, tq), F32)
        corr_sc[1] = (2.0 * slope2) * jnp.minimum(row - col, 0).astype(F32)

    aq = _alibi_columns(row_iota + qi * tq, coefs, q_side=True)
    for c in range(2):
        q_c = q_ref[:, c * dh:(c + 1) * dh]
        qaug_sc[0, c, :, 0:dh] = q_c
        qaug_sc[0, c, :, dh:2 * dh] = aq.astype(BF16)
        qaug_sc[1, c, :, 0:dh] = q_c
        qaug_sc[1, c, :, dh:2 * dh] = (-aq).astype(BF16)

    m_sc[...] = jnp.full(m_sc.shape, -jnp.inf, F32)
    l_sc[...] = jnp.zeros(l_sc.shape, F32)
    acc_sc[...] = jnp.zeros(acc_sc.shape, F32)

    n_slabs = tq // V7X_LANES
    last = n_chunks - 1

    def scores(kc, s_ref):
        ks = pl.multiple_of(kc * tq, tq)
        side = jnp.where(kc > qi, 1, 0)
        for c in range(2):
            s_ref[c] = lax.dot_general(
                qaug_sc[side, c], kaug_sc[c, pl.ds(ks, tq), :],
                (((1,), (1,)), ((), ())), preferred_element_type=F32)

    def softmax(kc, s_ref, p_ref, alpha_ref, near_diag):
        on_diag = jnp.where(kc == qi, 1, 0)
        for c in range(2):
            for b in range(tq // rb):
                rows = slice(b * rb, (b + 1) * rb)
                sb = s_ref[c, rows, :]
                if near_diag:
                    sb = sb + corr_sc[on_diag, rows, :]
                slabs = [sb[:, t * V7X_LANES:(t + 1) * V7X_LANES] for t in range(n_slabs)]
                mx = functools.reduce(jnp.maximum, slabs)
                m_old = m_sc[c, rows, :]
                m_new = jnp.maximum(m_old, jnp.broadcast_to(
                    jnp.max(mx, axis=-1, keepdims=True), (rb, V7X_LANES)))
                a = jnp.exp2(m_old - m_new)
                ps = [jnp.exp2(sl - m_new) for sl in slabs]
                l_sc[c, rows, :] = a * l_sc[c, rows, :] + functools.reduce(jnp.add, ps)
                p_ref[c, rows, :] = jnp.concatenate(ps, axis=-1).astype(BF16)
                alpha_ref[c, rows, :] = a
                m_sc[c, rows, :] = m_new

    def accumulate(kc, p_ref, alpha_ref):
        ks = pl.multiple_of(kc * tq, tq)
        for c in range(2):
            alpha = alpha_ref[c]
            acc_sc[c] = jnp.concatenate([alpha, alpha], axis=-1) * acc_sc[c] + jnp.dot(
                p_ref[c], v_ref[pl.ds(ks, tq), :], preferred_element_type=F32)

    def pair_body(kc, near_diag):
        accumulate(jnp.maximum(kc - 1, 0), pb_sc, alphab_sc)
        scores(kc + 1, sb_sc)
        softmax(kc, sa_sc, pa_sc, alphaa_sc, near_diag)
        accumulate(kc, pa_sc, alphaa_sc)
        scores(jnp.minimum(kc + 2, last), sa_sc)
        softmax(kc + 1, sb_sc, pb_sc, alphab_sc, near_diag)

    def pair(t, carry):
        kc = 2 * t
        lax.cond(t == qi // 2,
                 functools.partial(pair_body, kc, True),
                 functools.partial(pair_body, kc, False))
        return carry

    pb_sc[...] = jnp.zeros(pb_sc.shape, BF16)
    alphab_sc[...] = jnp.ones(alphab_sc.shape, F32)
    scores(0, sa_sc)
    lax.fori_loop(0, n_chunks // 2, pair, 0)
    accumulate(last, pb_sc, alphab_sc)

    lam_rows = lam_ref[...]
    lam = (jnp.exp(jnp.sum(lam_rows[0:1] * lam_rows[1:2], axis=-1, keepdims=True))
           - jnp.exp(jnp.sum(lam_rows[2:3] * lam_rows[3:4], axis=-1, keepdims=True))
           + lambda_init)
    l0 = jnp.sum(l_sc[0], axis=-1, keepdims=True)
    l1 = jnp.sum(l_sc[1], axis=-1, keepdims=True)
    o = acc_sc[0] / l0 - lam * (acc_sc[1] / l1)
    ms = jnp.mean(o * o, axis=-1, keepdims=True)
    o = o * lax.rsqrt(ms + SUBLN_EPS) * g_ref[...] * (1.0 - lambda_init)
    o_ref[...] = o.astype(o_ref.dtype)


def _diff_attention(z, coefs, lam_rows, subln_g, *, batch, seq, lambda_init, tq=512, rb=32):
    T = z.shape[0]
    e = 2 * DIFF_HEAD_DIM
    nq = seq // tq
    qspec = pl.BlockSpec((tq, e), lambda b, h, i: (b * nq + i, Z_DQ // e + h))
    kspec = pl.BlockSpec((seq, e), lambda b, h, i: (b, Z_DK // e + h))
    vspec = pl.BlockSpec((seq, e), lambda b, h, i: (b, Z_DV // e + h))
    return pl.pallas_call(
        functools.partial(_diff_attn_kernel, tq=tq, rb=rb, seq=seq, lambda_init=lambda_init),
        grid=(batch, DIFF_HEADS, nq),
        in_specs=[
            pl.BlockSpec(memory_space=pltpu.SMEM),
            pl.BlockSpec((4, DIFF_HEAD_DIM), lambda b, h, i: (0, 0)),
            pl.BlockSpec((1, e), lambda b, h, i: (0, 0)),
            qspec, kspec, vspec,
        ],
        out_specs=pl.BlockSpec((tq, e), lambda b, h, i: (b * nq + i, h)),
        out_shape=jax.ShapeDtypeStruct((T, DIFF_V_W), BF16),
        scratch_shapes=[
            pltpu.VMEM((2, seq, e), BF16),
            pltpu.VMEM((2, 2, tq, e), BF16),
            pltpu.VMEM((2, tq, tq), F32),
            pltpu.VMEM((2, tq, tq), F32),
            pltpu.VMEM((2, tq, tq), F32),
            pltpu.VMEM((2, tq, tq), BF16),
            pltpu.VMEM((2, tq, tq), BF16),
            pltpu.VMEM((2, tq, V7X_LANES), F32),
            pltpu.VMEM((2, tq, V7X_LANES), F32),
            pltpu.VMEM((2, tq, V7X_LANES), F32),
            pltpu.VMEM((2, tq, V7X_LANES), F32),
            pltpu.VMEM((2, tq, e), F32),
        ],
        compiler_params=_params(("parallel", "parallel", "arbitrary")),
        name="diff_attn",
    )(coefs, lam_rows, subln_g.reshape(1, e), z, z, z)


def _dil_attn_kernel(slopes_ref, q_ref, k_ref, v_ref, o_ref, lse_ref, *, seq, dil, tq, group):
    h = pl.program_id(1)
    slope = slopes_ref[h]
    w = DIL_HALF_KEYS
    nk = tq + 2 * w
    length = seq // dil
    blocks_per_class = length // tq
    scale = DIL_HEAD_DIM ** -0.5
    row = lax.broadcasted_iota(jnp.int32, (tq, nk), 0)
    col = lax.broadcasted_iota(jnp.int32, (tq, nk), 1)
    col_minus_row = col - row

    def block_offsets(i):
        cls = i // blocks_per_class
        qs_local = (i % blocks_per_class) * tq
        ks_local = jnp.clip(qs_local - w, 0, length - nk)
        return cls, qs_local, ks_local

    def body(it, carry):
        blocks = [block_offsets(it * group + g) for g in range(group)]
        scores = []
        for cls, qs_local, ks_local in blocks:
            q = q_ref[pl.ds(pl.multiple_of(cls * length + qs_local, tq), tq), :]
            k = k_ref[pl.ds(pl.multiple_of(cls * length + ks_local, w), nk), :]
            scores.append(lax.dot_general(q, k, (((1,), (1,)), ((), ())),
                                          preferred_element_type=F32))
        for (cls, qs_local, ks_local), s in zip(blocks, scores):
            rel = jnp.abs(col_minus_row + (ks_local - qs_local))
            s = jnp.where(rel <= w, s * scale - slope * rel.astype(F32), NEG_INF)
            m = jnp.max(s, axis=-1, keepdims=True)
            e = jnp.exp(s - m)
            z = jnp.sum(e, axis=-1, keepdims=True)
            v = v_ref[pl.ds(pl.multiple_of(cls * length + ks_local, w), nk), :]
            o = jnp.dot(e.astype(BF16), v, preferred_element_type=F32) / z
            if dil == 1:
                rows = pl.ds(pl.multiple_of(qs_local, tq), tq)
            else:
                rows = pl.ds(qs_local * dil + cls, tq, stride=dil)
            o_ref[rows, :] = o
            lse_ref[rows, :] = jnp.broadcast_to(m + jnp.log(z), (tq, DIL_HEAD_DIM))
        return carry

    lax.fori_loop(0, seq // (tq * group), body, 0)


def _dilated_attention(qkv, offsets, slopes_g, *, dil, batch, seq, tq=128, group=16):
    T = qkv.shape[0]
    hd = DIL_HEAD_DIM

    def in_spec(off):
        return pl.BlockSpec((seq, hd), lambda b, h: (b, off // hd + h))

    out_spec = pl.BlockSpec((seq, hd), lambda b, h: (b, h))
    out_sds = jax.ShapeDtypeStruct((T, DIL_OUT_W), F32)
    return pl.pallas_call(
        functools.partial(_dil_attn_kernel, seq=seq, dil=dil, tq=tq, group=group),
        grid=(batch, DIL_GROUP_HEADS),
        in_specs=[pl.BlockSpec(memory_space=pltpu.SMEM)] + [in_spec(o) for o in offsets],
        out_specs=[out_spec, out_spec],
        out_shape=[out_sds, out_sds],
        compiler_params=_params(("parallel", "parallel")),
        name=f"dil_attn_{dil}",
    )(slopes_g, qkv, qkv, qkv)


def _dil_mix_kernel(o0_ref, o1_ref, o2_ref, l0_ref, l1_ref, l2_ref, ob_ref):
    l0, l1, l2 = l0_ref[...], l1_ref[...], l2_ref[...]
    m = jnp.maximum(jnp.maximum(l0, l1), l2)
    w0, w1, w2 = jnp.exp(l0 - m), jnp.exp(l1 - m), jnp.exp(l2 - m)
    ob = (w0 * o0_ref[...] + w1 * o1_ref[...] + w2 * o2_ref[...]) / (w0 + w1 + w2)
    ob_ref[...] = ob.astype(ob_ref.dtype)


def _dil_mix(dil_outs, dil_lses, *, tm=512):
    T, W = dil_outs[0].shape
    row = pl.BlockSpec((tm, W), lambda i: (i, 0))
    return pl.pallas_call(
        _dil_mix_kernel,
        grid=(T // tm,),
        in_specs=[row] * 6,
        out_specs=row,
        out_shape=jax.ShapeDtypeStruct((T, W), BF16),
        compiler_params=_params(("parallel",)),
        name="dil_mix",
    )(*dil_outs, *dil_lses)


def _merge_kernel(oa_ref, ob_ref, ga_ref, gb_ref, wa_ref, wb_ref, y_ref):
    ya = jnp.dot(oa_ref[...], wa_ref[...].astype(BF16), preferred_element_type=F32)
    yb = jnp.dot(ob_ref[...], wb_ref[...].astype(BF16), preferred_element_type=F32)
    y = (jax.nn.sigmoid(ga_ref[...].astype(F32)) * ya
         + jax.nn.sigmoid(gb_ref[...].astype(F32)) * yb)
    y_ref[...] = y.astype(y_ref.dtype)


def _merge(oa, ob, z, wa, wb, *, tm=1024, tn=512):
    T = oa.shape[0]
    return pl.pallas_call(
        _merge_kernel,
        grid=(T // tm, D_MODEL // tn),
        in_specs=[pl.BlockSpec((tm, DIFF_V_W), lambda i, j: (i, 0)),
                  pl.BlockSpec((tm, DIL_OUT_W), lambda i, j: (i, 0)),
                  pl.BlockSpec((tm, tn), lambda i, j: (i, Z_GA // tn + j)),
                  pl.BlockSpec((tm, tn), lambda i, j: (i, Z_GB // tn + j)),
                  pl.BlockSpec((DIFF_V_W, tn), lambda i, j: (0, j)),
                  pl.BlockSpec((DIL_OUT_W, tn), lambda i, j: (0, j))],
        out_specs=pl.BlockSpec((tm, tn), lambda i, j: (i, j)),
        out_shape=jax.ShapeDtypeStruct((T, D_MODEL), BF16),
        compiler_params=_params(("parallel", "parallel")),
        name="merge",
    )(oa, ob, z, z, wa, wb)


def _ple_kernel(x_ref, p_ref, wg_ref, wp_ref, o_ref):
    gate = jnp.dot(x_ref[...], wg_ref[...].astype(BF16), preferred_element_type=F32)
    proj = jnp.dot(p_ref[...].astype(BF16), wp_ref[...].astype(BF16),
                   preferred_element_type=F32)
    o_ref[...] = jax.nn.sigmoid(gate) * proj


def _ple(x_bf, p, wg, wp, *, tm=1024, tn=512):
    T, D = x_bf.shape
    return pl.pallas_call(
        _ple_kernel,
        grid=(T // tm, D // tn),
        in_specs=[
            pl.BlockSpec((tm, D), lambda i, j: (i, 0)),
            pl.BlockSpec((tm, PLE_DIM), lambda i, j: (i, 0)),
            pl.BlockSpec((D, tn), lambda i, j: (0, j)),
            pl.BlockSpec((PLE_DIM, tn), lambda i, j: (0, j)),
        ],
        out_specs=pl.BlockSpec((tm, tn), lambda i, j: (i, j)),
        out_shape=jax.ShapeDtypeStruct((T, D), F32),
        compiler_params=_params(("parallel", "parallel")),
        name="ple",
    )(x_bf, p, wg, wp)


def _alibi_slopes(n):
    return jnp.exp2(-8.0 * jnp.arange(1, n + 1, dtype=F32) / n)


def _bf16_pieces(v):
    pieces, rest = [], v
    for _ in range(COEF_PIECES):
        piece = rest.astype(BF16).astype(F32)
        pieces.append(piece)
        rest = rest - piece
    return jnp.stack(pieces, axis=-1)


def kernel(x, p, ffn1_w_in, ffn1_w_out, ln1_g, ln1_b, w_in, lam_q1, lam_k1, lam_q2, lam_k2,
           subln_g, w_branch_diff, w_branch_dil, w_mix_out, ln2_g, ln2_b, ffn2_w_in,
           ffn2_w_out, ln3_g, ln3_b, w_ple_gate, w_ple_proj, ln4_g, ln4_b):
    batch, seq, d = x.shape
    depth = ffn1_w_in.shape[0]
    T = batch * seq
    alpha = (2 * depth) ** 0.25
    n_pat = len(DIL_PATTERNS)
    tn = 512

    h = x.reshape(T, d)
    h_bf = h.astype(BF16)
    diff_coefs = _bf16_pieces(_alibi_slopes(DIFF_HEADS) * LOG2E)
    dil_slopes = _alibi_slopes(n_pat * DIL_GROUP_HEADS).reshape(DIL_GROUP_HEADS, n_pat)
    colscale = jnp.ones((Z_W,), F32).at[Z_DQ:Z_DK].set(DIFF_HEAD_DIM ** -0.5 * LOG2E)

    for i in range(depth):
        lambda_init = 0.8 - 0.6 * math.exp(-0.3 * i)

        f = _ffn(h_bf, ffn1_w_in[i].astype(BF16), ffn1_w_out[i].astype(BF16))
        dils = tuple(dil for _window, dil in DIL_PATTERNS if dil > 1)
        h, h_bf, h_cm = _res_ln_class_major(h, f, ln1_g[i], ln1_b[i], alpha=alpha, scale=0.5,
                                            batch=batch, seq=seq, dils=dils)
        h_class_major = dict(zip(dils, h_cm))

        z, ffn2_w_in_bf, ffn2_w_out_bf = _matmul(
            h_bf, w_in[i], BF16, n_out=Z_W, wcol=_main_wcol(tn), colscale=colscale, tn=tn,
            riders=((ffn2_w_in[i], (d, V7X_LANES), 1), (ffn2_w_out[i], (RIDER_ROWS, d), 0)),
            name="in_proj")
        lam_rows = jnp.stack([lam_q1[i], lam_k1[i], lam_q2[i], lam_k2[i]]).astype(F32)
        oa = _diff_attention(z, diff_coefs, lam_rows, subln_g[i].astype(F32),
                             batch=batch, seq=seq, lambda_init=lambda_init)

        dil_outs, dil_lses = [], []
        for g, (_window, dil) in enumerate(DIL_PATTERNS):
            slopes_g = dil_slopes[:, g] * dil
            if dil == 1:
                o_g, lse_g = _dilated_attention(z, (Z_LQ, Z_LK, Z_LV), slopes_g, dil=dil,
                                                batch=batch, seq=seq)
            else:
                qkv = _matmul(h_class_major[dil], w_in[i], BF16,
                              n_out=3 * DIL_OUT_W, wcol=_group_wcol(g, tn), tn=tn,
                              name=f"in_proj_dil{dil}")
                o_g, lse_g = _dilated_attention(qkv, (0, DIL_OUT_W, 2 * DIL_OUT_W), slopes_g,
                                                dil=dil, batch=batch, seq=seq)
            dil_outs.append(o_g)
            dil_lses.append(lse_g)

        y = _merge(oa, _dil_mix(dil_outs, dil_lses), z, w_branch_diff[i], w_branch_dil[i])
        mix = _matmul(y, w_mix_out[i], F32, n_out=d, wcol=lambda j: j, tn=tn, name="mix_out")
        h, h_bf = _res_ln(h, mix, ln2_g[i], ln2_b[i], alpha=alpha, scale=1.0)

        f = _ffn(h_bf, ffn2_w_in_bf, ffn2_w_out_bf)
        h, h_bf = _res_ln(h, f, ln3_g[i], ln3_b[i], alpha=alpha, scale=0.5)

        ple = _ple(h_bf, p[i].reshape(T, PLE_DIM), w_ple_gate[i], w_ple_proj[i])
        if i + 1 < depth:
            h, h_bf = _res_ln(h, ple, ln4_g[i], ln4_b[i], alpha=alpha, scale=1.0)
        else:
            h = _res_ln(h, ple, ln4_g[i], ln4_b[i], alpha=alpha, scale=1.0, with_bf16=False)

    return h.reshape(batch, seq, d)
```

```python
import functools
import math

import jax
import jax.numpy as jnp
from jax import lax
from jax.experimental import pallas as pl
from jax.experimental.pallas import tpu as pltpu

F32 = jnp.float32
BF16 = jnp.bfloat16

D_MODEL = 4096
PLE_DIM = 256
D_FF = 11008
DIFF_HEADS = 8
DIFF_HEAD_DIM = 128
DIL_PATTERNS = ((128, 1), (512, 4), (2048, 16))
DIL_GROUP_HEADS = 8
DIL_HEAD_DIM = 128
DIL_HALF_KEYS = 64
LN_EPS = 1e-5
SUBLN_EPS = 1e-5
NEG_INF = -1e30

DIFF_QK_W = DIFF_HEADS * 2 * DIFF_HEAD_DIM
DIFF_V_W = DIFF_HEADS * 2 * DIFF_HEAD_DIM
DIL_W = len(DIL_PATTERNS) * DIL_GROUP_HEADS * DIL_HEAD_DIM
DIL_OUT_W = DIL_GROUP_HEADS * DIL_HEAD_DIM
IN_PROJ_W = 2 * DIFF_QK_W + DIFF_V_W + 3 * DIL_W + 2 * D_MODEL

OFF_DQ = 0
OFF_DK = OFF_DQ + DIFF_QK_W
OFF_DV = OFF_DK + DIFF_QK_W
OFF_LQ = OFF_DV + DIFF_V_W
OFF_LK = OFF_LQ + DIL_W
OFF_LV = OFF_LK + DIL_W
OFF_GA = OFF_LV + DIL_W
OFF_GB = OFF_GA + D_MODEL

Z_DQ = 0
Z_DK = Z_DQ + DIFF_QK_W
Z_DV = Z_DK + DIFF_QK_W
Z_LQ = Z_DV + DIFF_V_W
Z_LK = Z_LQ + DIL_OUT_W
Z_LV = Z_LK + DIL_OUT_W
Z_GA = Z_LV + DIL_OUT_W
Z_GB = Z_GA + D_MODEL
Z_W = Z_GB + D_MODEL

V7X_LANES = 128
V7X_VMEM_BYTES = 64 * 1024 * 1024
V7X_VMEM_REQUEST = 56 * 1024 * 1024

FFN_BLOCK = 256

LOG2E = math.log2(math.e)
COEF_PIECES = 3
POS_SPLIT = 64
RIDER_ROWS = 64


def _params(semantics, vmem_bytes=V7X_VMEM_REQUEST):
    return pltpu.CompilerParams(dimension_semantics=semantics, vmem_limit_bytes=vmem_bytes)


def _ffn_kernel(x_ref, wga_ref, wgb_ref, wua_ref, wub_ref, woa_ref, wob_ref, o_ref, *, n_blocks):
    j = pl.program_id(1)

    @pl.when(j == 0)
    def _():
        o_ref[...] = jnp.zeros(o_ref.shape, F32)

    x = x_ref[...]

    def hidden(wg_ref, wu_ref):
        g = jnp.dot(x, wg_ref[...], preferred_element_type=F32)
        u = jnp.dot(x, wu_ref[...], preferred_element_type=F32)
        return g * jax.nn.sigmoid(g) * u

    ha = hidden(wga_ref, wua_ref).astype(BF16)
    hb = jnp.where(2 * j + 1 < n_blocks, hidden(wgb_ref, wub_ref), 0.0).astype(BF16)
    part = jnp.dot(ha, woa_ref[...], preferred_element_type=F32)
    part += jnp.dot(hb, wob_ref[...], preferred_element_type=F32)
    o_ref[...] += part


def _ffn(x_bf, w_in_bf, w_out_bf, *, tm=512, tb=FFN_BLOCK):
    T, D = x_bf.shape
    n_blocks = w_out_bf.shape[0] // tb
    steps = -(-n_blocks // 2)

    def first(j):
        return 2 * j

    def second(j):
        return jnp.minimum(2 * j + 1, n_blocks - 1)

    def col_spec(blk, off):
        return pl.BlockSpec((D, tb), lambda i, j: (0, off + blk(j)))

    def row_spec(blk):
        return pl.BlockSpec((tb, D), lambda i, j: (blk(j), 0))

    return pl.pallas_call(
        functools.partial(_ffn_kernel, n_blocks=n_blocks),
        grid=(T // tm, steps),
        in_specs=[
            pl.BlockSpec((tm, D), lambda i, j: (i, 0)),
            col_spec(first, 0), col_spec(second, 0),
            col_spec(first, n_blocks), col_spec(second, n_blocks),
            row_spec(first), row_spec(second),
        ],
        out_specs=pl.BlockSpec((tm, D), lambda i, j: (i, 0)),
        out_shape=jax.ShapeDtypeStruct((T, D), F32),
        compiler_params=_params(("parallel", "arbitrary")),
        name="ffn",
    )(x_bf, w_in_bf, w_in_bf, w_in_bf, w_in_bf, w_out_bf, w_out_bf)


def _layer_norm_rows(y, g_ref, b_ref):
    mu = jnp.mean(y, axis=-1, keepdims=True)
    yc = y - mu
    var = jnp.mean(yc * yc, axis=-1, keepdims=True)
    return yc * lax.rsqrt(var + LN_EPS) * g_ref[...] + b_ref[...]


def _ln_kernel(h_ref, br_ref, g_ref, b_ref, o_ref, obf_ref, *, alpha, scale):
    out = _layer_norm_rows(alpha * h_ref[...] + scale * br_ref[...], g_ref, b_ref)
    o_ref[...] = out
    obf_ref[...] = out.astype(BF16)


def _ln_class_major_kernel(h_ref, br_ref, g_ref, b_ref, o_ref, obf_ref, *rest, alpha, scale, dils):
    cm_refs, slab_sc = rest[:-1], rest[-1]
    out = _layer_norm_rows(alpha * h_ref[...] + scale * br_ref[...], g_ref, b_ref)
    o_ref[...] = out
    obf_ref[...] = out.astype(BF16)
    tm, d = out.shape
    for s in range(d // V7X_LANES):
        slab_sc[s] = out[:, s * V7X_LANES:(s + 1) * V7X_LANES]
    for dil, cm_ref in zip(dils, cm_refs):
        for r in range(dil):
            for s in range(d // V7X_LANES):
                cm_ref[r, :, s * V7X_LANES:(s + 1) * V7X_LANES] = slab_sc[
                    s, pl.ds(r, tm // dil, stride=dil), :].astype(BF16)


def _res_ln_class_major(h, branch, g, b, *, alpha, scale, batch, seq, dils, tm=256):
    T, D = h.shape
    tiles = seq // tm
    row = pl.BlockSpec((tm, D), lambda i: (i, 0))
    vec = pl.BlockSpec((1, D), lambda i: (0, 0))
    cm_specs = [pl.BlockSpec((None, dil, tm // dil, D), lambda i: (i // tiles, 0, i % tiles, 0))
                for dil in dils]
    cm_shapes = [jax.ShapeDtypeStruct((batch, dil, seq // dil, D), BF16) for dil in dils]
    outs = pl.pallas_call(
        functools.partial(_ln_class_major_kernel, alpha=alpha, scale=scale, dils=dils),
        grid=(T // tm,),
        in_specs=[row, row, vec, vec],
        out_specs=[row, row] + cm_specs,
        out_shape=[jax.ShapeDtypeStruct((T, D), F32), jax.ShapeDtypeStruct((T, D), BF16)]
        + cm_shapes,
        scratch_shapes=[pltpu.VMEM((D // V7X_LANES, tm, V7X_LANES), F32)],
        compiler_params=_params(("parallel",)),
        name="res_ln_class_major",
    )(h, branch, g.reshape(1, D), b.reshape(1, D))
    return outs[0], outs[1], [cm.reshape(T, D) for cm in outs[2:]]


def _ln_f32_kernel(h_ref, br_ref, g_ref, b_ref, o_ref, *, alpha, scale):
    o_ref[...] = _layer_norm_rows(alpha * h_ref[...] + scale * br_ref[...], g_ref, b_ref)


def _res_ln(h, branch, g, b, *, alpha, scale, with_bf16=True, tm=256):
    T, D = h.shape
    row = pl.BlockSpec((tm, D), lambda i: (i, 0))
    vec = pl.BlockSpec((1, D), lambda i: (0, 0))
    f32_out = jax.ShapeDtypeStruct((T, D), F32)
    return pl.pallas_call(
        functools.partial(_ln_kernel if with_bf16 else _ln_f32_kernel, alpha=alpha, scale=scale),
        grid=(T // tm,),
        in_specs=[row, row, vec, vec],
        out_specs=[row, row] if with_bf16 else row,
        out_shape=[f32_out, jax.ShapeDtypeStruct((T, D), BF16)] if with_bf16 else f32_out,
        compiler_params=_params(("parallel",)),
        name="res_ln",
    )(h, branch, g.reshape(1, D), b.reshape(1, D))


def _matmul_kernel(*refs, scaled, rider_blocks, n_col_tiles):
    n_riders = len(rider_blocks)
    x_ref, w_ref = refs[:2]
    s_ref = refs[2] if scaled else None
    rider_in = refs[2 + scaled:2 + scaled + n_riders]
    o_ref = refs[2 + scaled + n_riders]
    rider_out = refs[3 + scaled + n_riders:]
    acc = jnp.dot(x_ref[...], w_ref[...].astype(BF16), preferred_element_type=F32)
    if scaled:
        acc = acc * s_ref[...]
    o_ref[...] = acc.astype(o_ref.dtype)
    step = pl.program_id(0) * n_col_tiles + pl.program_id(1)
    for src, dst, n_blocks in zip(rider_in, rider_out, rider_blocks):
        @pl.when(step < n_blocks)
        def _(src=src, dst=dst):
            dst[...] = src[...].astype(BF16)


def _matmul(x_bf, w, out_dtype, *, n_out, wcol, colscale=None, riders=(), tm=1024, tn=512,
            name="matmul"):
    T, K = x_bf.shape
    n_col_tiles = n_out // tn
    in_specs = [
        pl.BlockSpec((tm, K), lambda i, j: (i, 0)),
        pl.BlockSpec((K, tn), lambda i, j: (0, wcol(j))),
    ]
    args = [x_bf, w]
    if colscale is not None:
        in_specs.append(pl.BlockSpec((1, tn), lambda i, j: (0, j)))
        args.append(colscale.reshape(1, n_out))
    rider_specs, rider_blocks = [], []
    for arr, block, axis in riders:
        n_blocks = arr.shape[axis] // block[axis]
        assert n_blocks <= (T // tm) * n_col_tiles

        def index_map(i, j, axis=axis, n_blocks=n_blocks):
            blk = jnp.minimum(i * n_col_tiles + j, n_blocks - 1)
            return tuple(blk if a == axis else 0 for a in range(2))

        rider_specs.append(pl.BlockSpec(block, index_map))
        rider_blocks.append(n_blocks)
        args.append(arr)
    outs = pl.pallas_call(
        functools.partial(_matmul_kernel, scaled=colscale is not None,
                          rider_blocks=tuple(rider_blocks), n_col_tiles=n_col_tiles),
        grid=(T // tm, n_col_tiles),
        in_specs=in_specs + rider_specs,
        out_specs=[pl.BlockSpec((tm, tn), lambda i, j: (i, j))] + rider_specs,
        out_shape=[jax.ShapeDtypeStruct((T, n_out), out_dtype)]
        + [jax.ShapeDtypeStruct(arr.shape, BF16) for arr, _, _ in riders],
        compiler_params=_params(("arbitrary", "arbitrary")),
        name=name,
    )(*args)
    return outs if riders else outs[0]


def _main_wcol(tn):
    per_sec = DIL_OUT_W // tn
    first = OFF_LQ // tn
    n_pat = len(DIL_PATTERNS)

    def wcol(j):
        k = j - first
        dil_tile = first + (k // per_sec) * (n_pat * per_sec) + k % per_sec
        gate_tile = j + (n_pat - 1) * 3 * per_sec
        return jnp.where(j < first, j, jnp.where(k < 3 * per_sec, dil_tile, gate_tile))

    return wcol


def _group_wcol(group, tn):
    per_sec = DIL_OUT_W // tn

    def wcol(j):
        return (OFF_LQ + group * DIL_OUT_W) // tn + (j // per_sec) * (DIL_W // tn) + j % per_sec

    return wcol


def _alibi_columns(pos, coefs, q_side):
    lane = lax.broadcasted_iota(jnp.int32, pos.shape, 1)
    piece = jnp.where(lane < 3, lane, jnp.where(lane < 6, lane - 3,
                                                jnp.where(lane < 9, lane - 6, lane - 9)))
    cv = jnp.where(piece == 0, coefs[0], jnp.where(piece == 1, coefs[1], coefs[2]))
    hi = (pos - (pos & (POS_SPLIT - 1))).astype(F32)
    lo = (pos & (POS_SPLIT - 1)).astype(F32)
    n = 2 * COEF_PIECES
    if q_side:
        cols = jnp.where(lane < COEF_PIECES, -hi, jnp.where(lane < n, -lo, cv))
    else:
        cols = jnp.where(lane < n, cv, jnp.where(lane < n + COEF_PIECES, hi, lo))
    return jnp.where(lane < 2 * n, cols, 0.0)


def _diff_attn_kernel(coef_ref, lam_ref, g_ref, q_ref, k_ref, v_ref, o_ref,
                      kaug_sc, qaug_sc, corr_sc, sa_sc, sb_sc, pa_sc, pb_sc, m_sc, l_sc,
                      alphaa_sc, alphab_sc, acc_sc, *, tq, rb, seq, lambda_init):
    h = pl.program_id(1)
    qi = pl.program_id(2)
    dh = DIFF_HEAD_DIM
    n_chunks = seq // tq
    coefs = [coef_ref[h, t] for t in range(COEF_PIECES)]
    row_iota = lax.broadcasted_iota(jnp.int32, (tq, dh), 0)

    @pl.when(qi == 0)
    def _():
        def build(t, carry):
            r0 = pl.multiple_of(t * tq, tq)
            ak = _alibi_columns(row_iota + r0, coefs, q_side=False).astype(BF16)
            for c in range(2):
                kaug_sc[c, pl.ds(r0, tq), 0:dh] = k_ref[pl.ds(r0, tq), c * dh:(c + 1) * dh]
                kaug_sc[c, pl.ds(r0, tq), dh:2 * dh] = ak
            return carry

        lax.fori_loop(0, n_chunks, build, 0)
        row = lax.broadcasted_iota(jnp.int32, (tq, tq), 0)
        col = lax.broadcasted_iota(jnp.int32, (tq, tq), 1)
        slope2 = coefs[0] + coefs[1] + coefs[2]
        corr_sc[0] = jnp.zeros((tq, tq), F32)
        corr_sc[1] = (2.0 * slope2) * jnp.minimum(row - col, 0).astype(F32)

    aq = _alibi_columns(row_iota + qi * tq, coefs, q_side=True)
    for c in range(2):
        q_c = q_ref[:, c * dh:(c + 1) * dh]
        qaug_sc[0, c, :, 0:dh] = q_c
        qaug_sc[0, c, :, dh:2 * dh] = aq.astype(BF16)
        qaug_sc[1, c, :, 0:dh] = q_c
        qaug_sc[1, c, :, dh:2 * dh] = (-aq).astype(BF16)

    m_sc[...] = jnp.full(m_sc.shape, -jnp.inf, F32)
    l_sc[...] = jnp.zeros(l_sc.shape, F32)
    acc_sc[...] = jnp.zeros(acc_sc.shape, F32)

    n_slabs = tq // V7X_LANES
    last = n_chunks - 1

    def scores(kc, s_ref):
        ks = pl.multiple_of(kc * tq, tq)
        side = jnp.where(kc > qi, 1, 0)
        for c in range(2):
            s_ref[c] = lax.dot_general(
                qaug_sc[side, c], kaug_sc[c, pl.ds(ks, tq), :],
                (((1,), (1,)), ((), ())), preferred_element_type=F32)

    def softmax(kc, s_ref, p_ref, alpha_ref, near_diag):
        on_diag = jnp.where(kc == qi, 1, 0)
        for c in range(2):
            for b in range(tq // rb):
                rows = slice(b * rb, (b + 1) * rb)
                sb = s_ref[c, rows, :]
                if near_diag:
                    sb = sb + corr_sc[on_diag, rows, :]
                slabs = [sb[:, t * V7X_LANES:(t + 1) * V7X_LANES] for t in range(n_slabs)]
                mx = functools.reduce(jnp.maximum, slabs)
                m_old = m_sc[c, rows, :]
                m_new = jnp.maximum(m_old, jnp.broadcast_to(
                    jnp.max(mx, axis=-1, keepdims=True), (rb, V7X_LANES)))
                a = jnp.exp2(m_old - m_new)
                ps = [jnp.exp2(sl - m_new) for sl in slabs]
                l_sc[c, rows, :] = a * l_sc[c, rows, :] + functools.reduce(jnp.add, ps)
                p_ref[c, rows, :] = jnp.concatenate(ps, axis=-1).astype(BF16)
                alpha_ref[c, rows, :] = a
                m_sc[c, rows, :] = m_new

    def accumulate(kc, p_ref, alpha_ref):
        ks = pl.multiple_of(kc * tq, tq)
        for c in range(2):
            alpha = alpha_ref[c]
            acc_sc[c] = jnp.concatenate([alpha, alpha], axis=-1) * acc_sc[c] + jnp.dot(
                p_ref[c], v_ref[pl.ds(ks, tq), :], preferred_element_type=F32)

    def pair_body(kc, near_diag, first=False, final=False):
        if not first:
            accumulate(kc - 1, pb_sc, alphab_sc)
        scores(kc + 1, sb_sc)
        softmax(kc, sa_sc, pa_sc, alphaa_sc, near_diag)
        accumulate(kc, pa_sc, alphaa_sc)
        if not final:
            scores(kc + 2, sa_sc)
        softmax(kc + 1, sb_sc, pb_sc, alphab_sc, near_diag)

    def pair(t, first=False, final=False):
        lax.cond(t == qi // 2,
                 functools.partial(pair_body, 2 * t, True, first, final),
                 functools.partial(pair_body, 2 * t, False, first, final))

    def middle_pair(t, carry):
        pair(t)
        return carry

    n_pairs = n_chunks // 2
    scores(0, sa_sc)
    pair(0, first=True)
    lax.fori_loop(1, n_pairs - 1, middle_pair, 0)
    pair(n_pairs - 1, final=True)
    accumulate(last, pb_sc, alphab_sc)

    lam_rows = lam_ref[...]
    lam = (jnp.exp(jnp.sum(lam_rows[0:1] * lam_rows[1:2], axis=-1, keepdims=True))
           - jnp.exp(jnp.sum(lam_rows[2:3] * lam_rows[3:4], axis=-1, keepdims=True))
           + lambda_init)
    l0 = jnp.sum(l_sc[0], axis=-1, keepdims=True)
    l1 = jnp.sum(l_sc[1], axis=-1, keepdims=True)
    o = acc_sc[0] / l0 - lam * (acc_sc[1] / l1)
    ms = jnp.mean(o * o, axis=-1, keepdims=True)
    o = o * lax.rsqrt(ms + SUBLN_EPS) * g_ref[...] * (1.0 - lambda_init)
    o_ref[...] = o.astype(o_ref.dtype)


def _diff_attention(z, coefs, lam_rows, subln_g, *, batch, seq, lambda_init, tq=512, rb=32):
    T = z.shape[0]
    e = 2 * DIFF_HEAD_DIM
    nq = seq // tq
    qspec = pl.BlockSpec((tq, e), lambda b, h, i: (b * nq + i, Z_DQ // e + h))
    kspec = pl.BlockSpec((seq, e), lambda b, h, i: (b, Z_DK // e + h))
    vspec = pl.BlockSpec((seq, e), lambda b, h, i: (b, Z_DV // e + h))
    return pl.pallas_call(
        functools.partial(_diff_attn_kernel, tq=tq, rb=rb, seq=seq, lambda_init=lambda_init),
        grid=(batch, DIFF_HEADS, nq),
        in_specs=[
            pl.BlockSpec(memory_space=pltpu.SMEM),
            pl.BlockSpec((4, DIFF_HEAD_DIM), lambda b, h, i: (0, 0)),
            pl.BlockSpec((1, e), lambda b, h, i: (0, 0)),
            qspec, kspec, vspec,
        ],
        out_specs=pl.BlockSpec((tq, e), lambda b, h, i: (b * nq + i, h)),
        out_shape=jax.ShapeDtypeStruct((T, DIFF_V_W), BF16),
        scratch_shapes=[
            pltpu.VMEM((2, seq, e), BF16),
            pltpu.VMEM((2, 2, tq, e), BF16),
            pltpu.VMEM((2, tq, tq), F32),
            pltpu.VMEM((2, tq, tq), F32),
            pltpu.VMEM((2, tq, tq), F32),
            pltpu.VMEM((2, tq, tq), BF16),
            pltpu.VMEM((2, tq, tq), BF16),
            pltpu.VMEM((2, tq, V7X_LANES), F32),
            pltpu.VMEM((2, tq, V7X_LANES), F32),
            pltpu.VMEM((2, tq, V7X_LANES), F32),
            pltpu.VMEM((2, tq, V7X_LANES), F32),
            pltpu.VMEM((2, tq, e), F32),
        ],
        compiler_params=_params(("parallel", "parallel", "arbitrary")),
        name="diff_attn",
    )(coefs, lam_rows, subln_g.reshape(1, e), z, z, z)


def _dil_attn_kernel(slopes_ref, q_ref, k_ref, v_ref, o_ref, lse_ref, *, seq, dil, tq, group):
    h = pl.program_id(1)
    slope = slopes_ref[h]
    w = DIL_HALF_KEYS
    nk = tq + 2 * w
    length = seq // dil
    blocks_per_class = length // tq
    scale = DIL_HEAD_DIM ** -0.5
    row = lax.broadcasted_iota(jnp.int32, (tq, nk), 0)
    col = lax.broadcasted_iota(jnp.int32, (tq, nk), 1)
    col_minus_row = col - row

    def block_offsets(i):
        cls = i // blocks_per_class
        qs_local = (i % blocks_per_class) * tq
        ks_local = jnp.clip(qs_local - w, 0, length - nk)
        return cls, qs_local, ks_local

    def body(it, carry):
        blocks = [block_offsets(it * group + g) for g in range(group)]
        scores = []
        for cls, qs_local, ks_local in blocks:
            q = q_ref[pl.ds(pl.multiple_of(cls * length + qs_local, tq), tq), :]
            k = k_ref[pl.ds(pl.multiple_of(cls * length + ks_local, w), nk), :]
            scores.append(lax.dot_general(q, k, (((1,), (1,)), ((), ())),
                                          preferred_element_type=F32))
        for (cls, qs_local, ks_local), s in zip(blocks, scores):
            rel = jnp.abs(col_minus_row + (ks_local - qs_local))
            s = jnp.where(rel <= w, s * scale - slope * rel.astype(F32), NEG_INF)
            m = jnp.max(s, axis=-1, keepdims=True)
            e = jnp.exp(s - m)
            z = jnp.sum(e, axis=-1, keepdims=True)
            v = v_ref[pl.ds(pl.multiple_of(cls * length + ks_local, w), nk), :]
            o = jnp.dot(e.astype(BF16), v, preferred_element_type=F32) / z
            if dil == 1:
                rows = pl.ds(pl.multiple_of(qs_local, tq), tq)
            else:
                rows = pl.ds(qs_local * dil + cls, tq, stride=dil)
            o_ref[rows, :] = o
            lse_ref[rows, :] = jnp.broadcast_to(m + jnp.log(z), (tq, DIL_HEAD_DIM))
        return carry

    lax.fori_loop(0, seq // (tq * group), body, 0)


def _dilated_attention(qkv, offsets, slopes_g, *, dil, batch, seq, tq=128, group=16):
    T = qkv.shape[0]
    hd = DIL_HEAD_DIM

    def in_spec(off):
        return pl.BlockSpec((seq, hd), lambda b, h: (b, off // hd + h))

    out_spec = pl.BlockSpec((seq, hd), lambda b, h: (b, h))
    out_sds = jax.ShapeDtypeStruct((T, DIL_OUT_W), F32)
    return pl.pallas_call(
        functools.partial(_dil_attn_kernel, seq=seq, dil=dil, tq=tq, group=group),
        grid=(batch, DIL_GROUP_HEADS),
        in_specs=[pl.BlockSpec(memory_space=pltpu.SMEM)] + [in_spec(o) for o in offsets],
        out_specs=[out_spec, out_spec],
        out_shape=[out_sds, out_sds],
        compiler_params=_params(("parallel", "parallel")),
        name=f"dil_attn_{dil}",
    )(slopes_g, qkv, qkv, qkv)


def _dil_mix_kernel(o0_ref, o1_ref, o2_ref, l0_ref, l1_ref, l2_ref, ob_ref):
    l0, l1, l2 = l0_ref[...], l1_ref[...], l2_ref[...]
    m = jnp.maximum(jnp.maximum(l0, l1), l2)
    w0, w1, w2 = jnp.exp(l0 - m), jnp.exp(l1 - m), jnp.exp(l2 - m)
    ob = (w0 * o0_ref[...] + w1 * o1_ref[...] + w2 * o2_ref[...]) / (w0 + w1 + w2)
    ob_ref[...] = ob.astype(ob_ref.dtype)


def _dil_mix(dil_outs, dil_lses, *, tm=512):
    T, W = dil_outs[0].shape
    row = pl.BlockSpec((tm, W), lambda i: (i, 0))
    return pl.pallas_call(
        _dil_mix_kernel,
        grid=(T // tm,),
        in_specs=[row] * 6,
        out_specs=row,
        out_shape=jax.ShapeDtypeStruct((T, W), BF16),
        compiler_params=_params(("parallel",)),
        name="dil_mix",
    )(*dil_outs, *dil_lses)


def _merge_kernel(oa_ref, ob_ref, ga_ref, gb_ref, wa_ref, wb_ref, y_ref):
    ya = jnp.dot(oa_ref[...], wa_ref[...].astype(BF16), preferred_element_type=F32)
    yb = jnp.dot(ob_ref[...], wb_ref[...].astype(BF16), preferred_element_type=F32)
    y = (jax.nn.sigmoid(ga_ref[...].astype(F32)) * ya
         + jax.nn.sigmoid(gb_ref[...].astype(F32)) * yb)
    y_ref[...] = y.astype(y_ref.dtype)


def _merge(oa, ob, z, wa, wb, *, tm=1024, tn=512):
    T = oa.shape[0]
    return pl.pallas_call(
        _merge_kernel,
        grid=(T // tm, D_MODEL // tn),
        in_specs=[pl.BlockSpec((tm, DIFF_V_W), lambda i, j: (i, 0)),
                  pl.BlockSpec((tm, DIL_OUT_W), lambda i, j: (i, 0)),
                  pl.BlockSpec((tm, tn), lambda i, j: (i, Z_GA // tn + j)),
                  pl.BlockSpec((tm, tn), lambda i, j: (i, Z_GB // tn + j)),
                  pl.BlockSpec((DIFF_V_W, tn), lambda i, j: (0, j)),
                  pl.BlockSpec((DIL_OUT_W, tn), lambda i, j: (0, j))],
        out_specs=pl.BlockSpec((tm, tn), lambda i, j: (i, j)),
        out_shape=jax.ShapeDtypeStruct((T, D_MODEL), BF16),
        compiler_params=_params(("parallel", "parallel")),
        name="merge",
    )(oa, ob, z, z, wa, wb)


def _ple_kernel(x_ref, p_ref, wg_ref, wp_ref, o_ref):
    gate = jnp.dot(x_ref[...], wg_ref[...].astype(BF16), preferred_element_type=F32)
    proj = jnp.dot(p_ref[...].astype(BF16), wp_ref[...].astype(BF16),
                   preferred_element_type=F32)
    o_ref[...] = jax.nn.sigmoid(gate) * proj


def _ple(x_bf, p, wg, wp, *, tm=1024, tn=512):
    T, D = x_bf.shape
    return pl.pallas_call(
        _ple_kernel,
        grid=(T // tm, D // tn),
        in_specs=[
            pl.BlockSpec((tm, D), lambda i, j: (i, 0)),
            pl.BlockSpec((tm, PLE_DIM), lambda i, j: (i, 0)),
            pl.BlockSpec((D, tn), lambda i, j: (0, j)),
            pl.BlockSpec((PLE_DIM, tn), lambda i, j: (0, j)),
        ],
        out_specs=pl.BlockSpec((tm, tn), lambda i, j: (i, j)),
        out_shape=jax.ShapeDtypeStruct((T, D), F32),
        compiler_params=_params(("parallel", "parallel")),
        name="ple",
    )(x_bf, p, wg, wp)


def _alibi_slopes(n):
    return jnp.exp2(-8.0 * jnp.arange(1, n + 1, dtype=F32) / n)


def _bf16_pieces(v):
    pieces, rest = [], v
    for _ in range(COEF_PIECES):
        piece = rest.astype(BF16).astype(F32)
        pieces.append(piece)
        rest = rest - piece
    return jnp.stack(pieces, axis=-1)


def kernel(x, p, ffn1_w_in, ffn1_w_out, ln1_g, ln1_b, w_in, lam_q1, lam_k1, lam_q2, lam_k2,
           subln_g, w_branch_diff, w_branch_dil, w_mix_out, ln2_g, ln2_b, ffn2_w_in,
           ffn2_w_out, ln3_g, ln3_b, w_ple_gate, w_ple_proj, ln4_g, ln4_b):
    batch, seq, d = x.shape
    depth = ffn1_w_in.shape[0]
    T = batch * seq
    alpha = (2 * depth) ** 0.25
    n_pat = len(DIL_PATTERNS)
    tn = 512

    h = x.reshape(T, d)
    h_bf = h.astype(BF16)
    diff_coefs = _bf16_pieces(_alibi_slopes(DIFF_HEADS) * LOG2E)
    dil_slopes = _alibi_slopes(n_pat * DIL_GROUP_HEADS).reshape(DIL_GROUP_HEADS, n_pat)
    colscale = jnp.ones((Z_W,), F32).at[Z_DQ:Z_DK].set(DIFF_HEAD_DIM ** -0.5 * LOG2E)

    for i in range(depth):
        lambda_init = 0.8 - 0.6 * math.exp(-0.3 * i)

        f = _ffn(h_bf, ffn1_w_in[i].astype(BF16), ffn1_w_out[i].astype(BF16))
        dils = tuple(dil for _window, dil in DIL_PATTERNS if dil > 1)
        h, h_bf, h_cm = _res_ln_class_major(h, f, ln1_g[i], ln1_b[i], alpha=alpha, scale=0.5,
                                            batch=batch, seq=seq, dils=dils)
        h_class_major = dict(zip(dils, h_cm))

        z, ffn2_w_in_bf, ffn2_w_out_bf = _matmul(
            h_bf, w_in[i], BF16, n_out=Z_W, wcol=_main_wcol(tn), colscale=colscale, tn=tn,
            riders=((ffn2_w_in[i], (d, V7X_LANES), 1), (ffn2_w_out[i], (RIDER_ROWS, d), 0)),
            name="in_proj")
        lam_rows = jnp.stack([lam_q1[i], lam_k1[i], lam_q2[i], lam_k2[i]]).astype(F32)
        oa = _diff_attention(z, diff_coefs, lam_rows, subln_g[i].astype(F32),
                             batch=batch, seq=seq, lambda_init=lambda_init)

        dil_outs, dil_lses = [], []
        for g, (_window, dil) in enumerate(DIL_PATTERNS):
            slopes_g = dil_slopes[:, g] * dil
            if dil == 1:
                o_g, lse_g = _dilated_attention(z, (Z_LQ, Z_LK, Z_LV), slopes_g, dil=dil,
                                                batch=batch, seq=seq)
            else:
                qkv = _matmul(h_class_major[dil], w_in[i], BF16,
                              n_out=3 * DIL_OUT_W, wcol=_group_wcol(g, tn), tn=tn,
                              name=f"in_proj_dil{dil}")
                o_g, lse_g = _dilated_attention(qkv, (0, DIL_OUT_W, 2 * DIL_OUT_W), slopes_g,
                                                dil=dil, batch=batch, seq=seq)
            dil_outs.append(o_g)
            dil_lses.append(lse_g)

        y = _merge(oa, _dil_mix(dil_outs, dil_lses), z, w_branch_diff[i], w_branch_dil[i])
        mix = _matmul(y, w_mix_out[i], F32, n_out=d, wcol=lambda j: j, tn=tn, name="mix_out")
        h, h_bf = _res_ln(h, mix, ln2_g[i], ln2_b[i], alpha=alpha, scale=1.0)

        f = _ffn(h_bf, ffn2_w_in_bf, ffn2_w_out_bf)
        h, h_bf = _res_ln(h, f, ln3_g[i], ln3_b[i], alpha=alpha, scale=0.5)

        ple = _ple(h_bf, p[i].reshape(T, PLE_DIM), w_ple_gate[i], w_ple_proj[i])
        if i + 1 < depth:
            h, h_bf = _res_ln(h, ple, ln4_g[i], ln4_b[i], alpha=alpha, scale=1.0)
        else:
            h = _res_ln(h, ple, ln4_g[i], ln4_b[i], alpha=alpha, scale=1.0, with_bf16=False)

    return h.reshape(batch, seq, d)
```

```python
import functools
import math

import jax
import jax.numpy as jnp
from jax import lax
from jax.experimental import pallas as pl
from jax.experimental.pallas import tpu as pltpu

F32 = jnp.float32
BF16 = jnp.bfloat16

D_MODEL = 4096
PLE_DIM = 256
D_FF = 11008
DIFF_HEADS = 8
DIFF_HEAD_DIM = 128
DIL_PATTERNS = ((128, 1), (512, 4), (2048, 16))
DIL_GROUP_HEADS = 8
DIL_HEAD_DIM = 128
DIL_HALF_KEYS = 64
LN_EPS = 1e-5
SUBLN_EPS = 1e-5
NEG_INF = -1e30

DIFF_QK_W = DIFF_HEADS * 2 * DIFF_HEAD_DIM
DIFF_V_W = DIFF_HEADS * 2 * DIFF_HEAD_DIM
DIL_W = len(DIL_PATTERNS) * DIL_GROUP_HEADS * DIL_HEAD_DIM
DIL_OUT_W = DIL_GROUP_HEADS * DIL_HEAD_DIM
IN_PROJ_W = 2 * DIFF_QK_W + DIFF_V_W + 3 * DIL_W + 2 * D_MODEL

OFF_DQ = 0
OFF_DK = OFF_DQ + DIFF_QK_W
OFF_DV = OFF_DK + DIFF_QK_W
OFF_LQ = OFF_DV + DIFF_V_W
OFF_LK = OFF_LQ + DIL_W
OFF_LV = OFF_LK + DIL_W
OFF_GA = OFF_LV + DIL_W
OFF_GB = OFF_GA + D_MODEL

Z_DQ = 0
Z_DK = Z_DQ + DIFF_QK_W
Z_DV = Z_DK + DIFF_QK_W
Z_LQ = Z_DV + DIFF_V_W
Z_LK = Z_LQ + DIL_OUT_W
Z_LV = Z_LK + DIL_OUT_W
Z_GA = Z_LV + DIL_OUT_W
Z_GB = Z_GA + D_MODEL
Z_W = Z_GB + D_MODEL

V7X_LANES = 128
V7X_VMEM_BYTES = 64 * 1024 * 1024
V7X_VMEM_REQUEST = 56 * 1024 * 1024

FFN_BLOCK = 256

LOG2E = math.log2(math.e)
COEF_PIECES = 3
POS_SPLIT = 64
RIDER_ROWS = 64


def _params(semantics, vmem_bytes=V7X_VMEM_REQUEST):
    return pltpu.CompilerParams(dimension_semantics=semantics, vmem_limit_bytes=vmem_bytes)


def _ffn_kernel(x_ref, wga_ref, wgb_ref, wua_ref, wub_ref, woa_ref, wob_ref, o_ref, *, n_blocks):
    j = pl.program_id(1)

    @pl.when(j == 0)
    def _():
        o_ref[...] = jnp.zeros(o_ref.shape, F32)

    x = x_ref[...]

    def hidden(wg_ref, wu_ref):
        g = jnp.dot(x, wg_ref[...], preferred_element_type=F32)
        u = jnp.dot(x, wu_ref[...], preferred_element_type=F32)
        return g * jax.nn.sigmoid(g) * u

    ha = hidden(wga_ref, wua_ref).astype(BF16)
    hb = jnp.where(2 * j + 1 < n_blocks, hidden(wgb_ref, wub_ref), 0.0).astype(BF16)
    part = jnp.dot(ha, woa_ref[...], preferred_element_type=F32)
    part += jnp.dot(hb, wob_ref[...], preferred_element_type=F32)
    o_ref[...] += part


def _ffn(x_bf, w_in_bf, w_out_bf, *, tm=512, tb=FFN_BLOCK):
    T, D = x_bf.shape
    n_blocks = w_out_bf.shape[0] // tb
    steps = -(-n_blocks // 2)

    def first(j):
        return 2 * j

    def second(j):
        return jnp.minimum(2 * j + 1, n_blocks - 1)

    def col_spec(blk, off):
        return pl.BlockSpec((D, tb), lambda i, j: (0, off + blk(j)))

    def row_spec(blk):
        return pl.BlockSpec((tb, D), lambda i, j: (blk(j), 0))

    return pl.pallas_call(
        functools.partial(_ffn_kernel, n_blocks=n_blocks),
        grid=(T // tm, steps),
        in_specs=[
            pl.BlockSpec((tm, D), lambda i, j: (i, 0)),
            col_spec(first, 0), col_spec(second, 0),
            col_spec(first, n_blocks), col_spec(second, n_blocks),
            row_spec(first), row_spec(second),
        ],
        out_specs=pl.BlockSpec((tm, D), lambda i, j: (i, 0)),
        out_shape=jax.ShapeDtypeStruct((T, D), F32),
        compiler_params=_params(("parallel", "arbitrary")),
        name="ffn",
    )(x_bf, w_in_bf, w_in_bf, w_in_bf, w_in_bf, w_out_bf, w_out_bf)


def _layer_norm_rows(y, g_ref, b_ref):
    mu = jnp.mean(y, axis=-1, keepdims=True)
    yc = y - mu
    var = jnp.mean(yc * yc, axis=-1, keepdims=True)
    return yc * lax.rsqrt(var + LN_EPS) * g_ref[...] + b_ref[...]


def _ln_kernel(h_ref, br_ref, g_ref, b_ref, o_ref, obf_ref, *, alpha, scale):
    out = _layer_norm_rows(alpha * h_ref[...] + scale * br_ref[...], g_ref, b_ref)
    o_ref[...] = out
    obf_ref[...] = out.astype(BF16)


def _ln_class_major_kernel(h_ref, br_ref, g_ref, b_ref, o_ref, obf_ref, *rest, alpha, scale, dils):
    cm_refs, slab_sc = rest[:-1], rest[-1]
    out = _layer_norm_rows(alpha * h_ref[...] + scale * br_ref[...], g_ref, b_ref)
    o_ref[...] = out
    obf_ref[...] = out.astype(BF16)
    tm, d = out.shape
    for s in range(d // V7X_LANES):
        slab_sc[s] = out[:, s * V7X_LANES:(s + 1) * V7X_LANES]
    for dil, cm_ref in zip(dils, cm_refs):
        for r in range(dil):
            for s in range(d // V7X_LANES):
                cm_ref[r, :, s * V7X_LANES:(s + 1) * V7X_LANES] = slab_sc[
                    s, pl.ds(r, tm // dil, stride=dil), :].astype(BF16)


def _res_ln_class_major(h, branch, g, b, *, alpha, scale, batch, seq, dils, tm=256):
    T, D = h.shape
    tiles = seq // tm
    row = pl.BlockSpec((tm, D), lambda i: (i, 0))
    vec = pl.BlockSpec((1, D), lambda i: (0, 0))
    cm_specs = [pl.BlockSpec((None, dil, tm // dil, D), lambda i: (i // tiles, 0, i % tiles, 0))
                for dil in dils]
    cm_shapes = [jax.ShapeDtypeStruct((batch, dil, seq // dil, D), BF16) for dil in dils]
    outs = pl.pallas_call(
        functools.partial(_ln_class_major_kernel, alpha=alpha, scale=scale, dils=dils),
        grid=(T // tm,),
        in_specs=[row, row, vec, vec],
        out_specs=[row, row] + cm_specs,
        out_shape=[jax.ShapeDtypeStruct((T, D), F32), jax.ShapeDtypeStruct((T, D), BF16)]
        + cm_shapes,
        scratch_shapes=[pltpu.VMEM((D // V7X_LANES, tm, V7X_LANES), F32)],
        compiler_params=_params(("parallel",)),
        name="res_ln_class_major",
    )(h, branch, g.reshape(1, D), b.reshape(1, D))
    return outs[0], outs[1], [cm.reshape(T, D) for cm in outs[2:]]


def _ln_presummed_kernel(y_ref, g_ref, b_ref, *o_refs):
    out = _layer_norm_rows(y_ref[...], g_ref, b_ref)
    for o_ref in o_refs:
        o_ref[...] = out.astype(o_ref.dtype)


def _ln_presummed(y, g, b, *, with_bf16=True, tm=256):
    T, D = y.shape
    row = pl.BlockSpec((tm, D), lambda i: (i, 0))
    vec = pl.BlockSpec((1, D), lambda i: (0, 0))
    f32_out = jax.ShapeDtypeStruct((T, D), F32)
    return pl.pallas_call(
        _ln_presummed_kernel,
        grid=(T // tm,),
        in_specs=[row, vec, vec],
        out_specs=[row, row] if with_bf16 else row,
        out_shape=[f32_out, jax.ShapeDtypeStruct((T, D), BF16)] if with_bf16 else f32_out,
        compiler_params=_params(("parallel",)),
        name="ln",
    )(y, g.reshape(1, D), b.reshape(1, D))


def _res_ln(h, branch, g, b, *, alpha, scale, tm=256):
    T, D = h.shape
    row = pl.BlockSpec((tm, D), lambda i: (i, 0))
    vec = pl.BlockSpec((1, D), lambda i: (0, 0))
    return pl.pallas_call(
        functools.partial(_ln_kernel, alpha=alpha, scale=scale),
        grid=(T // tm,),
        in_specs=[row, row, vec, vec],
        out_specs=[row, row],
        out_shape=[jax.ShapeDtypeStruct((T, D), F32), jax.ShapeDtypeStruct((T, D), BF16)],
        compiler_params=_params(("parallel",)),
        name="res_ln",
    )(h, branch, g.reshape(1, D), b.reshape(1, D))


def _matmul_kernel(*refs, scaled, residual_alpha, rider_blocks, n_col_tiles):
    refs = list(refs)
    n_riders = len(rider_blocks)
    x_ref, w_ref = refs[:2]
    del refs[:2]
    s_ref = refs.pop(0) if scaled else None
    h_ref = refs.pop(0) if residual_alpha is not None else None
    rider_in, o_ref, rider_out = refs[:n_riders], refs[n_riders], refs[n_riders + 1:]
    acc = jnp.dot(x_ref[...], w_ref[...].astype(BF16), preferred_element_type=F32)
    if scaled:
        acc = acc * s_ref[...]
    if residual_alpha is not None:
        acc = residual_alpha * h_ref[...] + acc
    o_ref[...] = acc.astype(o_ref.dtype)
    step = pl.program_id(0) * n_col_tiles + pl.program_id(1)
    for src, dst, n_blocks in zip(rider_in, rider_out, rider_blocks):
        @pl.when(step < n_blocks)
        def _(src=src, dst=dst):
            dst[...] = src[...].astype(BF16)


def _matmul(x_bf, w, out_dtype, *, n_out, wcol, colscale=None, residual=None, riders=(),
            tm=1024, tn=512, name="matmul"):
    T, K = x_bf.shape
    n_col_tiles = n_out // tn
    tile = pl.BlockSpec((tm, tn), lambda i, j: (i, j))
    in_specs = [
        pl.BlockSpec((tm, K), lambda i, j: (i, 0)),
        pl.BlockSpec((K, tn), lambda i, j: (0, wcol(j))),
    ]
    args = [x_bf, w]
    if colscale is not None:
        in_specs.append(pl.BlockSpec((1, tn), lambda i, j: (0, j)))
        args.append(colscale.reshape(1, n_out))
    if residual is not None:
        in_specs.append(tile)
        args.append(residual[0])
    rider_specs, rider_blocks = [], []
    for arr, block, axis in riders:
        n_blocks = arr.shape[axis] // block[axis]
        assert n_blocks <= (T // tm) * n_col_tiles

        def index_map(i, j, axis=axis, n_blocks=n_blocks):
            blk = jnp.minimum(i * n_col_tiles + j, n_blocks - 1)
            return tuple(blk if a == axis else 0 for a in range(2))

        rider_specs.append(pl.BlockSpec(block, index_map))
        rider_blocks.append(n_blocks)
        args.append(arr)
    outs = pl.pallas_call(
        functools.partial(_matmul_kernel, scaled=colscale is not None,
                          residual_alpha=None if residual is None else residual[1],
                          rider_blocks=tuple(rider_blocks), n_col_tiles=n_col_tiles),
        grid=(T // tm, n_col_tiles),
        in_specs=in_specs + rider_specs,
        out_specs=[tile] + rider_specs,
        out_shape=[jax.ShapeDtypeStruct((T, n_out), out_dtype)]
        + [jax.ShapeDtypeStruct(arr.shape, BF16) for arr, _, _ in riders],
        compiler_params=_params(("arbitrary", "arbitrary")),
        name=name,
    )(*args)
    return outs if riders else outs[0]


def _main_wcol(tn):
    per_sec = DIL_OUT_W // tn
    first = OFF_LQ // tn
    n_pat = len(DIL_PATTERNS)

    def wcol(j):
        k = j - first
        dil_tile = first + (k // per_sec) * (n_pat * per_sec) + k % per_sec
        gate_tile = j + (n_pat - 1) * 3 * per_sec
        return jnp.where(j < first, j, jnp.where(k < 3 * per_sec, dil_tile, gate_tile))

    return wcol


def _group_wcol(group, tn):
    per_sec = DIL_OUT_W // tn

    def wcol(j):
        return (OFF_LQ + group * DIL_OUT_W) // tn + (j // per_sec) * (DIL_W // tn) + j % per_sec

    return wcol


def _alibi_columns(pos, coefs, q_side):
    lane = lax.broadcasted_iota(jnp.int32, pos.shape, 1)
    piece = jnp.where(lane < 3, lane, jnp.where(lane < 6, lane - 3,
                                                jnp.where(lane < 9, lane - 6, lane - 9)))
    cv = jnp.where(piece == 0, coefs[0], jnp.where(piece == 1, coefs[1], coefs[2]))
    hi = (pos - (pos & (POS_SPLIT - 1))).astype(F32)
    lo = (pos & (POS_SPLIT - 1)).astype(F32)
    n = 2 * COEF_PIECES
    if q_side:
        cols = jnp.where(lane < COEF_PIECES, -hi, jnp.where(lane < n, -lo, cv))
    else:
        cols = jnp.where(lane < n, cv, jnp.where(lane < n + COEF_PIECES, hi, lo))
    return jnp.where(lane < 2 * n, cols, 0.0)


def _diff_attn_kernel(coef_ref, lam_ref, g_ref, q_ref, k_ref, v_ref, o_ref,
                      kaug_sc, qaug_sc, corr_sc, sa_sc, sb_sc, pa_sc, pb_sc, m_sc, l_sc,
                      alphaa_sc, alphab_sc, acc_sc, *, tq, rb, seq, lambda_init):
    h = pl.program_id(1)
    qi = pl.program_id(2)
    dh = DIFF_HEAD_DIM
    n_chunks = seq // tq
    coefs = [coef_ref[h, t] for t in range(COEF_PIECES)]
    row_iota = lax.broadcasted_iota(jnp.int32, (tq, dh), 0)

    @pl.when(qi == 0)
    def _():
        def build(t, carry):
            r0 = pl.multiple_of(t * tq, tq)
            ak = _alibi_columns(row_iota + r0, coefs, q_side=False).astype(BF16)
            for c in range(2):
                kaug_sc[c, pl.ds(r0, tq), 0:dh] = k_ref[pl.ds(r0, tq), c * dh:(c + 1) * dh]
                kaug_sc[c, pl.ds(r0, tq), dh:2 * dh] = ak
            return carry

        lax.fori_loop(0, n_chunks, build, 0)
        row = lax.broadcasted_iota(jnp.int32, (tq, tq), 0)
        col = lax.broadcasted_iota(jnp.int32, (tq, tq), 1)
        slope2 = coefs[0] + coefs[1] + coefs[2]
        corr_sc[0] = jnp.zeros((tq, tq), F32)
        corr_sc[1] = (2.0 * slope2) * jnp.minimum(row - col, 0).astype(F32)

    aq = _alibi_columns(row_iota + qi * tq, coefs, q_side=True)
    for c in range(2):
        q_c = q_ref[:, c * dh:(c + 1) * dh]
        qaug_sc[0, c, :, 0:dh] = q_c
        qaug_sc[0, c, :, dh:2 * dh] = aq.astype(BF16)
        qaug_sc[1, c, :, 0:dh] = q_c
        qaug_sc[1, c, :, dh:2 * dh] = (-aq).astype(BF16)

    m_sc[...] = jnp.full(m_sc.shape, -jnp.inf, F32)
    l_sc[...] = jnp.zeros(l_sc.shape, F32)
    acc_sc[...] = jnp.zeros(acc_sc.shape, F32)

    n_slabs = tq // V7X_LANES
    last = n_chunks - 1

    def scores(kc, s_ref):
        ks = pl.multiple_of(kc * tq, tq)
        side = jnp.where(kc > qi, 1, 0)
        for c in range(2):
            s_ref[c] = lax.dot_general(
                qaug_sc[side, c], kaug_sc[c, pl.ds(ks, tq), :],
                (((1,), (1,)), ((), ())), preferred_element_type=F32)

    def softmax(kc, s_ref, p_ref, alpha_ref, near_diag):
        on_diag = jnp.where(kc == qi, 1, 0)
        for c in range(2):
            for b in range(tq // rb):
                rows = slice(b * rb, (b + 1) * rb)
                sb = s_ref[c, rows, :]
                if near_diag:
                    sb = sb + corr_sc[on_diag, rows, :]
                slabs = [sb[:, t * V7X_LANES:(t + 1) * V7X_LANES] for t in range(n_slabs)]
                mx = functools.reduce(jnp.maximum, slabs)
                m_old = m_sc[c, rows, :]
                m_new = jnp.maximum(m_old, jnp.broadcast_to(
                    jnp.max(mx, axis=-1, keepdims=True), (rb, V7X_LANES)))
                a = jnp.exp2(m_old - m_new)
                ps = [jnp.exp2(sl - m_new) for sl in slabs]
                l_sc[c, rows, :] = a * l_sc[c, rows, :] + functools.reduce(jnp.add, ps)
                p_ref[c, rows, :] = jnp.concatenate(ps, axis=-1).astype(BF16)
                alpha_ref[c, rows, :] = a
                m_sc[c, rows, :] = m_new

    def accumulate(kc, p_ref, alpha_ref):
        ks = pl.multiple_of(kc * tq, tq)
        for c in range(2):
            alpha = alpha_ref[c]
            acc_sc[c] = jnp.concatenate([alpha, alpha], axis=-1) * acc_sc[c] + jnp.dot(
                p_ref[c], v_ref[pl.ds(ks, tq), :], preferred_element_type=F32)

    def pair_body(kc, near_diag, first=False, final=False):
        if not first:
            accumulate(kc - 1, pb_sc, alphab_sc)
        scores(kc + 1, sb_sc)
        softmax(kc, sa_sc, pa_sc, alphaa_sc, near_diag)
        accumulate(kc, pa_sc, alphaa_sc)
        if not final:
            scores(kc + 2, sa_sc)
        softmax(kc + 1, sb_sc, pb_sc, alphab_sc, near_diag)

    def pair(t, first=False, final=False):
        lax.cond(t == qi // 2,
                 functools.partial(pair_body, 2 * t, True, first, final),
                 functools.partial(pair_body, 2 * t, False, first, final))

    def middle_pair(t, carry):
        pair(t)
        return carry

    n_pairs = n_chunks // 2
    scores(0, sa_sc)
    pair(0, first=True)
    lax.fori_loop(1, n_pairs - 1, middle_pair, 0)
    pair(n_pairs - 1, final=True)
    accumulate(last, pb_sc, alphab_sc)

    lam_rows = lam_ref[...]
    lam = (jnp.exp(jnp.sum(lam_rows[0:1] * lam_rows[1:2], axis=-1, keepdims=True))
           - jnp.exp(jnp.sum(lam_rows[2:3] * lam_rows[3:4], axis=-1, keepdims=True))
           + lambda_init)
    l0 = jnp.sum(l_sc[0], axis=-1, keepdims=True)
    l1 = jnp.sum(l_sc[1], axis=-1, keepdims=True)
    o = acc_sc[0] / l0 - lam * (acc_sc[1] / l1)
    ms = jnp.mean(o * o, axis=-1, keepdims=True)
    o = o * lax.rsqrt(ms + SUBLN_EPS) * g_ref[...] * (1.0 - lambda_init)
    o_ref[...] = o.astype(o_ref.dtype)


def _diff_attention(z, coefs, lam_rows, subln_g, *, batch, seq, lambda_init, tq=512, rb=32):
    T = z.shape[0]
    e = 2 * DIFF_HEAD_DIM
    nq = seq // tq
    qspec = pl.BlockSpec((tq, e), lambda b, h, i: (b * nq + i, Z_DQ // e + h))
    kspec = pl.BlockSpec((seq, e), lambda b, h, i: (b, Z_DK // e + h))
    vspec = pl.BlockSpec((seq, e), lambda b, h, i: (b, Z_DV // e + h))
    return pl.pallas_call(
        functools.partial(_diff_attn_kernel, tq=tq, rb=rb, seq=seq, lambda_init=lambda_init),
        grid=(batch, DIFF_HEADS, nq),
        in_specs=[
            pl.BlockSpec(memory_space=pltpu.SMEM),
            pl.BlockSpec((4, DIFF_HEAD_DIM), lambda b, h, i: (0, 0)),
            pl.BlockSpec((1, e), lambda b, h, i: (0, 0)),
            qspec, kspec, vspec,
        ],
        out_specs=pl.BlockSpec((tq, e), lambda b, h, i: (b * nq + i, h)),
        out_shape=jax.ShapeDtypeStruct((T, DIFF_V_W), BF16),
        scratch_shapes=[
            pltpu.VMEM((2, seq, e), BF16),
            pltpu.VMEM((2, 2, tq, e), BF16),
            pltpu.VMEM((2, tq, tq), F32),
            pltpu.VMEM((2, tq, tq), F32),
            pltpu.VMEM((2, tq, tq), F32),
            pltpu.VMEM((2, tq, tq), BF16),
            pltpu.VMEM((2, tq, tq), BF16),
            pltpu.VMEM((2, tq, V7X_LANES), F32),
            pltpu.VMEM((2, tq, V7X_LANES), F32),
            pltpu.VMEM((2, tq, V7X_LANES), F32),
            pltpu.VMEM((2, tq, V7X_LANES), F32),
            pltpu.VMEM((2, tq, e), F32),
        ],
        compiler_params=_params(("parallel", "parallel", "arbitrary")),
        name="diff_attn",
    )(coefs, lam_rows, subln_g.reshape(1, e), z, z, z)


def _dil_attn_kernel(slopes_ref, q_ref, k_ref, v_ref, o_ref, lse_ref, *, seq, dil, tq, group):
    h = pl.program_id(1)
    slope = slopes_ref[h]
    w = DIL_HALF_KEYS
    nk = tq + 2 * w
    length = seq // dil
    blocks_per_class = length // tq
    scale = DIL_HEAD_DIM ** -0.5
    row = lax.broadcasted_iota(jnp.int32, (tq, nk), 0)
    col = lax.broadcasted_iota(jnp.int32, (tq, nk), 1)
    col_minus_row = col - row

    def block_offsets(i):
        cls = i // blocks_per_class
        qs_local = (i % blocks_per_class) * tq
        ks_local = jnp.clip(qs_local - w, 0, length - nk)
        return cls, qs_local, ks_local

    def body(it, carry):
        blocks = [block_offsets(it * group + g) for g in range(group)]
        scores = []
        for cls, qs_local, ks_local in blocks:
            q = q_ref[pl.ds(pl.multiple_of(cls * length + qs_local, tq), tq), :]
            k = k_ref[pl.ds(pl.multiple_of(cls * length + ks_local, w), nk), :]
            scores.append(lax.dot_general(q, k, (((1,), (1,)), ((), ())),
                                          preferred_element_type=F32))
        for (cls, qs_local, ks_local), s in zip(blocks, scores):
            rel = jnp.abs(col_minus_row + (ks_local - qs_local))
            s = jnp.where(rel <= w, s * scale - slope * rel.astype(F32), NEG_INF)
            m = jnp.max(s, axis=-1, keepdims=True)
            e = jnp.exp(s - m)
            z = jnp.sum(e, axis=-1, keepdims=True)
            v = v_ref[pl.ds(pl.multiple_of(cls * length + ks_local, w), nk), :]
            o = jnp.dot(e.astype(BF16), v, preferred_element_type=F32) / z
            if dil == 1:
                rows = pl.ds(pl.multiple_of(qs_local, tq), tq)
            else:
                rows = pl.ds(qs_local * dil + cls, tq, stride=dil)
            o_ref[rows, :] = o
            lse_ref[rows, :] = jnp.broadcast_to(m + jnp.log(z), (tq, DIL_HEAD_DIM))
        return carry

    lax.fori_loop(0, seq // (tq * group), body, 0)


def _dilated_attention(qkv, offsets, slopes_g, *, dil, batch, seq, tq=128, group=16):
    T = qkv.shape[0]
    hd = DIL_HEAD_DIM

    def in_spec(off):
        return pl.BlockSpec((seq, hd), lambda b, h: (b, off // hd + h))

    out_spec = pl.BlockSpec((seq, hd), lambda b, h: (b, h))
    out_sds = jax.ShapeDtypeStruct((T, DIL_OUT_W), F32)
    return pl.pallas_call(
        functools.partial(_dil_attn_kernel, seq=seq, dil=dil, tq=tq, group=group),
        grid=(batch, DIL_GROUP_HEADS),
        in_specs=[pl.BlockSpec(memory_space=pltpu.SMEM)] + [in_spec(o) for o in offsets],
        out_specs=[out_spec, out_spec],
        out_shape=[out_sds, out_sds],
        compiler_params=_params(("parallel", "parallel")),
        name=f"dil_attn_{dil}",
    )(slopes_g, qkv, qkv, qkv)


def _dil_mix_kernel(o0_ref, o1_ref, o2_ref, l0_ref, l1_ref, l2_ref, ob_ref):
    l0, l1, l2 = l0_ref[...], l1_ref[...], l2_ref[...]
    m = jnp.maximum(jnp.maximum(l0, l1), l2)
    w0, w1, w2 = jnp.exp(l0 - m), jnp.exp(l1 - m), jnp.exp(l2 - m)
    ob = (w0 * o0_ref[...] + w1 * o1_ref[...] + w2 * o2_ref[...]) / (w0 + w1 + w2)
    ob_ref[...] = ob.astype(ob_ref.dtype)


def _dil_mix(dil_outs, dil_lses, *, tm=512):
    T, W = dil_outs[0].shape
    row = pl.BlockSpec((tm, W), lambda i: (i, 0))
    return pl.pallas_call(
        _dil_mix_kernel,
        grid=(T // tm,),
        in_specs=[row] * 6,
        out_specs=row,
        out_shape=jax.ShapeDtypeStruct((T, W), BF16),
        compiler_params=_params(("parallel",)),
        name="dil_mix",
    )(*dil_outs, *dil_lses)


def _merge_kernel(oa_ref, ob_ref, ga_ref, gb_ref, wa_ref, wb_ref, y_ref):
    ya = jnp.dot(oa_ref[...], wa_ref[...].astype(BF16), preferred_element_type=F32)
    yb = jnp.dot(ob_ref[...], wb_ref[...].astype(BF16), preferred_element_type=F32)
    y = (jax.nn.sigmoid(ga_ref[...].astype(F32)) * ya
         + jax.nn.sigmoid(gb_ref[...].astype(F32)) * yb)
    y_ref[...] = y.astype(y_ref.dtype)


def _merge(oa, ob, z, wa, wb, *, tm=1024, tn=512):
    T = oa.shape[0]
    return pl.pallas_call(
        _merge_kernel,
        grid=(T // tm, D_MODEL // tn),
        in_specs=[pl.BlockSpec((tm, DIFF_V_W), lambda i, j: (i, 0)),
                  pl.BlockSpec((tm, DIL_OUT_W), lambda i, j: (i, 0)),
                  pl.BlockSpec((tm, tn), lambda i, j: (i, Z_GA // tn + j)),
                  pl.BlockSpec((tm, tn), lambda i, j: (i, Z_GB // tn + j)),
                  pl.BlockSpec((DIFF_V_W, tn), lambda i, j: (0, j)),
                  pl.BlockSpec((DIL_OUT_W, tn), lambda i, j: (0, j))],
        out_specs=pl.BlockSpec((tm, tn), lambda i, j: (i, j)),
        out_shape=jax.ShapeDtypeStruct((T, D_MODEL), BF16),
        compiler_params=_params(("parallel", "parallel")),
        name="merge",
    )(oa, ob, z, z, wa, wb)


def _ple_kernel(x_ref, p_ref, wg_ref, wp_ref, h_ref, o_ref, *, alpha):
    gate = jnp.dot(x_ref[...], wg_ref[...].astype(BF16), preferred_element_type=F32)
    proj = jnp.dot(p_ref[...].astype(BF16), wp_ref[...].astype(BF16),
                   preferred_element_type=F32)
    o_ref[...] = alpha * h_ref[...] + jax.nn.sigmoid(gate) * proj


def _ple(x_bf, p, wg, wp, h, *, alpha, tm=1024, tn=512):
    T, D = x_bf.shape
    tile = pl.BlockSpec((tm, tn), lambda i, j: (i, j))
    return pl.pallas_call(
        functools.partial(_ple_kernel, alpha=alpha),
        grid=(T // tm, D // tn),
        in_specs=[
            pl.BlockSpec((tm, D), lambda i, j: (i, 0)),
            pl.BlockSpec((tm, PLE_DIM), lambda i, j: (i, 0)),
            pl.BlockSpec((D, tn), lambda i, j: (0, j)),
            pl.BlockSpec((PLE_DIM, tn), lambda i, j: (0, j)),
            tile,
        ],
        out_specs=tile,
        out_shape=jax.ShapeDtypeStruct((T, D), F32),
        compiler_params=_params(("parallel", "parallel")),
        name="ple",
    )(x_bf, p, wg, wp, h)


def _alibi_slopes(n):
    return jnp.exp2(-8.0 * jnp.arange(1, n + 1, dtype=F32) / n)


def _bf16_pieces(v):
    pieces, rest = [], v
    for _ in range(COEF_PIECES):
        piece = rest.astype(BF16).astype(F32)
        pieces.append(piece)
        rest = rest - piece
    return jnp.stack(pieces, axis=-1)


def kernel(x, p, ffn1_w_in, ffn1_w_out, ln1_g, ln1_b, w_in, lam_q1, lam_k1, lam_q2, lam_k2,
           subln_g, w_branch_diff, w_branch_dil, w_mix_out, ln2_g, ln2_b, ffn2_w_in,
           ffn2_w_out, ln3_g, ln3_b, w_ple_gate, w_ple_proj, ln4_g, ln4_b):
    batch, seq, d = x.shape
    depth = ffn1_w_in.shape[0]
    T = batch * seq
    alpha = (2 * depth) ** 0.25
    n_pat = len(DIL_PATTERNS)
    tn = 512

    h = x.reshape(T, d)
    h_bf = h.astype(BF16)
    diff_coefs = _bf16_pieces(_alibi_slopes(DIFF_HEADS) * LOG2E)
    dil_slopes = _alibi_slopes(n_pat * DIL_GROUP_HEADS).reshape(DIL_GROUP_HEADS, n_pat)
    colscale = jnp.ones((Z_W,), F32).at[Z_DQ:Z_DK].set(DIFF_HEAD_DIM ** -0.5 * LOG2E)

    for i in range(depth):
        lambda_init = 0.8 - 0.6 * math.exp(-0.3 * i)

        f = _ffn(h_bf, ffn1_w_in[i].astype(BF16), ffn1_w_out[i].astype(BF16))
        dils = tuple(dil for _window, dil in DIL_PATTERNS if dil > 1)
        h, h_bf, h_cm = _res_ln_class_major(h, f, ln1_g[i], ln1_b[i], alpha=alpha, scale=0.5,
                                            batch=batch, seq=seq, dils=dils)
        h_class_major = dict(zip(dils, h_cm))

        z, ffn2_w_in_bf, ffn2_w_out_bf = _matmul(
            h_bf, w_in[i], BF16, n_out=Z_W, wcol=_main_wcol(tn), colscale=colscale, tn=tn,
            riders=((ffn2_w_in[i], (d, V7X_LANES), 1), (ffn2_w_out[i], (RIDER_ROWS, d), 0)),
            name="in_proj")
        lam_rows = jnp.stack([lam_q1[i], lam_k1[i], lam_q2[i], lam_k2[i]]).astype(F32)
        oa = _diff_attention(z, diff_coefs, lam_rows, subln_g[i].astype(F32),
                             batch=batch, seq=seq, lambda_init=lambda_init)

        dil_outs, dil_lses = [], []
        for g, (_window, dil) in enumerate(DIL_PATTERNS):
            slopes_g = dil_slopes[:, g] * dil
            if dil == 1:
                o_g, lse_g = _dilated_attention(z, (Z_LQ, Z_LK, Z_LV), slopes_g, dil=dil,
                                                batch=batch, seq=seq)
            else:
                qkv = _matmul(h_class_major[dil], w_in[i], BF16,
                              n_out=3 * DIL_OUT_W, wcol=_group_wcol(g, tn), tn=tn,
                              name=f"in_proj_dil{dil}")
                o_g, lse_g = _dilated_attention(qkv, (0, DIL_OUT_W, 2 * DIL_OUT_W), slopes_g,
                                                dil=dil, batch=batch, seq=seq)
            dil_outs.append(o_g)
            dil_lses.append(lse_g)

        y = _merge(oa, _dil_mix(dil_outs, dil_lses), z, w_branch_diff[i], w_branch_dil[i])
        pre = _matmul(y, w_mix_out[i], F32, n_out=d, wcol=lambda j: j, residual=(h, alpha),
                      tn=tn, name="mix_out")
        h, h_bf = _ln_presummed(pre, ln2_g[i], ln2_b[i])

        f = _ffn(h_bf, ffn2_w_in_bf, ffn2_w_out_bf)
        h, h_bf = _res_ln(h, f, ln3_g[i], ln3_b[i], alpha=alpha, scale=0.5)

        pre = _ple(h_bf, p[i].reshape(T, PLE_DIM), w_ple_gate[i], w_ple_proj[i], h, alpha=alpha)
        if i + 1 < depth:
            h, h_bf = _ln_presummed(pre, ln4_g[i], ln4_b[i])
        else:
            h = _ln_presummed(pre, ln4_g[i], ln4_b[i], with_bf16=False)

    return h.reshape(batch, seq, d)
```

```python
import functools
import math

import jax
import jax.numpy as jnp
from jax import lax
from jax.experimental import pallas as pl
from jax.experimental.pallas import tpu as pltpu

F32 = jnp.float32
BF16 = jnp.bfloat16

D_MODEL = 4096
PLE_DIM = 256
D_FF = 11008
DIFF_HEADS = 8
DIFF_HEAD_DIM = 128
DIL_PATTERNS = ((128, 1), (512, 4), (2048, 16))
DIL_GROUP_HEADS = 8
DIL_HEAD_DIM = 128
DIL_HALF_KEYS = 64
LN_EPS = 1e-5
SUBLN_EPS = 1e-5
NEG_INF = -1e30

DIFF_QK_W = DIFF_HEADS * 2 * DIFF_HEAD_DIM
DIFF_V_W = DIFF_HEADS * 2 * DIFF_HEAD_DIM
DIL_W = len(DIL_PATTERNS) * DIL_GROUP_HEADS * DIL_HEAD_DIM
DIL_OUT_W = DIL_GROUP_HEADS * DIL_HEAD_DIM
IN_PROJ_W = 2 * DIFF_QK_W + DIFF_V_W + 3 * DIL_W + 2 * D_MODEL

OFF_DQ = 0
OFF_DK = OFF_DQ + DIFF_QK_W
OFF_DV = OFF_DK + DIFF_QK_W
OFF_LQ = OFF_DV + DIFF_V_W
OFF_LK = OFF_LQ + DIL_W
OFF_LV = OFF_LK + DIL_W
OFF_GA = OFF_LV + DIL_W
OFF_GB = OFF_GA + D_MODEL

Z_DQ = 0
Z_DK = Z_DQ + DIFF_QK_W
Z_DV = Z_DK + DIFF_QK_W
Z_LQ = Z_DV + DIFF_V_W
Z_LK = Z_LQ + DIL_OUT_W
Z_LV = Z_LK + DIL_OUT_W
Z_GA = Z_LV + DIL_OUT_W
Z_GB = Z_GA + D_MODEL
Z_W = Z_GB + D_MODEL

V7X_LANES = 128
V7X_VMEM_BYTES = 64 * 1024 * 1024
V7X_VMEM_REQUEST = 56 * 1024 * 1024
V7X_VMEM_REQUEST_LN = 60 * 1024 * 1024

FFN_BLOCK = 256

LOG2E = math.log2(math.e)
COEF_PIECES = 3
POS_SPLIT = 64
RIDER_ROWS = 64


def _params(semantics, vmem_bytes=V7X_VMEM_REQUEST):
    return pltpu.CompilerParams(dimension_semantics=semantics, vmem_limit_bytes=vmem_bytes)


def _ffn_kernel(x_ref, wga_ref, wgb_ref, wua_ref, wub_ref, woa_ref, wob_ref, o_ref):
    j = pl.program_id(1)

    @pl.when(j == 0)
    def _():
        o_ref[...] = jnp.zeros(o_ref.shape, F32)

    x = x_ref[...]
    ha = _swiglu_hidden(x, wga_ref, wua_ref)
    hb = _swiglu_hidden(x, wgb_ref, wub_ref)
    part = jnp.dot(ha, woa_ref[...], preferred_element_type=F32)
    part += jnp.dot(hb, wob_ref[...], preferred_element_type=F32)
    o_ref[...] += part


def _swiglu_hidden(x, wg_ref, wu_ref):
    g = jnp.dot(x, wg_ref[...], preferred_element_type=F32)
    u = jnp.dot(x, wu_ref[...], preferred_element_type=F32)
    return (g * jax.nn.sigmoid(g) * u).astype(BF16)


def _ffn(x_bf, w_in_bf, w_out_bf, *, tm=512, tb=FFN_BLOCK):
    T, D = x_bf.shape
    n_blocks = w_out_bf.shape[0] // tb
    steps = n_blocks // 2

    def first(j):
        return 2 * j

    def second(j):
        return 2 * j + 1

    def col_spec(blk, off):
        return pl.BlockSpec((D, tb), lambda i, j: (0, off + blk(j)))

    def row_spec(blk):
        return pl.BlockSpec((tb, D), lambda i, j: (blk(j), 0))

    return pl.pallas_call(
        _ffn_kernel,
        grid=(T // tm, steps),
        in_specs=[
            pl.BlockSpec((tm, D), lambda i, j: (i, 0)),
            col_spec(first, 0), col_spec(second, 0),
            col_spec(first, n_blocks), col_spec(second, n_blocks),
            row_spec(first), row_spec(second),
        ],
        out_specs=pl.BlockSpec((tm, D), lambda i, j: (i, 0)),
        out_shape=jax.ShapeDtypeStruct((T, D), F32),
        compiler_params=_params(("parallel", "arbitrary")),
        name="ffn",
    )(x_bf, w_in_bf, w_in_bf, w_in_bf, w_in_bf, w_out_bf, w_out_bf)


def _layer_norm_rows(y, g_ref, b_ref):
    mu = jnp.mean(y, axis=-1, keepdims=True)
    yc = y - mu
    var = jnp.mean(yc * yc, axis=-1, keepdims=True)
    return yc * lax.rsqrt(var + LN_EPS) * g_ref[...] + b_ref[...]


def _ffn_tail_specs(w_in_bf, w_out_bf, tm, tb=FFN_BLOCK):
    D = w_in_bf.shape[0]
    n_blocks = w_out_bf.shape[0] // tb
    last = n_blocks - 1
    once = pl.Buffered(1)
    specs = [pl.BlockSpec((tm, D), lambda i: (i, 0)),
             pl.BlockSpec((D, tb), lambda i: (0, last), pipeline_mode=once),
             pl.BlockSpec((D, tb), lambda i: (0, n_blocks + last), pipeline_mode=once),
             pl.BlockSpec((tb, D), lambda i: (last, 0), pipeline_mode=once)]
    return specs, [w_in_bf, w_in_bf, w_out_bf]


def _ffn_ln_rows(h_ref, br_ref, g_ref, b_ref, x_ref, wg_ref, wu_ref, wo_ref, alpha, scale):
    tail = jnp.dot(_swiglu_hidden(x_ref[...], wg_ref, wu_ref), wo_ref[...],
                   preferred_element_type=F32)
    return _layer_norm_rows(alpha * h_ref[...] + scale * (br_ref[...] + tail), g_ref, b_ref)


def _ln_kernel(h_ref, br_ref, g_ref, b_ref, x_ref, wg_ref, wu_ref, wo_ref, o_ref, obf_ref,
               *, alpha, scale):
    out = _ffn_ln_rows(h_ref, br_ref, g_ref, b_ref, x_ref, wg_ref, wu_ref, wo_ref, alpha, scale)
    o_ref[...] = out
    obf_ref[...] = out.astype(BF16)


def _ln_class_major_kernel(h_ref, br_ref, g_ref, b_ref, x_ref, wg_ref, wu_ref, wo_ref,
                           o_ref, obf_ref, *rest, alpha, scale, dils):
    cm_refs, slab_sc = rest[:-1], rest[-1]
    out = _ffn_ln_rows(h_ref, br_ref, g_ref, b_ref, x_ref, wg_ref, wu_ref, wo_ref, alpha, scale)
    o_ref[...] = out
    obf_ref[...] = out.astype(BF16)
    tm, d = out.shape
    for s in range(d // V7X_LANES):
        slab_sc[s] = out[:, s * V7X_LANES:(s + 1) * V7X_LANES]
    for dil, cm_ref in zip(dils, cm_refs):
        for r in range(dil):
            for s in range(d // V7X_LANES):
                cm_ref[r, :, s * V7X_LANES:(s + 1) * V7X_LANES] = slab_sc[
                    s, pl.ds(r, tm // dil, stride=dil), :].astype(BF16)


def _res_ln_class_major(h, branch, g, b, ffn_in, w_in_bf, w_out_bf, *, alpha, scale, batch, seq,
                        dils, tm=256):
    T, D = h.shape
    tiles = seq // tm
    row = pl.BlockSpec((tm, D), lambda i: (i, 0))
    vec = pl.BlockSpec((1, D), lambda i: (0, 0))
    tail_specs, tail_args = _ffn_tail_specs(w_in_bf, w_out_bf, tm)
    cm_specs = [pl.BlockSpec((None, dil, tm // dil, D), lambda i: (i // tiles, 0, i % tiles, 0))
                for dil in dils]
    cm_shapes = [jax.ShapeDtypeStruct((batch, dil, seq // dil, D), BF16) for dil in dils]
    outs = pl.pallas_call(
        functools.partial(_ln_class_major_kernel, alpha=alpha, scale=scale, dils=dils),
        grid=(T // tm,),
        in_specs=[row, row, vec, vec] + tail_specs,
        out_specs=[row, row] + cm_specs,
        out_shape=[jax.ShapeDtypeStruct((T, D), F32), jax.ShapeDtypeStruct((T, D), BF16)]
        + cm_shapes,
        scratch_shapes=[pltpu.VMEM((D // V7X_LANES, tm, V7X_LANES), F32)],
        compiler_params=_params(("parallel",), V7X_VMEM_REQUEST_LN),
        name="res_ln_class_major",
    )(h, branch, g.reshape(1, D), b.reshape(1, D), ffn_in, *tail_args)
    return outs[0], outs[1], [cm.reshape(T, D) for cm in outs[2:]]


def _ln_presummed_kernel(y_ref, g_ref, b_ref, *o_refs):
    out = _layer_norm_rows(y_ref[...], g_ref, b_ref)
    for o_ref in o_refs:
        o_ref[...] = out.astype(o_ref.dtype)


def _ln_presummed(y, g, b, *, with_bf16=True, tm=256):
    T, D = y.shape
    row = pl.BlockSpec((tm, D), lambda i: (i, 0))
    vec = pl.BlockSpec((1, D), lambda i: (0, 0))
    f32_out = jax.ShapeDtypeStruct((T, D), F32)
    return pl.pallas_call(
        _ln_presummed_kernel,
        grid=(T // tm,),
        in_specs=[row, vec, vec],
        out_specs=[row, row] if with_bf16 else row,
        out_shape=[f32_out, jax.ShapeDtypeStruct((T, D), BF16)] if with_bf16 else f32_out,
        compiler_params=_params(("parallel",)),
        name="ln",
    )(y, g.reshape(1, D), b.reshape(1, D))


def _res_ln(h, branch, g, b, ffn_in, w_in_bf, w_out_bf, *, alpha, scale, tm=256):
    T, D = h.shape
    row = pl.BlockSpec((tm, D), lambda i: (i, 0))
    vec = pl.BlockSpec((1, D), lambda i: (0, 0))
    tail_specs, tail_args = _ffn_tail_specs(w_in_bf, w_out_bf, tm)
    return pl.pallas_call(
        functools.partial(_ln_kernel, alpha=alpha, scale=scale),
        grid=(T // tm,),
        in_specs=[row, row, vec, vec] + tail_specs,
        out_specs=[row, row],
        out_shape=[jax.ShapeDtypeStruct((T, D), F32), jax.ShapeDtypeStruct((T, D), BF16)],
        compiler_params=_params(("parallel",)),
        name="res_ln",
    )(h, branch, g.reshape(1, D), b.reshape(1, D), ffn_in, *tail_args)


def _matmul_kernel(*refs, scaled, residual_alpha, rider_blocks, n_col_tiles):
    refs = list(refs)
    n_riders = len(rider_blocks)
    x_ref, w_ref = refs[:2]
    del refs[:2]
    s_ref = refs.pop(0) if scaled else None
    h_ref = refs.pop(0) if residual_alpha is not None else None
    rider_in, o_ref, rider_out = refs[:n_riders], refs[n_riders], refs[n_riders + 1:]
    acc = jnp.dot(x_ref[...], w_ref[...].astype(BF16), preferred_element_type=F32)
    if scaled:
        acc = acc * s_ref[...]
    if residual_alpha is not None:
        acc = residual_alpha * h_ref[...] + acc
    o_ref[...] = acc.astype(o_ref.dtype)
    step = pl.program_id(0) * n_col_tiles + pl.program_id(1)
    for src, dst, n_blocks in zip(rider_in, rider_out, rider_blocks):
        @pl.when(step < n_blocks)
        def _(src=src, dst=dst):
            dst[...] = src[...].astype(BF16)


def _matmul(x_bf, w, out_dtype, *, n_out, wcol, colscale=None, residual=None, riders=(),
            tm=1024, tn=512, name="matmul"):
    T, K = x_bf.shape
    n_col_tiles = n_out // tn
    tile = pl.BlockSpec((tm, tn), lambda i, j: (i, j))
    in_specs = [
        pl.BlockSpec((tm, K), lambda i, j: (i, 0)),
        pl.BlockSpec((K, tn), lambda i, j: (0, wcol(j))),
    ]
    args = [x_bf, w]
    if colscale is not None:
        in_specs.append(pl.BlockSpec((1, tn), lambda i, j: (0, j)))
        args.append(colscale.reshape(1, n_out))
    if residual is not None:
        in_specs.append(tile)
        args.append(residual[0])
    rider_specs, rider_blocks = [], []
    for arr, block, axis in riders:
        n_blocks = arr.shape[axis] // block[axis]
        assert n_blocks <= (T // tm) * n_col_tiles

        def index_map(i, j, axis=axis, n_blocks=n_blocks):
            blk = jnp.minimum(i * n_col_tiles + j, n_blocks - 1)
            return tuple(blk if a == axis else 0 for a in range(2))

        rider_specs.append(pl.BlockSpec(block, index_map))
        rider_blocks.append(n_blocks)
        args.append(arr)
    outs = pl.pallas_call(
        functools.partial(_matmul_kernel, scaled=colscale is not None,
                          residual_alpha=None if residual is None else residual[1],
                          rider_blocks=tuple(rider_blocks), n_col_tiles=n_col_tiles),
        grid=(T // tm, n_col_tiles),
        in_specs=in_specs + rider_specs,
        out_specs=[tile] + rider_specs,
        out_shape=[jax.ShapeDtypeStruct((T, n_out), out_dtype)]
        + [jax.ShapeDtypeStruct(arr.shape, BF16) for arr, _, _ in riders],
        compiler_params=_params(("arbitrary", "arbitrary")),
        name=name,
    )(*args)
    return outs if riders else outs[0]


def _main_wcol(tn):
    per_sec = DIL_OUT_W // tn
    first = OFF_LQ // tn
    n_pat = len(DIL_PATTERNS)

    def wcol(j):
        k = j - first
        dil_tile = first + (k // per_sec) * (n_pat * per_sec) + k % per_sec
        gate_tile = j + (n_pat - 1) * 3 * per_sec
        return jnp.where(j < first, j, jnp.where(k < 3 * per_sec, dil_tile, gate_tile))

    return wcol


def _group_wcol(group, tn):
    per_sec = DIL_OUT_W // tn

    def wcol(j):
        return (OFF_LQ + group * DIL_OUT_W) // tn + (j // per_sec) * (DIL_W // tn) + j % per_sec

    return wcol


def _alibi_columns(pos, coefs, q_side):
    lane = lax.broadcasted_iota(jnp.int32, pos.shape, 1)
    piece = jnp.where(lane < 3, lane, jnp.where(lane < 6, lane - 3,
                                                jnp.where(lane < 9, lane - 6, lane - 9)))
    cv = jnp.where(piece == 0, coefs[0], jnp.where(piece == 1, coefs[1], coefs[2]))
    hi = (pos - (pos & (POS_SPLIT - 1))).astype(F32)
    lo = (pos & (POS_SPLIT - 1)).astype(F32)
    n = 2 * COEF_PIECES
    if q_side:
        cols = jnp.where(lane < COEF_PIECES, -hi, jnp.where(lane < n, -lo, cv))
    else:
        cols = jnp.where(lane < n, cv, jnp.where(lane < n + COEF_PIECES, hi, lo))
    return jnp.where(lane < 2 * n, cols, 0.0)


def _diff_attn_kernel(coef_ref, lam_ref, g_ref, q_ref, k_ref, v_ref, o_ref,
                      kaug_sc, qaug_sc, corr_sc, sa_sc, sb_sc, pa_sc, pb_sc, m_sc, l_sc,
                      alphaa_sc, alphab_sc, acc_sc, *, tq, rb, seq, lambda_init):
    h = pl.program_id(1)
    qi = pl.program_id(2)
    dh = DIFF_HEAD_DIM
    n_chunks = seq // tq
    coefs = [coef_ref[h, t] for t in range(COEF_PIECES)]
    row_iota = lax.broadcasted_iota(jnp.int32, (tq, dh), 0)

    @pl.when(qi == 0)
    def _():
        def build(t, carry):
            r0 = pl.multiple_of(t * tq, tq)
            ak = _alibi_columns(row_iota + r0, coefs, q_side=False).astype(BF16)
            for c in range(2):
                kaug_sc[c, pl.ds(r0, tq), 0:dh] = k_ref[pl.ds(r0, tq), c * dh:(c + 1) * dh]
                kaug_sc[c, pl.ds(r0, tq), dh:2 * dh] = ak
            return carry

        lax.fori_loop(0, n_chunks, build, 0)
        row = lax.broadcasted_iota(jnp.int32, (tq, tq), 0)
        col = lax.broadcasted_iota(jnp.int32, (tq, tq), 1)
        slope2 = coefs[0] + coefs[1] + coefs[2]
        corr_sc[0] = jnp.zeros((tq, tq), F32)
        corr_sc[1] = (2.0 * slope2) * jnp.minimum(row - col, 0).astype(F32)

    aq = _alibi_columns(row_iota + qi * tq, coefs, q_side=True)
    for c in range(2):
        q_c = q_ref[:, c * dh:(c + 1) * dh]
        qaug_sc[0, c, :, 0:dh] = q_c
        qaug_sc[0, c, :, dh:2 * dh] = aq.astype(BF16)
        qaug_sc[1, c, :, 0:dh] = q_c
        qaug_sc[1, c, :, dh:2 * dh] = (-aq).astype(BF16)

    m_sc[...] = jnp.full(m_sc.shape, -jnp.inf, F32)
    l_sc[...] = jnp.zeros(l_sc.shape, F32)
    acc_sc[...] = jnp.zeros(acc_sc.shape, F32)

    n_slabs = tq // V7X_LANES
    last = n_chunks - 1

    def scores(kc, s_ref):
        ks = pl.multiple_of(kc * tq, tq)
        side = jnp.where(kc > qi, 1, 0)
        for c in range(2):
            s_ref[c] = lax.dot_general(
                qaug_sc[side, c], kaug_sc[c, pl.ds(ks, tq), :],
                (((1,), (1,)), ((), ())), preferred_element_type=F32)

    def softmax(kc, s_ref, p_ref, alpha_ref, near_diag):
        on_diag = jnp.where(kc == qi, 1, 0)
        for c in range(2):
            for b in range(tq // rb):
                rows = slice(b * rb, (b + 1) * rb)
                sb = s_ref[c, rows, :]
                if near_diag:
                    sb = sb + corr_sc[on_diag, rows, :]
                slabs = [sb[:, t * V7X_LANES:(t + 1) * V7X_LANES] for t in range(n_slabs)]
                mx = functools.reduce(jnp.maximum, slabs)
                m_old = m_sc[c, rows, :]
                m_new = jnp.maximum(m_old, jnp.broadcast_to(
                    jnp.max(mx, axis=-1, keepdims=True), (rb, V7X_LANES)))
                a = jnp.exp2(m_old - m_new)
                ps = [jnp.exp2(sl - m_new) for sl in slabs]
                l_sc[c, rows, :] = a * l_sc[c, rows, :] + functools.reduce(jnp.add, ps)
                p_ref[c, rows, :] = jnp.concatenate(ps, axis=-1).astype(BF16)
                alpha_ref[c, rows, :] = a
                m_sc[c, rows, :] = m_new

    def accumulate(kc, p_ref, alpha_ref):
        ks = pl.multiple_of(kc * tq, tq)
        for c in range(2):
            alpha = alpha_ref[c]
            acc_sc[c] = jnp.concatenate([alpha, alpha], axis=-1) * acc_sc[c] + jnp.dot(
                p_ref[c], v_ref[pl.ds(ks, tq), :], preferred_element_type=F32)

    def pair_body(kc, near_diag, first=False, final=False):
        if not first:
            accumulate(kc - 1, pb_sc, alphab_sc)
        scores(kc + 1, sb_sc)
        softmax(kc, sa_sc, pa_sc, alphaa_sc, near_diag)
        accumulate(kc, pa_sc, alphaa_sc)
        if not final:
            scores(kc + 2, sa_sc)
        softmax(kc + 1, sb_sc, pb_sc, alphab_sc, near_diag)

    def pair(t, first=False, final=False):
        lax.cond(t == qi // 2,
                 functools.partial(pair_body, 2 * t, True, first, final),
                 functools.partial(pair_body, 2 * t, False, first, final))

    def middle_pair(t, carry):
        pair(t)
        return carry

    n_pairs = n_chunks // 2
    scores(0, sa_sc)
    pair(0, first=True)
    lax.fori_loop(1, n_pairs - 1, middle_pair, 0)
    pair(n_pairs - 1, final=True)
    accumulate(last, pb_sc, alphab_sc)

    lam_rows = lam_ref[...]
    lam = (jnp.exp(jnp.sum(lam_rows[0:1] * lam_rows[1:2], axis=-1, keepdims=True))
           - jnp.exp(jnp.sum(lam_rows[2:3] * lam_rows[3:4], axis=-1, keepdims=True))
           + lambda_init)
    l0 = jnp.sum(l_sc[0], axis=-1, keepdims=True)
    l1 = jnp.sum(l_sc[1], axis=-1, keepdims=True)
    o = acc_sc[0] / l0 - lam * (acc_sc[1] / l1)
    ms = jnp.mean(o * o, axis=-1, keepdims=True)
    o = o * lax.rsqrt(ms + SUBLN_EPS) * g_ref[...] * (1.0 - lambda_init)
    o_ref[...] = o.astype(o_ref.dtype)


def _diff_attention(z, coefs, lam_rows, subln_g, *, batch, seq, lambda_init, tq=512, rb=32):
    T = z.shape[0]
    e = 2 * DIFF_HEAD_DIM
    nq = seq // tq
    qspec = pl.BlockSpec((tq, e), lambda b, h, i: (b * nq + i, Z_DQ // e + h))
    kspec = pl.BlockSpec((seq, e), lambda b, h, i: (b, Z_DK // e + h))
    vspec = pl.BlockSpec((seq, e), lambda b, h, i: (b, Z_DV // e + h))
    return pl.pallas_call(
        functools.partial(_diff_attn_kernel, tq=tq, rb=rb, seq=seq, lambda_init=lambda_init),
        grid=(batch, DIFF_HEADS, nq),
        in_specs=[
            pl.BlockSpec(memory_space=pltpu.SMEM),
            pl.BlockSpec((4, DIFF_HEAD_DIM), lambda b, h, i: (0, 0)),
            pl.BlockSpec((1, e), lambda b, h, i: (0, 0)),
            qspec, kspec, vspec,
        ],
        out_specs=pl.BlockSpec((tq, e), lambda b, h, i: (b * nq + i, h)),
        out_shape=jax.ShapeDtypeStruct((T, DIFF_V_W), BF16),
        scratch_shapes=[
            pltpu.VMEM((2, seq, e), BF16),
            pltpu.VMEM((2, 2, tq, e), BF16),
            pltpu.VMEM((2, tq, tq), F32),
            pltpu.VMEM((2, tq, tq), F32),
            pltpu.VMEM((2, tq, tq), F32),
            pltpu.VMEM((2, tq, tq), BF16),
            pltpu.VMEM((2, tq, tq), BF16),
            pltpu.VMEM((2, tq, V7X_LANES), F32),
            pltpu.VMEM((2, tq, V7X_LANES), F32),
            pltpu.VMEM((2, tq, V7X_LANES), F32),
            pltpu.VMEM((2, tq, V7X_LANES), F32),
            pltpu.VMEM((2, tq, e), F32),
        ],
        compiler_params=_params(("parallel", "parallel", "arbitrary")),
        name="diff_attn",
    )(coefs, lam_rows, subln_g.reshape(1, e), z, z, z)


def _dil_attn_kernel(slopes_ref, q_ref, k_ref, v_ref, o_ref, lse_ref, *, seq, dil, tq, group):
    h = pl.program_id(1)
    slope = slopes_ref[h]
    w = DIL_HALF_KEYS
    nk = tq + 2 * w
    length = seq // dil
    blocks_per_class = length // tq
    scale = DIL_HEAD_DIM ** -0.5
    row = lax.broadcasted_iota(jnp.int32, (tq, nk), 0)
    col = lax.broadcasted_iota(jnp.int32, (tq, nk), 1)
    col_minus_row = col - row

    def block_offsets(i):
        cls = i // blocks_per_class
        qs_local = (i % blocks_per_class) * tq
        ks_local = jnp.clip(qs_local - w, 0, length - nk)
        return cls, qs_local, ks_local

    def body(it, carry):
        blocks = [block_offsets(it * group + g) for g in range(group)]
        scores = []
        for cls, qs_local, ks_local in blocks:
            q = q_ref[pl.ds(pl.multiple_of(cls * length + qs_local, tq), tq), :]
            k = k_ref[pl.ds(pl.multiple_of(cls * length + ks_local, w), nk), :]
            scores.append(lax.dot_general(q, k, (((1,), (1,)), ((), ())),
                                          preferred_element_type=F32))
        for (cls, qs_local, ks_local), s in zip(blocks, scores):
            rel = jnp.abs(col_minus_row + (ks_local - qs_local))
            s = jnp.where(rel <= w, s * scale - slope * rel.astype(F32), NEG_INF)
            m = jnp.max(s, axis=-1, keepdims=True)
            e = jnp.exp(s - m)
            z = jnp.sum(e, axis=-1, keepdims=True)
            v = v_ref[pl.ds(pl.multiple_of(cls * length + ks_local, w), nk), :]
            o = jnp.dot(e.astype(BF16), v, preferred_element_type=F32) / z
            if dil == 1:
                rows = pl.ds(pl.multiple_of(qs_local, tq), tq)
            else:
                rows = pl.ds(qs_local * dil + cls, tq, stride=dil)
            o_ref[rows, :] = o
            lse_ref[rows, :] = jnp.broadcast_to(m + jnp.log(z), (tq, DIL_HEAD_DIM))
        return carry

    lax.fori_loop(0, seq // (tq * group), body, 0)


def _dilated_attention(qkv, offsets, slopes_g, *, dil, batch, seq, tq=128, group=16):
    T = qkv.shape[0]
    hd = DIL_HEAD_DIM

    def in_spec(off):
        return pl.BlockSpec((seq, hd), lambda b, h: (b, off // hd + h))

    out_spec = pl.BlockSpec((seq, hd), lambda b, h: (b, h))
    out_sds = jax.ShapeDtypeStruct((T, DIL_OUT_W), F32)
    return pl.pallas_call(
        functools.partial(_dil_attn_kernel, seq=seq, dil=dil, tq=tq, group=group),
        grid=(batch, DIL_GROUP_HEADS),
        in_specs=[pl.BlockSpec(memory_space=pltpu.SMEM)] + [in_spec(o) for o in offsets],
        out_specs=[out_spec, out_spec],
        out_shape=[out_sds, out_sds],
        compiler_params=_params(("parallel", "parallel")),
        name=f"dil_attn_{dil}",
    )(slopes_g, qkv, qkv, qkv)


def _dil_mix_kernel(o0_ref, o1_ref, o2_ref, l0_ref, l1_ref, l2_ref, ob_ref):
    l0, l1, l2 = l0_ref[...], l1_ref[...], l2_ref[...]
    m = jnp.maximum(jnp.maximum(l0, l1), l2)
    w0, w1, w2 = jnp.exp(l0 - m), jnp.exp(l1 - m), jnp.exp(l2 - m)
    ob = (w0 * o0_ref[...] + w1 * o1_ref[...] + w2 * o2_ref[...]) / (w0 + w1 + w2)
    ob_ref[...] = ob.astype(ob_ref.dtype)


def _dil_mix(dil_outs, dil_lses, *, tm=512):
    T, W = dil_outs[0].shape
    row = pl.BlockSpec((tm, W), lambda i: (i, 0))
    return pl.pallas_call(
        _dil_mix_kernel,
        grid=(T // tm,),
        in_specs=[row] * 6,
        out_specs=row,
        out_shape=jax.ShapeDtypeStruct((T, W), BF16),
        compiler_params=_params(("parallel",)),
        name="dil_mix",
    )(*dil_outs, *dil_lses)


def _merge_kernel(oa_ref, ob_ref, ga_ref, gb_ref, wa_ref, wb_ref, y_ref):
    ya = jnp.dot(oa_ref[...], wa_ref[...].astype(BF16), preferred_element_type=F32)
    yb = jnp.dot(ob_ref[...], wb_ref[...].astype(BF16), preferred_element_type=F32)
    y = (jax.nn.sigmoid(ga_ref[...].astype(F32)) * ya
         + jax.nn.sigmoid(gb_ref[...].astype(F32)) * yb)
    y_ref[...] = y.astype(y_ref.dtype)


def _merge(oa, ob, z, wa, wb, *, tm=1024, tn=512):
    T = oa.shape[0]
    return pl.pallas_call(
        _merge_kernel,
        grid=(T // tm, D_MODEL // tn),
        in_specs=[pl.BlockSpec((tm, DIFF_V_W), lambda i, j: (i, 0)),
                  pl.BlockSpec((tm, DIL_OUT_W), lambda i, j: (i, 0)),
                  pl.BlockSpec((tm, tn), lambda i, j: (i, Z_GA // tn + j)),
                  pl.BlockSpec((tm, tn), lambda i, j: (i, Z_GB // tn + j)),
                  pl.BlockSpec((DIFF_V_W, tn), lambda i, j: (0, j)),
                  pl.BlockSpec((DIL_OUT_W, tn), lambda i, j: (0, j))],
        out_specs=pl.BlockSpec((tm, tn), lambda i, j: (i, j)),
        out_shape=jax.ShapeDtypeStruct((T, D_MODEL), BF16),
        compiler_params=_params(("parallel", "parallel")),
        name="merge",
    )(oa, ob, z, z, wa, wb)


def _ple_kernel(x_ref, p_ref, wg_ref, wp_ref, h_ref, o_ref, *, alpha):
    gate = jnp.dot(x_ref[...], wg_ref[...].astype(BF16), preferred_element_type=F32)
    proj = jnp.dot(p_ref[...].astype(BF16), wp_ref[...].astype(BF16),
                   preferred_element_type=F32)
    o_ref[...] = alpha * h_ref[...] + jax.nn.sigmoid(gate) * proj


def _ple(x_bf, p, wg, wp, h, *, alpha, tm=1024, tn=512):
    T, D = x_bf.shape
    tile = pl.BlockSpec((tm, tn), lambda i, j: (i, j))
    return pl.pallas_call(
        functools.partial(_ple_kernel, alpha=alpha),
        grid=(T // tm, D // tn),
        in_specs=[
            pl.BlockSpec((tm, D), lambda i, j: (i, 0)),
            pl.BlockSpec((tm, PLE_DIM), lambda i, j: (i, 0)),
            pl.BlockSpec((D, tn), lambda i, j: (0, j)),
            pl.BlockSpec((PLE_DIM, tn), lambda i, j: (0, j)),
            tile,
        ],
        out_specs=tile,
        out_shape=jax.ShapeDtypeStruct((T, D), F32),
        compiler_params=_params(("parallel", "parallel")),
        name="ple",
    )(x_bf, p, wg, wp, h)


def _alibi_slopes(n):
    return jnp.exp2(-8.0 * jnp.arange(1, n + 1, dtype=F32) / n)


def _bf16_pieces(v):
    pieces, rest = [], v
    for _ in range(COEF_PIECES):
        piece = rest.astype(BF16).astype(F32)
        pieces.append(piece)
        rest = rest - piece
    return jnp.stack(pieces, axis=-1)


def kernel(x, p, ffn1_w_in, ffn1_w_out, ln1_g, ln1_b, w_in, lam_q1, lam_k1, lam_q2, lam_k2,
           subln_g, w_branch_diff, w_branch_dil, w_mix_out, ln2_g, ln2_b, ffn2_w_in,
           ffn2_w_out, ln3_g, ln3_b, w_ple_gate, w_ple_proj, ln4_g, ln4_b):
    batch, seq, d = x.shape
    depth = ffn1_w_in.shape[0]
    T = batch * seq
    alpha = (2 * depth) ** 0.25
    n_pat = len(DIL_PATTERNS)
    tn = 512

    h = x.reshape(T, d)
    h_bf = h.astype(BF16)
    diff_coefs = _bf16_pieces(_alibi_slopes(DIFF_HEADS) * LOG2E)
    dil_slopes = _alibi_slopes(n_pat * DIL_GROUP_HEADS).reshape(DIL_GROUP_HEADS, n_pat)
    colscale = jnp.ones((Z_W,), F32).at[Z_DQ:Z_DK].set(DIFF_HEAD_DIM ** -0.5 * LOG2E)

    for i in range(depth):
        lambda_init = 0.8 - 0.6 * math.exp(-0.3 * i)

        ffn1_w_in_bf, ffn1_w_out_bf = ffn1_w_in[i].astype(BF16), ffn1_w_out[i].astype(BF16)
        f = _ffn(h_bf, ffn1_w_in_bf, ffn1_w_out_bf)
        dils = tuple(dil for _window, dil in DIL_PATTERNS if dil > 1)
        h, h_bf, h_cm = _res_ln_class_major(h, f, ln1_g[i], ln1_b[i], h_bf, ffn1_w_in_bf,
                                            ffn1_w_out_bf, alpha=alpha, scale=0.5,
                                            batch=batch, seq=seq, dils=dils)
        h_class_major = dict(zip(dils, h_cm))

        z, ffn2_w_in_bf, ffn2_w_out_bf = _matmul(
            h_bf, w_in[i], BF16, n_out=Z_W, wcol=_main_wcol(tn), colscale=colscale, tn=tn,
            riders=((ffn2_w_in[i], (d, V7X_LANES), 1), (ffn2_w_out[i], (RIDER_ROWS, d), 0)),
            name="in_proj")
        lam_rows = jnp.stack([lam_q1[i], lam_k1[i], lam_q2[i], lam_k2[i]]).astype(F32)
        oa = _diff_attention(z, diff_coefs, lam_rows, subln_g[i].astype(F32),
                             batch=batch, seq=seq, lambda_init=lambda_init)

        dil_outs, dil_lses = [], []
        for g, (_window, dil) in enumerate(DIL_PATTERNS):
            slopes_g = dil_slopes[:, g] * dil
            if dil == 1:
                o_g, lse_g = _dilated_attention(z, (Z_LQ, Z_LK, Z_LV), slopes_g, dil=dil,
                                                batch=batch, seq=seq)
            else:
                qkv = _matmul(h_class_major[dil], w_in[i], BF16,
                              n_out=3 * DIL_OUT_W, wcol=_group_wcol(g, tn), tn=tn,
                              name=f"in_proj_dil{dil}")
                o_g, lse_g = _dilated_attention(qkv, (0, DIL_OUT_W, 2 * DIL_OUT_W), slopes_g,
                                                dil=dil, batch=batch, seq=seq)
            dil_outs.append(o_g)
            dil_lses.append(lse_g)

        y = _merge(oa, _dil_mix(dil_outs, dil_lses), z, w_branch_diff[i], w_branch_dil[i])
        pre = _matmul(y, w_mix_out[i], F32, n_out=d, wcol=lambda j: j, residual=(h, alpha),
                      tn=tn, name="mix_out")
        h, h_bf = _ln_presummed(pre, ln2_g[i], ln2_b[i])

        f = _ffn(h_bf, ffn2_w_in_bf, ffn2_w_out_bf)
        h, h_bf = _res_ln(h, f, ln3_g[i], ln3_b[i], h_bf, ffn2_w_in_bf, ffn2_w_out_bf,
                          alpha=alpha, scale=0.5)

        pre = _ple(h_bf, p[i].reshape(T, PLE_DIM), w_ple_gate[i], w_ple_proj[i], h, alpha=alpha)
        if i + 1 < depth:
            h, h_bf = _ln_presummed(pre, ln4_g[i], ln4_b[i])
        else:
            h = _ln_presummed(pre, ln4_g[i], ln4_b[i], with_bf16=False)

    return h.reshape(batch, seq, d)
```

```python
import functools
import math

import jax
import jax.numpy as jnp
from jax import lax
from jax.experimental import pallas as pl
from jax.experimental.pallas import tpu as pltpu

F32 = jnp.float32
BF16 = jnp.bfloat16

D_MODEL = 4096
PLE_DIM = 256
D_FF = 11008
DIFF_HEADS = 8
DIFF_HEAD_DIM = 128
DIL_PATTERNS = ((128, 1), (512, 4), (2048, 16))
DIL_GROUP_HEADS = 8
DIL_HEAD_DIM = 128
DIL_HALF_KEYS = 64
LN_EPS = 1e-5
SUBLN_EPS = 1e-5
NEG_INF = -1e30

DIFF_QK_W = DIFF_HEADS * 2 * DIFF_HEAD_DIM
DIFF_V_W = DIFF_HEADS * 2 * DIFF_HEAD_DIM
DIL_W = len(DIL_PATTERNS) * DIL_GROUP_HEADS * DIL_HEAD_DIM
DIL_OUT_W = DIL_GROUP_HEADS * DIL_HEAD_DIM
IN_PROJ_W = 2 * DIFF_QK_W + DIFF_V_W + 3 * DIL_W + 2 * D_MODEL

OFF_DQ = 0
OFF_DK = OFF_DQ + DIFF_QK_W
OFF_DV = OFF_DK + DIFF_QK_W
OFF_LQ = OFF_DV + DIFF_V_W
OFF_LK = OFF_LQ + DIL_W
OFF_LV = OFF_LK + DIL_W
OFF_GA = OFF_LV + DIL_W
OFF_GB = OFF_GA + D_MODEL

Z_DQ = 0
Z_DK = Z_DQ + DIFF_QK_W
Z_DV = Z_DK + DIFF_QK_W
Z_LQ = Z_DV + DIFF_V_W
Z_LK = Z_LQ + DIL_OUT_W
Z_LV = Z_LK + DIL_OUT_W
Z_GA = Z_LV + DIL_OUT_W
Z_GB = Z_GA + D_MODEL
Z_W = Z_GB + D_MODEL

V7X_LANES = 128
V7X_VMEM_BYTES = 64 * 1024 * 1024
V7X_VMEM_REQUEST = 56 * 1024 * 1024
V7X_VMEM_REQUEST_LN = 60 * 1024 * 1024

FFN_BLOCK = 256

LOG2E = math.log2(math.e)
COEF_PIECES = 3
POS_SPLIT = 64
RIDER_ROWS = 64


def _params(semantics, vmem_bytes=V7X_VMEM_REQUEST):
    return pltpu.CompilerParams(dimension_semantics=semantics, vmem_limit_bytes=vmem_bytes)


def _ffn_kernel(x_ref, wga_ref, wgb_ref, wua_ref, wub_ref, woa_ref, wob_ref, o_ref):
    j = pl.program_id(1)

    @pl.when(j == 0)
    def _():
        o_ref[...] = jnp.zeros(o_ref.shape, F32)

    x = x_ref[...]
    ha = _swiglu_hidden(x, wga_ref, wua_ref)
    hb = _swiglu_hidden(x, wgb_ref, wub_ref)
    part = jnp.dot(ha, woa_ref[...], preferred_element_type=F32)
    part += jnp.dot(hb, wob_ref[...], preferred_element_type=F32)
    o_ref[...] += part


def _swiglu_hidden(x, wg_ref, wu_ref):
    g = jnp.dot(x, wg_ref[...], preferred_element_type=F32)
    u = jnp.dot(x, wu_ref[...], preferred_element_type=F32)
    return (g * jax.nn.sigmoid(g) * u).astype(BF16)


def _ffn(x_bf, w_in_bf, w_out_bf, *, tm=512, tb=FFN_BLOCK):
    T, D = x_bf.shape
    n_blocks = w_out_bf.shape[0] // tb
    steps = n_blocks // 2

    def first(j):
        return 2 * j

    def second(j):
        return 2 * j + 1

    def col_spec(blk, off):
        return pl.BlockSpec((D, tb), lambda i, j: (0, off + blk(j)))

    def row_spec(blk):
        return pl.BlockSpec((tb, D), lambda i, j: (blk(j), 0))

    return pl.pallas_call(
        _ffn_kernel,
        grid=(T // tm, steps),
        in_specs=[
            pl.BlockSpec((tm, D), lambda i, j: (i, 0)),
            col_spec(first, 0), col_spec(second, 0),
            col_spec(first, n_blocks), col_spec(second, n_blocks),
            row_spec(first), row_spec(second),
        ],
        out_specs=pl.BlockSpec((tm, D), lambda i, j: (i, 0)),
        out_shape=jax.ShapeDtypeStruct((T, D), F32),
        compiler_params=_params(("parallel", "arbitrary")),
        name="ffn",
    )(x_bf, w_in_bf, w_in_bf, w_in_bf, w_in_bf, w_out_bf, w_out_bf)


def _layer_norm_rows(y, g_ref, b_ref):
    mu = jnp.mean(y, axis=-1, keepdims=True)
    yc = y - mu
    var = jnp.mean(yc * yc, axis=-1, keepdims=True)
    return yc * lax.rsqrt(var + LN_EPS) * g_ref[...] + b_ref[...]


def _ffn_tail_specs(w_in_bf, w_out_bf, tm, tb=FFN_BLOCK):
    D = w_in_bf.shape[0]
    n_blocks = w_out_bf.shape[0] // tb
    last = n_blocks - 1
    once = pl.Buffered(1)
    specs = [pl.BlockSpec((tm, D), lambda i: (i, 0)),
             pl.BlockSpec((D, tb), lambda i: (0, last), pipeline_mode=once),
             pl.BlockSpec((D, tb), lambda i: (0, n_blocks + last), pipeline_mode=once),
             pl.BlockSpec((tb, D), lambda i: (last, 0), pipeline_mode=once)]
    return specs, [w_in_bf, w_in_bf, w_out_bf]


def _ffn_ln_rows(h_ref, br_ref, g_ref, b_ref, x_ref, wg_ref, wu_ref, wo_ref, alpha, scale):
    tail = jnp.dot(_swiglu_hidden(x_ref[...], wg_ref, wu_ref), wo_ref[...],
                   preferred_element_type=F32)
    return _layer_norm_rows(alpha * h_ref[...] + scale * (br_ref[...] + tail), g_ref, b_ref)


def _ln_kernel(h_ref, br_ref, g_ref, b_ref, x_ref, wg_ref, wu_ref, wo_ref, o_ref, obf_ref,
               *, alpha, scale):
    out = _ffn_ln_rows(h_ref, br_ref, g_ref, b_ref, x_ref, wg_ref, wu_ref, wo_ref, alpha, scale)
    o_ref[...] = out
    obf_ref[...] = out.astype(BF16)


def _ln_class_major_kernel(h_ref, br_ref, g_ref, b_ref, x_ref, wg_ref, wu_ref, wo_ref,
                           o_ref, obf_ref, *rest, alpha, scale, dils):
    cm_refs, slab_sc = rest[:-1], rest[-1]
    out = _ffn_ln_rows(h_ref, br_ref, g_ref, b_ref, x_ref, wg_ref, wu_ref, wo_ref, alpha, scale)
    o_ref[...] = out
    obf_ref[...] = out.astype(BF16)
    tm, d = out.shape
    for s in range(d // V7X_LANES):
        slab_sc[s] = out[:, s * V7X_LANES:(s + 1) * V7X_LANES]
    for dil, cm_ref in zip(dils, cm_refs):
        for r in range(dil):
            for s in range(d // V7X_LANES):
                cm_ref[r, :, s * V7X_LANES:(s + 1) * V7X_LANES] = slab_sc[
                    s, pl.ds(r, tm // dil, stride=dil), :].astype(BF16)


def _res_ln_class_major(h, branch, g, b, ffn_in, w_in_bf, w_out_bf, *, alpha, scale, batch, seq,
                        dils, tm=256):
    T, D = h.shape
    tiles = seq // tm
    row = pl.BlockSpec((tm, D), lambda i: (i, 0))
    vec = pl.BlockSpec((1, D), lambda i: (0, 0))
    tail_specs, tail_args = _ffn_tail_specs(w_in_bf, w_out_bf, tm)
    cm_specs = [pl.BlockSpec((None, dil, tm // dil, D), lambda i: (i // tiles, 0, i % tiles, 0))
                for dil in dils]
    cm_shapes = [jax.ShapeDtypeStruct((batch, dil, seq // dil, D), BF16) for dil in dils]
    outs = pl.pallas_call(
        functools.partial(_ln_class_major_kernel, alpha=alpha, scale=scale, dils=dils),
        grid=(T // tm,),
        in_specs=[row, row, vec, vec] + tail_specs,
        out_specs=[row, row] + cm_specs,
        out_shape=[jax.ShapeDtypeStruct((T, D), F32), jax.ShapeDtypeStruct((T, D), BF16)]
        + cm_shapes,
        scratch_shapes=[pltpu.VMEM((D // V7X_LANES, tm, V7X_LANES), F32)],
        compiler_params=_params(("parallel",), V7X_VMEM_REQUEST_LN),
        name="res_ln_class_major",
    )(h, branch, g.reshape(1, D), b.reshape(1, D), ffn_in, *tail_args)
    return outs[0], outs[1], [cm.reshape(T, D) for cm in outs[2:]]


def _ln_presummed_kernel(y_ref, g_ref, b_ref, *o_refs):
    out = _layer_norm_rows(y_ref[...], g_ref, b_ref)
    for o_ref in o_refs:
        o_ref[...] = out.astype(o_ref.dtype)


def _ln_presummed(y, g, b, *, with_bf16=True, tm=256):
    T, D = y.shape
    row = pl.BlockSpec((tm, D), lambda i: (i, 0))
    vec = pl.BlockSpec((1, D), lambda i: (0, 0))
    f32_out = jax.ShapeDtypeStruct((T, D), F32)
    return pl.pallas_call(
        _ln_presummed_kernel,
        grid=(T // tm,),
        in_specs=[row, vec, vec],
        out_specs=[row, row] if with_bf16 else row,
        out_shape=[f32_out, jax.ShapeDtypeStruct((T, D), BF16)] if with_bf16 else f32_out,
        compiler_params=_params(("parallel",)),
        name="ln",
    )(y, g.reshape(1, D), b.reshape(1, D))


def _res_ln(h, branch, g, b, ffn_in, w_in_bf, w_out_bf, *, alpha, scale, tm=256):
    T, D = h.shape
    row = pl.BlockSpec((tm, D), lambda i: (i, 0))
    vec = pl.BlockSpec((1, D), lambda i: (0, 0))
    tail_specs, tail_args = _ffn_tail_specs(w_in_bf, w_out_bf, tm)
    return pl.pallas_call(
        functools.partial(_ln_kernel, alpha=alpha, scale=scale),
        grid=(T // tm,),
        in_specs=[row, row, vec, vec] + tail_specs,
        out_specs=[row, row],
        out_shape=[jax.ShapeDtypeStruct((T, D), F32), jax.ShapeDtypeStruct((T, D), BF16)],
        compiler_params=_params(("parallel",)),
        name="res_ln",
    )(h, branch, g.reshape(1, D), b.reshape(1, D), ffn_in, *tail_args)


def _matmul_kernel(*refs, scaled, residual_alpha, rider_blocks, n_col_tiles):
    refs = list(refs)
    n_riders = len(rider_blocks)
    x_ref, w_ref = refs[:2]
    del refs[:2]
    s_ref = refs.pop(0) if scaled else None
    h_ref = refs.pop(0) if residual_alpha is not None else None
    rider_in, o_ref, rider_out = refs[:n_riders], refs[n_riders], refs[n_riders + 1:]
    acc = jnp.dot(x_ref[...], w_ref[...].astype(BF16), preferred_element_type=F32)
    if scaled:
        acc = acc * s_ref[...]
    if residual_alpha is not None:
        acc = residual_alpha * h_ref[...] + acc
    o_ref[...] = acc.astype(o_ref.dtype)
    step = pl.program_id(0) * n_col_tiles + pl.program_id(1)
    for src, dst, n_blocks in zip(rider_in, rider_out, rider_blocks):
        @pl.when(step < n_blocks)
        def _(src=src, dst=dst):
            dst[...] = src[...].astype(BF16)


def _matmul(x_bf, w, out_dtype, *, n_out, wcol, colscale=None, residual=None, riders=(),
            tm=1024, tn=512, name="matmul"):
    T, K = x_bf.shape
    n_col_tiles = n_out // tn
    tile = pl.BlockSpec((tm, tn), lambda i, j: (i, j))
    in_specs = [
        pl.BlockSpec((tm, K), lambda i, j: (i, 0)),
        pl.BlockSpec((K, tn), lambda i, j: (0, wcol(j))),
    ]
    args = [x_bf, w]
    if colscale is not None:
        in_specs.append(pl.BlockSpec((1, tn), lambda i, j: (0, j)))
        args.append(colscale.reshape(1, n_out))
    if residual is not None:
        in_specs.append(tile)
        args.append(residual[0])
    rider_specs, rider_blocks = [], []
    for arr, block, axis in riders:
        n_blocks = arr.shape[axis] // block[axis]
        assert n_blocks <= (T // tm) * n_col_tiles

        def index_map(i, j, axis=axis, n_blocks=n_blocks):
            blk = jnp.minimum(i * n_col_tiles + j, n_blocks - 1)
            return tuple(blk if a == axis else 0 for a in range(2))

        rider_specs.append(pl.BlockSpec(block, index_map))
        rider_blocks.append(n_blocks)
        args.append(arr)
    outs = pl.pallas_call(
        functools.partial(_matmul_kernel, scaled=colscale is not None,
                          residual_alpha=None if residual is None else residual[1],
                          rider_blocks=tuple(rider_blocks), n_col_tiles=n_col_tiles),
        grid=(T // tm, n_col_tiles),
        in_specs=in_specs + rider_specs,
        out_specs=[tile] + rider_specs,
        out_shape=[jax.ShapeDtypeStruct((T, n_out), out_dtype)]
        + [jax.ShapeDtypeStruct(arr.shape, BF16) for arr, _, _ in riders],
        compiler_params=_params(("arbitrary", "arbitrary")),
        name=name,
    )(*args)
    return outs if riders else outs[0]


def _main_wcol(tn):
    per_sec = DIL_OUT_W // tn
    first = OFF_LQ // tn
    n_pat = len(DIL_PATTERNS)

    def wcol(j):
        k = j - first
        dil_tile = first + (k // per_sec) * (n_pat * per_sec) + k % per_sec
        gate_tile = j + (n_pat - 1) * 3 * per_sec
        return jnp.where(j < first, j, jnp.where(k < 3 * per_sec, dil_tile, gate_tile))

    return wcol


def _group_wcol(group, tn):
    per_sec = DIL_OUT_W // tn

    def wcol(j):
        return (OFF_LQ + group * DIL_OUT_W) // tn + (j // per_sec) * (DIL_W // tn) + j % per_sec

    return wcol


def _alibi_columns(pos, coefs, q_side):
    lane = lax.broadcasted_iota(jnp.int32, pos.shape, 1)
    piece = jnp.where(lane < 3, lane, jnp.where(lane < 6, lane - 3,
                                                jnp.where(lane < 9, lane - 6, lane - 9)))
    cv = jnp.where(piece == 0, coefs[0], jnp.where(piece == 1, coefs[1], coefs[2]))
    hi = (pos - (pos & (POS_SPLIT - 1))).astype(F32)
    lo = (pos & (POS_SPLIT - 1)).astype(F32)
    n = 2 * COEF_PIECES
    if q_side:
        cols = jnp.where(lane < COEF_PIECES, -hi, jnp.where(lane < n, -lo, cv))
    else:
        cols = jnp.where(lane < n, cv, jnp.where(lane < n + COEF_PIECES, hi, lo))
    return jnp.where(lane < 2 * n, cols, 0.0)


def _diff_attn_kernel(coef_ref, lam_ref, g_ref, q_ref, k_ref, v_ref, o_ref,
                      kaug_sc, qaug_sc, corr_sc, sa_sc, sb_sc, pa_sc, pb_sc, m_sc, l_sc,
                      alphaa_sc, alphab_sc, acc_sc, *, tq, nt, rb, seq, lambda_init):
    h = pl.program_id(1)
    qt = pl.program_id(2)
    qis = [nt * qt + u for u in range(nt)]
    dh = DIFF_HEAD_DIM
    n_chunks = seq // tq
    coefs = [coef_ref[h, t] for t in range(COEF_PIECES)]
    row_iota = lax.broadcasted_iota(jnp.int32, (tq, dh), 0)

    @pl.when(qt == 0)
    def _():
        def build(t, carry):
            r0 = pl.multiple_of(t * tq, tq)
            ak = _alibi_columns(row_iota + r0, coefs, q_side=False).astype(BF16)
            for c in range(2):
                kaug_sc[c, pl.ds(r0, tq), 0:dh] = k_ref[pl.ds(r0, tq), c * dh:(c + 1) * dh]
                kaug_sc[c, pl.ds(r0, tq), dh:2 * dh] = ak
            return carry

        lax.fori_loop(0, n_chunks, build, 0)
        row = lax.broadcasted_iota(jnp.int32, (tq, tq), 0)
        col = lax.broadcasted_iota(jnp.int32, (tq, tq), 1)
        slope2 = coefs[0] + coefs[1] + coefs[2]
        corr_sc[0] = jnp.zeros((tq, tq), F32)
        corr_sc[1] = (2.0 * slope2) * jnp.minimum(row - col, 0).astype(F32)

    for u, qi in enumerate(qis):
        aq = _alibi_columns(row_iota + qi * tq, coefs, q_side=True)
        for c in range(2):
            q_c = q_ref[u * tq:(u + 1) * tq, c * dh:(c + 1) * dh]
            qaug_sc[u, 0, c, :, 0:dh] = q_c
            qaug_sc[u, 0, c, :, dh:2 * dh] = aq.astype(BF16)
            qaug_sc[u, 1, c, :, 0:dh] = q_c
            qaug_sc[u, 1, c, :, dh:2 * dh] = (-aq).astype(BF16)

    m_sc[...] = jnp.full(m_sc.shape, -jnp.inf, F32)
    l_sc[...] = jnp.zeros(l_sc.shape, F32)
    acc_sc[...] = jnp.zeros(acc_sc.shape, F32)

    n_slabs = tq // V7X_LANES
    last = n_chunks - 1

    def scores(kc, s_ref, u):
        ks = pl.multiple_of(kc * tq, tq)
        side = jnp.where(kc > qis[u], 1, 0)
        for c in range(2):
            s_ref[u, c] = lax.dot_general(
                qaug_sc[u, side, c], kaug_sc[c, pl.ds(ks, tq), :],
                (((1,), (1,)), ((), ())), preferred_element_type=F32)

    def softmax(kc, s_ref, p_ref, alpha_ref, near_diag, u):
        on_diag = jnp.where(kc == qis[u], 1, 0)
        for c in range(2):
            for b in range(tq // rb):
                rows = slice(b * rb, (b + 1) * rb)
                sb = s_ref[u, c, rows, :]
                if near_diag:
                    sb = sb + corr_sc[on_diag, rows, :]
                slabs = [sb[:, t * V7X_LANES:(t + 1) * V7X_LANES] for t in range(n_slabs)]
                mx = functools.reduce(jnp.maximum, slabs)
                m_old = m_sc[u, c, rows, :]
                m_new = jnp.maximum(m_old, jnp.broadcast_to(
                    jnp.max(mx, axis=-1, keepdims=True), (rb, V7X_LANES)))
                a = jnp.exp2(m_old - m_new)
                ps = [jnp.exp2(sl - m_new) for sl in slabs]
                l_sc[u, c, rows, :] = a * l_sc[u, c, rows, :] + functools.reduce(jnp.add, ps)
                p_ref[u, c, rows, :] = jnp.concatenate(ps, axis=-1).astype(BF16)
                alpha_ref[u, c, rows, :] = a
                m_sc[u, c, rows, :] = m_new

    def accumulate(kc, p_ref, alpha_ref, u):
        ks = pl.multiple_of(kc * tq, tq)
        for c in range(2):
            alpha = alpha_ref[u, c]
            acc_sc[u, c] = jnp.concatenate([alpha, alpha], axis=-1) * acc_sc[u, c] + jnp.dot(
                p_ref[u, c], v_ref[pl.ds(ks, tq), :], preferred_element_type=F32)

    tiles = range(nt)

    def pair_body(kc, near_diag, first=False, final=False):
        for u in tiles:
            if not first:
                accumulate(kc - 1, pb_sc, alphab_sc, u)
            scores(kc + 1, sb_sc, u)
        for u in tiles:
            softmax(kc, sa_sc, pa_sc, alphaa_sc, near_diag, u)
        for u in tiles:
            accumulate(kc, pa_sc, alphaa_sc, u)
            if not final:
                scores(kc + 2, sa_sc, u)
        for u in tiles:
            softmax(kc + 1, sb_sc, pb_sc, alphab_sc, near_diag, u)

    def pair(t, first=False, final=False):
        lax.cond(t == qt,
                 functools.partial(pair_body, 2 * t, True, first, final),
                 functools.partial(pair_body, 2 * t, False, first, final))

    def middle_pair(t, carry):
        pair(t)
        return carry

    n_pairs = n_chunks // 2
    for u in tiles:
        scores(0, sa_sc, u)
    pair(0, first=True)
    lax.fori_loop(1, n_pairs - 1, middle_pair, 0)
    pair(n_pairs - 1, final=True)
    for u in tiles:
        accumulate(last, pb_sc, alphab_sc, u)

    lam_rows = lam_ref[...]
    lam = (jnp.exp(jnp.sum(lam_rows[0:1] * lam_rows[1:2], axis=-1, keepdims=True))
           - jnp.exp(jnp.sum(lam_rows[2:3] * lam_rows[3:4], axis=-1, keepdims=True))
           + lambda_init)
    for u in tiles:
        l0 = jnp.sum(l_sc[u, 0], axis=-1, keepdims=True)
        l1 = jnp.sum(l_sc[u, 1], axis=-1, keepdims=True)
        o = acc_sc[u, 0] / l0 - lam * (acc_sc[u, 1] / l1)
        ms = jnp.mean(o * o, axis=-1, keepdims=True)
        o = o * lax.rsqrt(ms + SUBLN_EPS) * g_ref[...] * (1.0 - lambda_init)
        o_ref[u * tq:(u + 1) * tq, :] = o.astype(o_ref.dtype)


def _diff_attention(z, coefs, lam_rows, subln_g, *, batch, seq, lambda_init, tq=512, nt=2, rb=32):
    assert nt == 2
    T = z.shape[0]
    e = 2 * DIFF_HEAD_DIM
    nq = seq // (nt * tq)
    qspec = pl.BlockSpec((nt * tq, e), lambda b, h, i: (b * nq + i, Z_DQ // e + h))
    kspec = pl.BlockSpec((seq, e), lambda b, h, i: (b, Z_DK // e + h))
    vspec = pl.BlockSpec((seq, e), lambda b, h, i: (b, Z_DV // e + h))
    return pl.pallas_call(
        functools.partial(_diff_attn_kernel, tq=tq, nt=nt, rb=rb, seq=seq,
                          lambda_init=lambda_init),
        grid=(batch, DIFF_HEADS, nq),
        in_specs=[
            pl.BlockSpec(memory_space=pltpu.SMEM),
            pl.BlockSpec((4, DIFF_HEAD_DIM), lambda b, h, i: (0, 0)),
            pl.BlockSpec((1, e), lambda b, h, i: (0, 0)),
            qspec, kspec, vspec,
        ],
        out_specs=pl.BlockSpec((nt * tq, e), lambda b, h, i: (b * nq + i, h)),
        out_shape=jax.ShapeDtypeStruct((T, DIFF_V_W), BF16),
        scratch_shapes=[
            pltpu.VMEM((2, seq, e), BF16),
            pltpu.VMEM((nt, 2, 2, tq, e), BF16),
            pltpu.VMEM((2, tq, tq), F32),
            pltpu.VMEM((nt, 2, tq, tq), F32),
            pltpu.VMEM((nt, 2, tq, tq), F32),
            pltpu.VMEM((nt, 2, tq, tq), BF16),
            pltpu.VMEM((nt, 2, tq, tq), BF16),
            pltpu.VMEM((nt, 2, tq, V7X_LANES), F32),
            pltpu.VMEM((nt, 2, tq, V7X_LANES), F32),
            pltpu.VMEM((nt, 2, tq, V7X_LANES), F32),
            pltpu.VMEM((nt, 2, tq, V7X_LANES), F32),
            pltpu.VMEM((nt, 2, tq, e), F32),
        ],
        compiler_params=_params(("parallel", "parallel", "arbitrary")),
        name="diff_attn",
    )(coefs, lam_rows, subln_g.reshape(1, e), z, z, z)


def _dil_attn_kernel(slopes_ref, q_ref, k_ref, v_ref, o_ref, lse_ref, *, seq, dil, tq, group):
    h = pl.program_id(1)
    slope = slopes_ref[h]
    w = DIL_HALF_KEYS
    nk = tq + 2 * w
    length = seq // dil
    blocks_per_class = length // tq
    scale = DIL_HEAD_DIM ** -0.5
    row = lax.broadcasted_iota(jnp.int32, (tq, nk), 0)
    col = lax.broadcasted_iota(jnp.int32, (tq, nk), 1)
    col_minus_row = col - row

    def block_offsets(i):
        cls = i // blocks_per_class
        qs_local = (i % blocks_per_class) * tq
        ks_local = jnp.clip(qs_local - w, 0, length - nk)
        return cls, qs_local, ks_local

    def body(it, carry):
        blocks = [block_offsets(it * group + g) for g in range(group)]
        scores = []
        for cls, qs_local, ks_local in blocks:
            q = q_ref[pl.ds(pl.multiple_of(cls * length + qs_local, tq), tq), :]
            k = k_ref[pl.ds(pl.multiple_of(cls * length + ks_local, w), nk), :]
            scores.append(lax.dot_general(q, k, (((1,), (1,)), ((), ())),
                                          preferred_element_type=F32))
        for (cls, qs_local, ks_local), s in zip(blocks, scores):
            rel = jnp.abs(col_minus_row + (ks_local - qs_local))
            s = jnp.where(rel <= w, s * scale - slope * rel.astype(F32), NEG_INF)
            m = jnp.max(s, axis=-1, keepdims=True)
            e = jnp.exp(s - m)
            z = jnp.sum(e, axis=-1, keepdims=True)
            v = v_ref[pl.ds(pl.multiple_of(cls * length + ks_local, w), nk), :]
            o = jnp.dot(e.astype(BF16), v, preferred_element_type=F32) / z
            if dil == 1:
                rows = pl.ds(pl.multiple_of(qs_local, tq), tq)
            else:
                rows = pl.ds(qs_local * dil + cls, tq, stride=dil)
            o_ref[rows, :] = o
            lse_ref[rows, :] = jnp.broadcast_to(m + jnp.log(z), (tq, DIL_HEAD_DIM))
        return carry

    lax.fori_loop(0, seq // (tq * group), body, 0)


def _dilated_attention(qkv, offsets, slopes_g, *, dil, batch, seq, tq=128, group=16):
    T = qkv.shape[0]
    hd = DIL_HEAD_DIM

    def in_spec(off):
        return pl.BlockSpec((seq, hd), lambda b, h: (b, off // hd + h))

    out_spec = pl.BlockSpec((seq, hd), lambda b, h: (b, h))
    out_sds = jax.ShapeDtypeStruct((T, DIL_OUT_W), F32)
    return pl.pallas_call(
        functools.partial(_dil_attn_kernel, seq=seq, dil=dil, tq=tq, group=group),
        grid=(batch, DIL_GROUP_HEADS),
        in_specs=[pl.BlockSpec(memory_space=pltpu.SMEM)] + [in_spec(o) for o in offsets],
        out_specs=[out_spec, out_spec],
        out_shape=[out_sds, out_sds],
        compiler_params=_params(("parallel", "parallel")),
        name=f"dil_attn_{dil}",
    )(slopes_g, qkv, qkv, qkv)


def _dil_mix_kernel(o0_ref, o1_ref, o2_ref, l0_ref, l1_ref, l2_ref, ob_ref):
    l0, l1, l2 = l0_ref[...], l1_ref[...], l2_ref[...]
    m = jnp.maximum(jnp.maximum(l0, l1), l2)
    w0, w1, w2 = jnp.exp(l0 - m), jnp.exp(l1 - m), jnp.exp(l2 - m)
    ob = (w0 * o0_ref[...] + w1 * o1_ref[...] + w2 * o2_ref[...]) / (w0 + w1 + w2)
    ob_ref[...] = ob.astype(ob_ref.dtype)


def _dil_mix(dil_outs, dil_lses, *, tm=512):
    T, W = dil_outs[0].shape
    row = pl.BlockSpec((tm, W), lambda i: (i, 0))
    return pl.pallas_call(
        _dil_mix_kernel,
        grid=(T // tm,),
        in_specs=[row] * 6,
        out_specs=row,
        out_shape=jax.ShapeDtypeStruct((T, W), BF16),
        compiler_params=_params(("parallel",)),
        name="dil_mix",
    )(*dil_outs, *dil_lses)


def _merge_kernel(oa_ref, ob_ref, ga_ref, gb_ref, wa_ref, wb_ref, y_ref):
    ya = jnp.dot(oa_ref[...], wa_ref[...].astype(BF16), preferred_element_type=F32)
    yb = jnp.dot(ob_ref[...], wb_ref[...].astype(BF16), preferred_element_type=F32)
    y = (jax.nn.sigmoid(ga_ref[...].astype(F32)) * ya
         + jax.nn.sigmoid(gb_ref[...].astype(F32)) * yb)
    y_ref[...] = y.astype(y_ref.dtype)


def _merge(oa, ob, z, wa, wb, *, tm=1024, tn=512):
    T = oa.shape[0]
    return pl.pallas_call(
        _merge_kernel,
        grid=(T // tm, D_MODEL // tn),
        in_specs=[pl.BlockSpec((tm, DIFF_V_W), lambda i, j: (i, 0)),
                  pl.BlockSpec((tm, DIL_OUT_W), lambda i, j: (i, 0)),
                  pl.BlockSpec((tm, tn), lambda i, j: (i, Z_GA // tn + j)),
                  pl.BlockSpec((tm, tn), lambda i, j: (i, Z_GB // tn + j)),
                  pl.BlockSpec((DIFF_V_W, tn), lambda i, j: (0, j)),
                  pl.BlockSpec((DIL_OUT_W, tn), lambda i, j: (0, j))],
        out_specs=pl.BlockSpec((tm, tn), lambda i, j: (i, j)),
        out_shape=jax.ShapeDtypeStruct((T, D_MODEL), BF16),
        compiler_params=_params(("parallel", "parallel")),
        name="merge",
    )(oa, ob, z, z, wa, wb)


def _ple_kernel(x_ref, p_ref, wg_ref, wp_ref, h_ref, o_ref, *, alpha):
    gate = jnp.dot(x_ref[...], wg_ref[...].astype(BF16), preferred_element_type=F32)
    proj = jnp.dot(p_ref[...].astype(BF16), wp_ref[...].astype(BF16),
                   preferred_element_type=F32)
    o_ref[...] = alpha * h_ref[...] + jax.nn.sigmoid(gate) * proj


def _ple(x_bf, p, wg, wp, h, *, alpha, tm=1024, tn=512):
    T, D = x_bf.shape
    tile = pl.BlockSpec((tm, tn), lambda i, j: (i, j))
    return pl.pallas_call(
        functools.partial(_ple_kernel, alpha=alpha),
        grid=(T // tm, D // tn),
        in_specs=[
            pl.BlockSpec((tm, D), lambda i, j: (i, 0)),
            pl.BlockSpec((tm, PLE_DIM), lambda i, j: (i, 0)),
            pl.BlockSpec((D, tn), lambda i, j: (0, j)),
            pl.BlockSpec((PLE_DIM, tn), lambda i, j: (0, j)),
            tile,
        ],
        out_specs=tile,
        out_shape=jax.ShapeDtypeStruct((T, D), F32),
        compiler_params=_params(("parallel", "parallel")),
        name="ple",
    )(x_bf, p, wg, wp, h)


def _alibi_slopes(n):
    return jnp.exp2(-8.0 * jnp.arange(1, n + 1, dtype=F32) / n)


def _bf16_pieces(v):
    pieces, rest = [], v
    for _ in range(COEF_PIECES):
        piece = rest.astype(BF16).astype(F32)
        pieces.append(piece)
        rest = rest - piece
    return jnp.stack(pieces, axis=-1)


def kernel(x, p, ffn1_w_in, ffn1_w_out, ln1_g, ln1_b, w_in, lam_q1, lam_k1, lam_q2, lam_k2,
           subln_g, w_branch_diff, w_branch_dil, w_mix_out, ln2_g, ln2_b, ffn2_w_in,
           ffn2_w_out, ln3_g, ln3_b, w_ple_gate, w_ple_proj, ln4_g, ln4_b):
    batch, seq, d = x.shape
    depth = ffn1_w_in.shape[0]
    T = batch * seq
    alpha = (2 * depth) ** 0.25
    n_pat = len(DIL_PATTERNS)
    tn = 512

    h = x.reshape(T, d)
    h_bf = h.astype(BF16)
    diff_coefs = _bf16_pieces(_alibi_slopes(DIFF_HEADS) * LOG2E)
    dil_slopes = _alibi_slopes(n_pat * DIL_GROUP_HEADS).reshape(DIL_GROUP_HEADS, n_pat)
    colscale = jnp.ones((Z_W,), F32).at[Z_DQ:Z_DK].set(DIFF_HEAD_DIM ** -0.5 * LOG2E)

    for i in range(depth):
        lambda_init = 0.8 - 0.6 * math.exp(-0.3 * i)

        ffn1_w_in_bf, ffn1_w_out_bf = ffn1_w_in[i].astype(BF16), ffn1_w_out[i].astype(BF16)
        f = _ffn(h_bf, ffn1_w_in_bf, ffn1_w_out_bf)
        dils = tuple(dil for _window, dil in DIL_PATTERNS if dil > 1)
        h, h_bf, h_cm = _res_ln_class_major(h, f, ln1_g[i], ln1_b[i], h_bf, ffn1_w_in_bf,
                                            ffn1_w_out_bf, alpha=alpha, scale=0.5,
                                            batch=batch, seq=seq, dils=dils)
        h_class_major = dict(zip(dils, h_cm))

        z, ffn2_w_in_bf, ffn2_w_out_bf = _matmul(
            h_bf, w_in[i], BF16, n_out=Z_W, wcol=_main_wcol(tn), colscale=colscale, tn=tn,
            riders=((ffn2_w_in[i], (d, V7X_LANES), 1), (ffn2_w_out[i], (RIDER_ROWS, d), 0)),
            name="in_proj")
        lam_rows = jnp.stack([lam_q1[i], lam_k1[i], lam_q2[i], lam_k2[i]]).astype(F32)
        oa = _diff_attention(z, diff_coefs, lam_rows, subln_g[i].astype(F32),
                             batch=batch, seq=seq, lambda_init=lambda_init)

        dil_outs, dil_lses = [], []
        for g, (_window, dil) in enumerate(DIL_PATTERNS):
            slopes_g = dil_slopes[:, g] * dil
            if dil == 1:
                o_g, lse_g = _dilated_attention(z, (Z_LQ, Z_LK, Z_LV), slopes_g, dil=dil,
                                                batch=batch, seq=seq)
            else:
                qkv = _matmul(h_class_major[dil], w_in[i], BF16,
                              n_out=3 * DIL_OUT_W, wcol=_group_wcol(g, tn), tn=tn,
                              name=f"in_proj_dil{dil}")
                o_g, lse_g = _dilated_attention(qkv, (0, DIL_OUT_W, 2 * DIL_OUT_W), slopes_g,
                                                dil=dil, batch=batch, seq=seq)
            dil_outs.append(o_g)
            dil_lses.append(lse_g)

        y = _merge(oa, _dil_mix(dil_outs, dil_lses), z, w_branch_diff[i], w_branch_dil[i])
        pre = _matmul(y, w_mix_out[i], F32, n_out=d, wcol=lambda j: j, residual=(h, alpha),
                      tn=tn, name="mix_out")
        h, h_bf = _ln_presummed(pre, ln2_g[i], ln2_b[i])

        f = _ffn(h_bf, ffn2_w_in_bf, ffn2_w_out_bf)
        h, h_bf = _res_ln(h, f, ln3_g[i], ln3_b[i], h_bf, ffn2_w_in_bf, ffn2_w_out_bf,
                          alpha=alpha, scale=0.5)

        pre = _ple(h_bf, p[i].reshape(T, PLE_DIM), w_ple_gate[i], w_ple_proj[i], h, alpha=alpha)
        if i + 1 < depth:
            h, h_bf = _ln_presummed(pre, ln4_g[i], ln4_b[i])
        else:
            h = _ln_presummed(pre, ln4_g[i], ln4_b[i], with_bf16=False)

    return h.reshape(batch, seq, d)
```

```python
import functools
import math

import jax
import jax.numpy as jnp
from jax import lax
from jax.experimental import pallas as pl
from jax.experimental.pallas import tpu as pltpu

F32 = jnp.float32
BF16 = jnp.bfloat16

D_MODEL = 4096
PLE_DIM = 256
D_FF = 11008
DIFF_HEADS = 8
DIFF_HEAD_DIM = 128
DIL_PATTERNS = ((128, 1), (512, 4), (2048, 16))
DIL_GROUP_HEADS = 8
DIL_HEAD_DIM = 128
DIL_HALF_KEYS = 64
LN_EPS = 1e-5
SUBLN_EPS = 1e-5
NEG_INF = -1e30

DIFF_QK_W = DIFF_HEADS * 2 * DIFF_HEAD_DIM
DIFF_V_W = DIFF_HEADS * 2 * DIFF_HEAD_DIM
DIL_W = len(DIL_PATTERNS) * DIL_GROUP_HEADS * DIL_HEAD_DIM
DIL_OUT_W = DIL_GROUP_HEADS * DIL_HEAD_DIM
IN_PROJ_W = 2 * DIFF_QK_W + DIFF_V_W + 3 * DIL_W + 2 * D_MODEL

OFF_DQ = 0
OFF_DK = OFF_DQ + DIFF_QK_W
OFF_DV = OFF_DK + DIFF_QK_W
OFF_LQ = OFF_DV + DIFF_V_W
OFF_LK = OFF_LQ + DIL_W
OFF_LV = OFF_LK + DIL_W
OFF_GA = OFF_LV + DIL_W
OFF_GB = OFF_GA + D_MODEL

Z_DQ = 0
Z_DK = Z_DQ + DIFF_QK_W
Z_DV = Z_DK + DIFF_QK_W
Z_LQ = Z_DV + DIFF_V_W
Z_LK = Z_LQ + DIL_OUT_W
Z_LV = Z_LK + DIL_OUT_W
Z_GA = Z_LV + DIL_OUT_W
Z_GB = Z_GA + D_MODEL
Z_W = Z_GB + D_MODEL

V7X_LANES = 128
V7X_VMEM_BYTES = 64 * 1024 * 1024
V7X_VMEM_REQUEST = 56 * 1024 * 1024
V7X_VMEM_REQUEST_LN = 60 * 1024 * 1024

FFN_BLOCK = 256

LOG2E = math.log2(math.e)
COEF_PIECES = 3
POS_SPLIT = 64
RIDER_ROWS = 64


def _params(semantics, vmem_bytes=V7X_VMEM_REQUEST):
    return pltpu.CompilerParams(dimension_semantics=semantics, vmem_limit_bytes=vmem_bytes)


def _ffn_kernel(x_ref, wga_ref, wgb_ref, wua_ref, wub_ref, woa_ref, wob_ref, o_ref):
    j = pl.program_id(1)

    @pl.when(j == 0)
    def _():
        o_ref[...] = jnp.zeros(o_ref.shape, F32)

    x = x_ref[...]
    ha = _swiglu_hidden(x, wga_ref, wua_ref)
    hb = _swiglu_hidden(x, wgb_ref, wub_ref)
    part = jnp.dot(ha, woa_ref[...], preferred_element_type=F32)
    part += jnp.dot(hb, wob_ref[...], preferred_element_type=F32)
    o_ref[...] += part


def _swiglu_hidden(x, wg_ref, wu_ref):
    g = jnp.dot(x, wg_ref[...], preferred_element_type=F32)
    u = jnp.dot(x, wu_ref[...], preferred_element_type=F32)
    return (g * jax.nn.sigmoid(g) * u).astype(BF16)


def _ffn(x_bf, w_in_bf, w_out_bf, *, tm=512, tb=FFN_BLOCK):
    T, D = x_bf.shape
    n_blocks = w_out_bf.shape[0] // tb
    steps = n_blocks // 2

    def first(j):
        return 2 * j

    def second(j):
        return 2 * j + 1

    def col_spec(blk, off):
        return pl.BlockSpec((D, tb), lambda i, j: (0, off + blk(j)))

    def row_spec(blk):
        return pl.BlockSpec((tb, D), lambda i, j: (blk(j), 0))

    return pl.pallas_call(
        _ffn_kernel,
        grid=(T // tm, steps),
        in_specs=[
            pl.BlockSpec((tm, D), lambda i, j: (i, 0)),
            col_spec(first, 0), col_spec(second, 0),
            col_spec(first, n_blocks), col_spec(second, n_blocks),
            row_spec(first), row_spec(second),
        ],
        out_specs=pl.BlockSpec((tm, D), lambda i, j: (i, 0)),
        out_shape=jax.ShapeDtypeStruct((T, D), F32),
        compiler_params=_params(("parallel", "arbitrary")),
        name="ffn",
    )(x_bf, w_in_bf, w_in_bf, w_in_bf, w_in_bf, w_out_bf, w_out_bf)


def _layer_norm_rows(y, g_ref, b_ref):
    mu = jnp.mean(y, axis=-1, keepdims=True)
    yc = y - mu
    var = jnp.mean(yc * yc, axis=-1, keepdims=True)
    return yc * lax.rsqrt(var + LN_EPS) * g_ref[...] + b_ref[...]


def _ffn_tail_specs(w_in_bf, w_out_bf, tm, tb=FFN_BLOCK):
    D = w_in_bf.shape[0]
    n_blocks = w_out_bf.shape[0] // tb
    last = n_blocks - 1
    once = pl.Buffered(1)
    specs = [pl.BlockSpec((tm, D), lambda i: (i, 0)),
             pl.BlockSpec((D, tb), lambda i: (0, last), pipeline_mode=once),
             pl.BlockSpec((D, tb), lambda i: (0, n_blocks + last), pipeline_mode=once),
             pl.BlockSpec((tb, D), lambda i: (last, 0), pipeline_mode=once)]
    return specs, [w_in_bf, w_in_bf, w_out_bf]


def _ffn_ln_rows(h_ref, br_ref, g_ref, b_ref, x_ref, wg_ref, wu_ref, wo_ref, alpha, scale):
    tail = jnp.dot(_swiglu_hidden(x_ref[...], wg_ref, wu_ref), wo_ref[...],
                   preferred_element_type=F32)
    return _layer_norm_rows(alpha * h_ref[...] + scale * (br_ref[...] + tail), g_ref, b_ref)


def _ln_kernel(h_ref, br_ref, g_ref, b_ref, x_ref, wg_ref, wu_ref, wo_ref, o_ref, obf_ref,
               *, alpha, scale):
    out = _ffn_ln_rows(h_ref, br_ref, g_ref, b_ref, x_ref, wg_ref, wu_ref, wo_ref, alpha, scale)
    o_ref[...] = out
    obf_ref[...] = out.astype(BF16)


def _ln_class_major_kernel(h_ref, br_ref, g_ref, b_ref, x_ref, wg_ref, wu_ref, wo_ref,
                           o_ref, obf_ref, *rest, alpha, scale, dils):
    cm_refs, slab_sc = rest[:-1], rest[-1]
    out = _ffn_ln_rows(h_ref, br_ref, g_ref, b_ref, x_ref, wg_ref, wu_ref, wo_ref, alpha, scale)
    o_ref[...] = out
    obf_ref[...] = out.astype(BF16)
    tm, d = out.shape
    for s in range(d // V7X_LANES):
        slab_sc[s] = out[:, s * V7X_LANES:(s + 1) * V7X_LANES]
    for dil, cm_ref in zip(dils, cm_refs):
        for r in range(dil):
            for s in range(d // V7X_LANES):
                cm_ref[r, :, s * V7X_LANES:(s + 1) * V7X_LANES] = slab_sc[
                    s, pl.ds(r, tm // dil, stride=dil), :].astype(BF16)


def _res_ln_class_major(h, branch, g, b, ffn_in, w_in_bf, w_out_bf, *, alpha, scale, batch, seq,
                        dils, tm=256):
    T, D = h.shape
    tiles = seq // tm
    row = pl.BlockSpec((tm, D), lambda i: (i, 0))
    vec = pl.BlockSpec((1, D), lambda i: (0, 0))
    tail_specs, tail_args = _ffn_tail_specs(w_in_bf, w_out_bf, tm)
    cm_specs = [pl.BlockSpec((None, dil, tm // dil, D), lambda i: (i // tiles, 0, i % tiles, 0))
                for dil in dils]
    cm_shapes = [jax.ShapeDtypeStruct((batch, dil, seq // dil, D), BF16) for dil in dils]
    outs = pl.pallas_call(
        functools.partial(_ln_class_major_kernel, alpha=alpha, scale=scale, dils=dils),
        grid=(T // tm,),
        in_specs=[row, row, vec, vec] + tail_specs,
        out_specs=[row, row] + cm_specs,
        out_shape=[jax.ShapeDtypeStruct((T, D), F32), jax.ShapeDtypeStruct((T, D), BF16)]
        + cm_shapes,
        scratch_shapes=[pltpu.VMEM((D // V7X_LANES, tm, V7X_LANES), F32)],
        compiler_params=_params(("parallel",), V7X_VMEM_REQUEST_LN),
        name="res_ln_class_major",
    )(h, branch, g.reshape(1, D), b.reshape(1, D), ffn_in, *tail_args)
    return outs[0], outs[1], [cm.reshape(T, D) for cm in outs[2:]]


def _ln_presummed_kernel(y_ref, g_ref, b_ref, *o_refs):
    out = _layer_norm_rows(y_ref[...], g_ref, b_ref)
    for o_ref in o_refs:
        o_ref[...] = out.astype(o_ref.dtype)


def _ln_presummed(y, g, b, *, with_bf16=True, tm=256):
    T, D = y.shape
    row = pl.BlockSpec((tm, D), lambda i: (i, 0))
    vec = pl.BlockSpec((1, D), lambda i: (0, 0))
    f32_out = jax.ShapeDtypeStruct((T, D), F32)
    return pl.pallas_call(
        _ln_presummed_kernel,
        grid=(T // tm,),
        in_specs=[row, vec, vec],
        out_specs=[row, row] if with_bf16 else row,
        out_shape=[f32_out, jax.ShapeDtypeStruct((T, D), BF16)] if with_bf16 else f32_out,
        compiler_params=_params(("parallel",)),
        name="ln",
    )(y, g.reshape(1, D), b.reshape(1, D))


def _res_ln(h, branch, g, b, ffn_in, w_in_bf, w_out_bf, *, alpha, scale, tm=256):
    T, D = h.shape
    row = pl.BlockSpec((tm, D), lambda i: (i, 0))
    vec = pl.BlockSpec((1, D), lambda i: (0, 0))
    tail_specs, tail_args = _ffn_tail_specs(w_in_bf, w_out_bf, tm)
    return pl.pallas_call(
        functools.partial(_ln_kernel, alpha=alpha, scale=scale),
        grid=(T // tm,),
        in_specs=[row, row, vec, vec] + tail_specs,
        out_specs=[row, row],
        out_shape=[jax.ShapeDtypeStruct((T, D), F32), jax.ShapeDtypeStruct((T, D), BF16)],
        compiler_params=_params(("parallel",)),
        name="res_ln",
    )(h, branch, g.reshape(1, D), b.reshape(1, D), ffn_in, *tail_args)


def _matmul_kernel(*refs, scaled, residual_alpha, rider_blocks, n_col_tiles):
    refs = list(refs)
    n_riders = len(rider_blocks)
    x_ref, w_ref = refs[:2]
    del refs[:2]
    s_ref = refs.pop(0) if scaled else None
    h_ref = refs.pop(0) if residual_alpha is not None else None
    rider_in, o_ref, rider_out = refs[:n_riders], refs[n_riders], refs[n_riders + 1:]
    acc = jnp.dot(x_ref[...], w_ref[...].astype(BF16), preferred_element_type=F32)
    if scaled:
        acc = acc * s_ref[...]
    if residual_alpha is not None:
        acc = residual_alpha * h_ref[...] + acc
    o_ref[...] = acc.astype(o_ref.dtype)
    step = pl.program_id(0) * n_col_tiles + pl.program_id(1)
    for src, dst, n_blocks in zip(rider_in, rider_out, rider_blocks):
        @pl.when(step < n_blocks)
        def _(src=src, dst=dst):
            dst[...] = src[...].astype(BF16)


def _matmul(x_bf, w, out_dtype, *, n_out, wcol, colscale=None, residual=None, riders=(),
            tm=1024, tn=512, name="matmul"):
    T, K = x_bf.shape
    n_col_tiles = n_out // tn
    tile = pl.BlockSpec((tm, tn), lambda i, j: (i, j))
    in_specs = [
        pl.BlockSpec((tm, K), lambda i, j: (i, 0)),
        pl.BlockSpec((K, tn), lambda i, j: (0, wcol(j))),
    ]
    args = [x_bf, w]
    if colscale is not None:
        in_specs.append(pl.BlockSpec((1, tn), lambda i, j: (0, j)))
        args.append(colscale.reshape(1, n_out))
    if residual is not None:
        in_specs.append(tile)
        args.append(residual[0])
    rider_specs, rider_blocks = [], []
    for arr, block, axis in riders:
        n_blocks = arr.shape[axis] // block[axis]
        assert n_blocks <= (T // tm) * n_col_tiles

        def index_map(i, j, axis=axis, n_blocks=n_blocks):
            blk = jnp.minimum(i * n_col_tiles + j, n_blocks - 1)
            return tuple(blk if a == axis else 0 for a in range(2))

        rider_specs.append(pl.BlockSpec(block, index_map))
        rider_blocks.append(n_blocks)
        args.append(arr)
    outs = pl.pallas_call(
        functools.partial(_matmul_kernel, scaled=colscale is not None,
                          residual_alpha=None if residual is None else residual[1],
                          rider_blocks=tuple(rider_blocks), n_col_tiles=n_col_tiles),
        grid=(T // tm, n_col_tiles),
        in_specs=in_specs + rider_specs,
        out_specs=[tile] + rider_specs,
        out_shape=[jax.ShapeDtypeStruct((T, n_out), out_dtype)]
        + [jax.ShapeDtypeStruct(arr.shape, BF16) for arr, _, _ in riders],
        compiler_params=_params(("arbitrary", "arbitrary")),
        name=name,
    )(*args)
    return outs if riders else outs[0]


def _main_wcol(tn):
    per_sec = DIL_OUT_W // tn
    first = OFF_LQ // tn
    n_pat = len(DIL_PATTERNS)

    def wcol(j):
        k = j - first
        dil_tile = first + (k // per_sec) * (n_pat * per_sec) + k % per_sec
        gate_tile = j + (n_pat - 1) * 3 * per_sec
        return jnp.where(j < first, j, jnp.where(k < 3 * per_sec, dil_tile, gate_tile))

    return wcol


def _group_wcol(group, tn):
    per_sec = DIL_OUT_W // tn

    def wcol(j):
        return (OFF_LQ + group * DIL_OUT_W) // tn + (j // per_sec) * (DIL_W // tn) + j % per_sec

    return wcol


def _alibi_columns(pos, coefs, q_side):
    lane = lax.broadcasted_iota(jnp.int32, pos.shape, 1)
    piece = jnp.where(lane < 3, lane, jnp.where(lane < 6, lane - 3,
                                                jnp.where(lane < 9, lane - 6, lane - 9)))
    cv = jnp.where(piece == 0, coefs[0], jnp.where(piece == 1, coefs[1], coefs[2]))
    hi = (pos - (pos & (POS_SPLIT - 1))).astype(F32)
    lo = (pos & (POS_SPLIT - 1)).astype(F32)
    n = 2 * COEF_PIECES
    if q_side:
        cols = jnp.where(lane < COEF_PIECES, -hi, jnp.where(lane < n, -lo, cv))
    else:
        cols = jnp.where(lane < n, cv, jnp.where(lane < n + COEF_PIECES, hi, lo))
    return jnp.where(lane < 2 * n, cols, 0.0)


def _diff_attn_kernel(coef_ref, lam_ref, g_ref, q_ref, k_ref, v_ref, o_ref,
                      kaug_sc, qaug_sc, corr_sc, sa_sc, sb_sc, pa_sc, pb_sc, m_sc, l_sc,
                      alphaa_sc, alphab_sc, acc_sc, *, tq, nt, rb, seq, lambda_init):
    h = pl.program_id(1)
    qt = pl.program_id(2)
    qis = [nt * qt + u for u in range(nt)]
    dh = DIFF_HEAD_DIM
    n_chunks = seq // tq
    coefs = [coef_ref[h, t] for t in range(COEF_PIECES)]
    row_iota = lax.broadcasted_iota(jnp.int32, (tq, dh), 0)

    @pl.when(qt == 0)
    def _():
        def build(t, carry):
            r0 = pl.multiple_of(t * tq, tq)
            ak = _alibi_columns(row_iota + r0, coefs, q_side=False).astype(BF16)
            for c in range(2):
                kaug_sc[c, pl.ds(r0, tq), 0:dh] = k_ref[pl.ds(r0, tq), c * dh:(c + 1) * dh]
                kaug_sc[c, pl.ds(r0, tq), dh:2 * dh] = ak
            return carry

        lax.fori_loop(0, n_chunks, build, 0)
        row = lax.broadcasted_iota(jnp.int32, (tq, tq), 0)
        col = lax.broadcasted_iota(jnp.int32, (tq, tq), 1)
        slope2 = coefs[0] + coefs[1] + coefs[2]
        corr_sc[0] = jnp.zeros((tq, tq), F32)
        corr_sc[1] = (2.0 * slope2) * jnp.minimum(row - col, 0).astype(F32)

    for u, qi in enumerate(qis):
        aq = _alibi_columns(row_iota + qi * tq, coefs, q_side=True)
        for c in range(2):
            q_c = q_ref[u * tq:(u + 1) * tq, c * dh:(c + 1) * dh]
            qaug_sc[u, 0, c, :, 0:dh] = q_c
            qaug_sc[u, 0, c, :, dh:2 * dh] = aq.astype(BF16)
            qaug_sc[u, 1, c, :, 0:dh] = q_c
            qaug_sc[u, 1, c, :, dh:2 * dh] = (-aq).astype(BF16)

    n_slabs = tq // V7X_LANES
    last = n_chunks - 1

    def scores(kc, s_ref, u):
        ks = pl.multiple_of(kc * tq, tq)
        side = jnp.where(kc > qis[u], 1, 0)
        for c in range(2):
            s_ref[u, c] = lax.dot_general(
                qaug_sc[u, side, c], kaug_sc[c, pl.ds(ks, tq), :],
                (((1,), (1,)), ((), ())), preferred_element_type=F32)

    def softmax(kc, s_ref, p_ref, alpha_ref, near_diag, u, fresh=False):
        on_diag = jnp.where(kc == qis[u], 1, 0)
        for c in range(2):
            for b in range(tq // rb):
                rows = slice(b * rb, (b + 1) * rb)
                sb = s_ref[u, c, rows, :]
                if near_diag:
                    sb = sb + corr_sc[on_diag, rows, :]
                slabs = [sb[:, t * V7X_LANES:(t + 1) * V7X_LANES] for t in range(n_slabs)]
                mx = functools.reduce(jnp.maximum, slabs)
                m_new = jnp.broadcast_to(jnp.max(mx, axis=-1, keepdims=True), (rb, V7X_LANES))
                if not fresh:
                    m_old = m_sc[u, c, rows, :]
                    m_new = jnp.maximum(m_old, m_new)
                    a = jnp.exp2(m_old - m_new)
                    alpha_ref[u, c, rows, :] = a
                ps = [jnp.exp2(sl - m_new) for sl in slabs]
                psum = functools.reduce(jnp.add, ps)
                l_sc[u, c, rows, :] = psum if fresh else a * l_sc[u, c, rows, :] + psum
                p_ref[u, c, rows, :] = jnp.concatenate(ps, axis=-1).astype(BF16)
                m_sc[u, c, rows, :] = m_new

    def accumulate(kc, p_ref, alpha_ref, u, fresh=False):
        ks = pl.multiple_of(kc * tq, tq)
        for c in range(2):
            pv = jnp.dot(p_ref[u, c], v_ref[pl.ds(ks, tq), :], preferred_element_type=F32)
            if fresh:
                acc_sc[u, c] = pv
            else:
                alpha = alpha_ref[u, c]
                acc_sc[u, c] = jnp.concatenate([alpha, alpha], axis=-1) * acc_sc[u, c] + pv

    tiles = range(nt)

    def pair_body(kc, near_diag, first=False, final=False):
        for u in tiles:
            if not first:
                accumulate(kc - 1, pb_sc, alphab_sc, u)
            scores(kc + 1, sb_sc, u)
        for u in tiles:
            softmax(kc, sa_sc, pa_sc, alphaa_sc, near_diag, u, fresh=first)
        for u in tiles:
            accumulate(kc, pa_sc, alphaa_sc, u, fresh=first)
            if not final:
                scores(kc + 2, sa_sc, u)
        for u in tiles:
            softmax(kc + 1, sb_sc, pb_sc, alphab_sc, near_diag, u)

    def pair(t, first=False, final=False):
        lax.cond(t == qt,
                 functools.partial(pair_body, 2 * t, True, first, final),
                 functools.partial(pair_body, 2 * t, False, first, final))

    def middle_pair(t, carry):
        pair(t)
        return carry

    n_pairs = n_chunks // 2
    for u in tiles:
        scores(0, sa_sc, u)
    pair(0, first=True)
    lax.fori_loop(1, n_pairs - 1, middle_pair, 0)
    pair(n_pairs - 1, final=True)
    for u in tiles:
        accumulate(last, pb_sc, alphab_sc, u)

    lam_rows = lam_ref[...]
    lam = (jnp.exp(jnp.sum(lam_rows[0:1] * lam_rows[1:2], axis=-1, keepdims=True))
           - jnp.exp(jnp.sum(lam_rows[2:3] * lam_rows[3:4], axis=-1, keepdims=True))
           + lambda_init)
    for u in tiles:
        l0 = jnp.sum(l_sc[u, 0], axis=-1, keepdims=True)
        l1 = jnp.sum(l_sc[u, 1], axis=-1, keepdims=True)
        o = acc_sc[u, 0] / l0 - lam * (acc_sc[u, 1] / l1)
        ms = jnp.mean(o * o, axis=-1, keepdims=True)
        o = o * lax.rsqrt(ms + SUBLN_EPS) * g_ref[...] * (1.0 - lambda_init)
        o_ref[u * tq:(u + 1) * tq, :] = o.astype(o_ref.dtype)


def _diff_attention(z, coefs, lam_rows, subln_g, *, batch, seq, lambda_init, tq=512, nt=2, rb=32):
    assert nt == 2
    T = z.shape[0]
    e = 2 * DIFF_HEAD_DIM
    nq = seq // (nt * tq)
    qspec = pl.BlockSpec((nt * tq, e), lambda b, h, i: (b * nq + i, Z_DQ // e + h))
    kspec = pl.BlockSpec((seq, e), lambda b, h, i: (b, Z_DK // e + h))
    vspec = pl.BlockSpec((seq, e), lambda b, h, i: (b, Z_DV // e + h))
    return pl.pallas_call(
        functools.partial(_diff_attn_kernel, tq=tq, nt=nt, rb=rb, seq=seq,
                          lambda_init=lambda_init),
        grid=(batch, DIFF_HEADS, nq),
        in_specs=[
            pl.BlockSpec(memory_space=pltpu.SMEM),
            pl.BlockSpec((4, DIFF_HEAD_DIM), lambda b, h, i: (0, 0)),
            pl.BlockSpec((1, e), lambda b, h, i: (0, 0)),
            qspec, kspec, vspec,
        ],
        out_specs=pl.BlockSpec((nt * tq, e), lambda b, h, i: (b * nq + i, h)),
        out_shape=jax.ShapeDtypeStruct((T, DIFF_V_W), BF16),
        scratch_shapes=[
            pltpu.VMEM((2, seq, e), BF16),
            pltpu.VMEM((nt, 2, 2, tq, e), BF16),
            pltpu.VMEM((2, tq, tq), F32),
            pltpu.VMEM((nt, 2, tq, tq), F32),
            pltpu.VMEM((nt, 2, tq, tq), F32),
            pltpu.VMEM((nt, 2, tq, tq), BF16),
            pltpu.VMEM((nt, 2, tq, tq), BF16),
            pltpu.VMEM((nt, 2, tq, V7X_LANES), F32),
            pltpu.VMEM((nt, 2, tq, V7X_LANES), F32),
            pltpu.VMEM((nt, 2, tq, V7X_LANES), F32),
            pltpu.VMEM((nt, 2, tq, V7X_LANES), F32),
            pltpu.VMEM((nt, 2, tq, e), F32),
        ],
        compiler_params=_params(("parallel", "parallel", "arbitrary")),
        name="diff_attn",
    )(coefs, lam_rows, subln_g.reshape(1, e), z, z, z)


def _dil_attn_kernel(slopes_ref, q_ref, k_ref, v_ref, o_ref, lse_ref, *, seq, dil, tq, group):
    h = pl.program_id(1)
    slope = slopes_ref[h]
    w = DIL_HALF_KEYS
    nk = tq + 2 * w
    length = seq // dil
    blocks_per_class = length // tq
    scale = DIL_HEAD_DIM ** -0.5
    row = lax.broadcasted_iota(jnp.int32, (tq, nk), 0)
    col = lax.broadcasted_iota(jnp.int32, (tq, nk), 1)
    col_minus_row = col - row

    def block_offsets(i):
        cls = i // blocks_per_class
        qs_local = (i % blocks_per_class) * tq
        ks_local = jnp.clip(qs_local - w, 0, length - nk)
        return cls, qs_local, ks_local

    def body(it, carry):
        blocks = [block_offsets(it * group + g) for g in range(group)]
        scores = []
        for cls, qs_local, ks_local in blocks:
            q = q_ref[pl.ds(pl.multiple_of(cls * length + qs_local, tq), tq), :]
            k = k_ref[pl.ds(pl.multiple_of(cls * length + ks_local, w), nk), :]
            scores.append(lax.dot_general(q, k, (((1,), (1,)), ((), ())),
                                          preferred_element_type=F32))
        for (cls, qs_local, ks_local), s in zip(blocks, scores):
            rel = jnp.abs(col_minus_row + (ks_local - qs_local))
            s = jnp.where(rel <= w, s * scale - slope * rel.astype(F32), NEG_INF)
            m = jnp.max(s, axis=-1, keepdims=True)
            e = jnp.exp(s - m)
            z = jnp.sum(e, axis=-1, keepdims=True)
            v = v_ref[pl.ds(pl.multiple_of(cls * length + ks_local, w), nk), :]
            o = jnp.dot(e.astype(BF16), v, preferred_element_type=F32) / z
            if dil == 1:
                rows = pl.ds(pl.multiple_of(qs_local, tq), tq)
            else:
                rows = pl.ds(qs_local * dil + cls, tq, stride=dil)
            o_ref[rows, :] = o
            lse_ref[rows, :] = jnp.broadcast_to(m + jnp.log(z), (tq, DIL_HEAD_DIM))
        return carry

    lax.fori_loop(0, seq // (tq * group), body, 0)


def _dilated_attention(qkv, offsets, slopes_g, *, dil, batch, seq, tq=128, group=16):
    T = qkv.shape[0]
    hd = DIL_HEAD_DIM

    def in_spec(off):
        return pl.BlockSpec((seq, hd), lambda b, h: (b, off // hd + h))

    out_spec = pl.BlockSpec((seq, hd), lambda b, h: (b, h))
    out_sds = jax.ShapeDtypeStruct((T, DIL_OUT_W), F32)
    return pl.pallas_call(
        functools.partial(_dil_attn_kernel, seq=seq, dil=dil, tq=tq, group=group),
        grid=(batch, DIL_GROUP_HEADS),
        in_specs=[pl.BlockSpec(memory_space=pltpu.SMEM)] + [in_spec(o) for o in offsets],
        out_specs=[out_spec, out_spec],
        out_shape=[out_sds, out_sds],
        compiler_params=_params(("parallel", "parallel")),
        name=f"dil_attn_{dil}",
    )(slopes_g, qkv, qkv, qkv)


def _dil_mix_kernel(o0_ref, o1_ref, o2_ref, l0_ref, l1_ref, l2_ref, ob_ref):
    l0, l1, l2 = l0_ref[...], l1_ref[...], l2_ref[...]
    m = jnp.maximum(jnp.maximum(l0, l1), l2)
    w0, w1, w2 = jnp.exp(l0 - m), jnp.exp(l1 - m), jnp.exp(l2 - m)
    ob = (w0 * o0_ref[...] + w1 * o1_ref[...] + w2 * o2_ref[...]) / (w0 + w1 + w2)
    ob_ref[...] = ob.astype(ob_ref.dtype)


def _dil_mix(dil_outs, dil_lses, *, tm=512):
    T, W = dil_outs[0].shape
    row = pl.BlockSpec((tm, W), lambda i: (i, 0))
    return pl.pallas_call(
        _dil_mix_kernel,
        grid=(T // tm,),
        in_specs=[row] * 6,
        out_specs=row,
        out_shape=jax.ShapeDtypeStruct((T, W), BF16),
        compiler_params=_params(("parallel",)),
        name="dil_mix",
    )(*dil_outs, *dil_lses)


def _merge_kernel(oa_ref, ob_ref, ga_ref, gb_ref, wa_ref, wb_ref, y_ref):
    ya = jnp.dot(oa_ref[...], wa_ref[...].astype(BF16), preferred_element_type=F32)
    yb = jnp.dot(ob_ref[...], wb_ref[...].astype(BF16), preferred_element_type=F32)
    y = (jax.nn.sigmoid(ga_ref[...].astype(F32)) * ya
         + jax.nn.sigmoid(gb_ref[...].astype(F32)) * yb)
    y_ref[...] = y.astype(y_ref.dtype)


def _merge(oa, ob, z, wa, wb, *, tm=1024, tn=512):
    T = oa.shape[0]
    return pl.pallas_call(
        _merge_kernel,
        grid=(T // tm, D_MODEL // tn),
        in_specs=[pl.BlockSpec((tm, DIFF_V_W), lambda i, j: (i, 0)),
                  pl.BlockSpec((tm, DIL_OUT_W), lambda i, j: (i, 0)),
                  pl.BlockSpec((tm, tn), lambda i, j: (i, Z_GA // tn + j)),
                  pl.BlockSpec((tm, tn), lambda i, j: (i, Z_GB // tn + j)),
                  pl.BlockSpec((DIFF_V_W, tn), lambda i, j: (0, j)),
                  pl.BlockSpec((DIL_OUT_W, tn), lambda i, j: (0, j))],
        out_specs=pl.BlockSpec((tm, tn), lambda i, j: (i, j)),
        out_shape=jax.ShapeDtypeStruct((T, D_MODEL), BF16),
        compiler_params=_params(("parallel", "parallel")),
        name="merge",
    )(oa, ob, z, z, wa, wb)


def _ple_kernel(x_ref, p_ref, wg_ref, wp_ref, h_ref, o_ref, *, alpha):
    gate = jnp.dot(x_ref[...], wg_ref[...].astype(BF16), preferred_element_type=F32)
    proj = jnp.dot(p_ref[...].astype(BF16), wp_ref[...].astype(BF16),
                   preferred_element_type=F32)
    o_ref[...] = alpha * h_ref[...] + jax.nn.sigmoid(gate) * proj


def _ple(x_bf, p, wg, wp, h, *, alpha, tm=1024, tn=512):
    T, D = x_bf.shape
    tile = pl.BlockSpec((tm, tn), lambda i, j: (i, j))
    return pl.pallas_call(
        functools.partial(_ple_kernel, alpha=alpha),
        grid=(T // tm, D // tn),
        in_specs=[
            pl.BlockSpec((tm, D), lambda i, j: (i, 0)),
            pl.BlockSpec((tm, PLE_DIM), lambda i, j: (i, 0)),
            pl.BlockSpec((D, tn), lambda i, j: (0, j)),
            pl.BlockSpec((PLE_DIM, tn), lambda i, j: (0, j)),
            tile,
        ],
        out_specs=tile,
        out_shape=jax.ShapeDtypeStruct((T, D), F32),
        compiler_params=_params(("parallel", "parallel")),
        name="ple",
    )(x_bf, p, wg, wp, h)


def _alibi_slopes(n):
    return jnp.exp2(-8.0 * jnp.arange(1, n + 1, dtype=F32) / n)


def _bf16_pieces(v):
    pieces, rest = [], v
    for _ in range(COEF_PIECES):
        piece = rest.astype(BF16).astype(F32)
        pieces.append(piece)
        rest = rest - piece
    return jnp.stack(pieces, axis=-1)


def kernel(x, p, ffn1_w_in, ffn1_w_out, ln1_g, ln1_b, w_in, lam_q1, lam_k1, lam_q2, lam_k2,
           subln_g, w_branch_diff, w_branch_dil, w_mix_out, ln2_g, ln2_b, ffn2_w_in,
           ffn2_w_out, ln3_g, ln3_b, w_ple_gate, w_ple_proj, ln4_g, ln4_b):
    batch, seq, d = x.shape
    depth = ffn1_w_in.shape[0]
    T = batch * seq
    alpha = (2 * depth) ** 0.25
    n_pat = len(DIL_PATTERNS)
    tn = 512

    h = x.reshape(T, d)
    h_bf = h.astype(BF16)
    diff_coefs = _bf16_pieces(_alibi_slopes(DIFF_HEADS) * LOG2E)
    dil_slopes = _alibi_slopes(n_pat * DIL_GROUP_HEADS).reshape(DIL_GROUP_HEADS, n_pat)
    colscale = jnp.ones((Z_W,), F32).at[Z_DQ:Z_DK].set(DIFF_HEAD_DIM ** -0.5 * LOG2E)

    for i in range(depth):
        lambda_init = 0.8 - 0.6 * math.exp(-0.3 * i)

        ffn1_w_in_bf, ffn1_w_out_bf = ffn1_w_in[i].astype(BF16), ffn1_w_out[i].astype(BF16)
        f = _ffn(h_bf, ffn1_w_in_bf, ffn1_w_out_bf)
        dils = tuple(dil for _window, dil in DIL_PATTERNS if dil > 1)
        h, h_bf, h_cm = _res_ln_class_major(h, f, ln1_g[i], ln1_b[i], h_bf, ffn1_w_in_bf,
                                            ffn1_w_out_bf, alpha=alpha, scale=0.5,
                                            batch=batch, seq=seq, dils=dils)
        h_class_major = dict(zip(dils, h_cm))

        z, ffn2_w_in_bf, ffn2_w_out_bf = _matmul(
            h_bf, w_in[i], BF16, n_out=Z_W, wcol=_main_wcol(tn), colscale=colscale, tn=tn,
            riders=((ffn2_w_in[i], (d, V7X_LANES), 1), (ffn2_w_out[i], (RIDER_ROWS, d), 0)),
            name="in_proj")
        lam_rows = jnp.stack([lam_q1[i], lam_k1[i], lam_q2[i], lam_k2[i]]).astype(F32)
        oa = _diff_attention(z, diff_coefs, lam_rows, subln_g[i].astype(F32),
                             batch=batch, seq=seq, lambda_init=lambda_init)

        dil_outs, dil_lses = [], []
        for g, (_window, dil) in enumerate(DIL_PATTERNS):
            slopes_g = dil_slopes[:, g] * dil
            if dil == 1:
                o_g, lse_g = _dilated_attention(z, (Z_LQ, Z_LK, Z_LV), slopes_g, dil=dil,
                                                batch=batch, seq=seq)
            else:
                qkv = _matmul(h_class_major[dil], w_in[i], BF16,
                              n_out=3 * DIL_OUT_W, wcol=_group_wcol(g, tn), tn=tn,
                              name=f"in_proj_dil{dil}")
                o_g, lse_g = _dilated_attention(qkv, (0, DIL_OUT_W, 2 * DIL_OUT_W), slopes_g,
                                                dil=dil, batch=batch, seq=seq)
            dil_outs.append(o_g)
            dil_lses.append(lse_g)

        y = _merge(oa, _dil_mix(dil_outs, dil_lses), z, w_branch_diff[i], w_branch_dil[i])
        pre = _matmul(y, w_mix_out[i], F32, n_out=d, wcol=lambda j: j, residual=(h, alpha),
                      tn=tn, name="mix_out")
        h, h_bf = _ln_presummed(pre, ln2_g[i], ln2_b[i])

        f = _ffn(h_bf, ffn2_w_in_bf, ffn2_w_out_bf)
        h, h_bf = _res_ln(h, f, ln3_g[i], ln3_b[i], h_bf, ffn2_w_in_bf, ffn2_w_out_bf,
                          alpha=alpha, scale=0.5)

        pre = _ple(h_bf, p[i].reshape(T, PLE_DIM), w_ple_gate[i], w_ple_proj[i], h, alpha=alpha)
        if i + 1 < depth:
            h, h_bf = _ln_presummed(pre, ln4_g[i], ln4_b[i])
        else:
            h = _ln_presummed(pre, ln4_g[i], ln4_b[i], with_bf16=False)

    return h.reshape(batch, seq, d)
```

```python
import functools
import math

import jax
import jax.numpy as jnp
from jax import lax
from jax.experimental import pallas as pl
from jax.experimental.pallas import tpu as pltpu

F32 = jnp.float32
BF16 = jnp.bfloat16

D_MODEL = 4096
PLE_DIM = 256
D_FF = 11008
DIFF_HEADS = 8
DIFF_HEAD_DIM = 128
DIL_PATTERNS = ((128, 1), (512, 4), (2048, 16))
DIL_GROUP_HEADS = 8
DIL_HEAD_DIM = 128
DIL_HALF_KEYS = 64
LN_EPS = 1e-5
SUBLN_EPS = 1e-5
NEG_INF = -1e30

DIFF_QK_W = DIFF_HEADS * 2 * DIFF_HEAD_DIM
DIFF_V_W = DIFF_HEADS * 2 * DIFF_HEAD_DIM
DIL_W = len(DIL_PATTERNS) * DIL_GROUP_HEADS * DIL_HEAD_DIM
DIL_OUT_W = DIL_GROUP_HEADS * DIL_HEAD_DIM
IN_PROJ_W = 2 * DIFF_QK_W + DIFF_V_W + 3 * DIL_W + 2 * D_MODEL

OFF_DQ = 0
OFF_DK = OFF_DQ + DIFF_QK_W
OFF_DV = OFF_DK + DIFF_QK_W
OFF_LQ = OFF_DV + DIFF_V_W
OFF_LK = OFF_LQ + DIL_W
OFF_LV = OFF_LK + DIL_W
OFF_GA = OFF_LV + DIL_W
OFF_GB = OFF_GA + D_MODEL

Z_DQ = 0
Z_DK = Z_DQ + DIFF_QK_W
Z_DV = Z_DK + DIFF_QK_W
Z_LQ = Z_DV + DIFF_V_W
Z_LK = Z_LQ + DIL_OUT_W
Z_LV = Z_LK + DIL_OUT_W
Z_GA = Z_LV + DIL_OUT_W
Z_GB = Z_GA + D_MODEL
Z_W = Z_GB + D_MODEL

V7X_LANES = 128
V7X_VMEM_BYTES = 64 * 1024 * 1024
V7X_VMEM_REQUEST = 56 * 1024 * 1024
V7X_VMEM_REQUEST_LN = 60 * 1024 * 1024

FFN_BLOCK = 256

LOG2E = math.log2(math.e)
COEF_PIECES = 3
POS_SPLIT = 64
RIDER_ROWS = 64


def _params(semantics, vmem_bytes=V7X_VMEM_REQUEST):
    return pltpu.CompilerParams(dimension_semantics=semantics, vmem_limit_bytes=vmem_bytes)


def _ffn_kernel(x_ref, wga_ref, wgb_ref, wua_ref, wub_ref, woa_ref, wob_ref, o_ref):
    j = pl.program_id(1)

    @pl.when(j == 0)
    def _():
        o_ref[...] = jnp.zeros(o_ref.shape, F32)

    x = x_ref[...]
    ha = _swiglu_hidden(x, wga_ref, wua_ref)
    hb = _swiglu_hidden(x, wgb_ref, wub_ref)
    part = jnp.dot(ha, woa_ref[...], preferred_element_type=F32)
    part += jnp.dot(hb, wob_ref[...], preferred_element_type=F32)
    o_ref[...] += part


def _swiglu_hidden(x, wg_ref, wu_ref):
    g = jnp.dot(x, wg_ref[...], preferred_element_type=F32)
    u = jnp.dot(x, wu_ref[...], preferred_element_type=F32)
    return (g * jax.nn.sigmoid(g) * u).astype(BF16)


def _ffn(x_bf, w_in_bf, w_out_bf, *, tm=512, tb=FFN_BLOCK):
    T, D = x_bf.shape
    n_blocks = w_out_bf.shape[0] // tb
    steps = n_blocks // 2

    def first(j):
        return 2 * j

    def second(j):
        return 2 * j + 1

    def col_spec(blk, off):
        return pl.BlockSpec((D, tb), lambda i, j: (0, off + blk(j)))

    def row_spec(blk):
        return pl.BlockSpec((tb, D), lambda i, j: (blk(j), 0))

    return pl.pallas_call(
        _ffn_kernel,
        grid=(T // tm, steps),
        in_specs=[
            pl.BlockSpec((tm, D), lambda i, j: (i, 0)),
            col_spec(first, 0), col_spec(second, 0),
            col_spec(first, n_blocks), col_spec(second, n_blocks),
            row_spec(first), row_spec(second),
        ],
        out_specs=pl.BlockSpec((tm, D), lambda i, j: (i, 0)),
        out_shape=jax.ShapeDtypeStruct((T, D), F32),
        compiler_params=_params(("parallel", "arbitrary")),
        name="ffn",
    )(x_bf, w_in_bf, w_in_bf, w_in_bf, w_in_bf, w_out_bf, w_out_bf)


def _layer_norm_rows(y, g_ref, b_ref):
    mu = jnp.mean(y, axis=-1, keepdims=True)
    yc = y - mu
    var = jnp.mean(yc * yc, axis=-1, keepdims=True)
    return yc * lax.rsqrt(var + LN_EPS) * g_ref[...] + b_ref[...]


def _ffn_tail_specs(w_in_bf, w_out_bf, tm, tb=FFN_BLOCK):
    D = w_in_bf.shape[0]
    n_blocks = w_out_bf.shape[0] // tb
    last = n_blocks - 1
    once = pl.Buffered(1)
    specs = [pl.BlockSpec((tm, D), lambda i: (i, 0)),
             pl.BlockSpec((D, tb), lambda i: (0, last), pipeline_mode=once),
             pl.BlockSpec((D, tb), lambda i: (0, n_blocks + last), pipeline_mode=once),
             pl.BlockSpec((tb, D), lambda i: (last, 0), pipeline_mode=once)]
    return specs, [w_in_bf, w_in_bf, w_out_bf]


def _ffn_ln_rows(h_ref, br_ref, g_ref, b_ref, x_ref, wg_ref, wu_ref, wo_ref, alpha, scale):
    half = x_ref.shape[0] // 2
    outs = []
    for rows in (slice(0, half), slice(half, 2 * half)):
        tail = jnp.dot(_swiglu_hidden(x_ref[rows, :], wg_ref, wu_ref), wo_ref[...],
                       preferred_element_type=F32)
        outs.append(_layer_norm_rows(
            alpha * h_ref[rows, :] + scale * (br_ref[rows, :] + tail), g_ref, b_ref))
    return jnp.concatenate(outs, axis=0)


def _ln_kernel(h_ref, br_ref, g_ref, b_ref, x_ref, wg_ref, wu_ref, wo_ref, o_ref, obf_ref,
               *, alpha, scale):
    out = _ffn_ln_rows(h_ref, br_ref, g_ref, b_ref, x_ref, wg_ref, wu_ref, wo_ref, alpha, scale)
    o_ref[...] = out
    obf_ref[...] = out.astype(BF16)


def _ln_class_major_kernel(h_ref, br_ref, g_ref, b_ref, x_ref, wg_ref, wu_ref, wo_ref,
                           o_ref, obf_ref, *rest, alpha, scale, dils):
    cm_refs, slab_sc = rest[:-1], rest[-1]
    out = _ffn_ln_rows(h_ref, br_ref, g_ref, b_ref, x_ref, wg_ref, wu_ref, wo_ref, alpha, scale)
    o_ref[...] = out
    obf_ref[...] = out.astype(BF16)
    tm, d = out.shape
    for s in range(d // V7X_LANES):
        slab_sc[s] = out[:, s * V7X_LANES:(s + 1) * V7X_LANES]
    for dil, cm_ref in zip(dils, cm_refs):
        for r in range(dil):
            for s in range(d // V7X_LANES):
                cm_ref[r, :, s * V7X_LANES:(s + 1) * V7X_LANES] = slab_sc[
                    s, pl.ds(r, tm // dil, stride=dil), :].astype(BF16)


def _res_ln_class_major(h, branch, g, b, ffn_in, w_in_bf, w_out_bf, *, alpha, scale, batch, seq,
                        dils, tm=256):
    T, D = h.shape
    tiles = seq // tm
    row = pl.BlockSpec((tm, D), lambda i: (i, 0))
    vec = pl.BlockSpec((1, D), lambda i: (0, 0))
    tail_specs, tail_args = _ffn_tail_specs(w_in_bf, w_out_bf, tm)
    cm_specs = [pl.BlockSpec((None, dil, tm // dil, D), lambda i: (i // tiles, 0, i % tiles, 0))
                for dil in dils]
    cm_shapes = [jax.ShapeDtypeStruct((batch, dil, seq // dil, D), BF16) for dil in dils]
    outs = pl.pallas_call(
        functools.partial(_ln_class_major_kernel, alpha=alpha, scale=scale, dils=dils),
        grid=(T // tm,),
        in_specs=[row, row, vec, vec] + tail_specs,
        out_specs=[row, row] + cm_specs,
        out_shape=[jax.ShapeDtypeStruct((T, D), F32), jax.ShapeDtypeStruct((T, D), BF16)]
        + cm_shapes,
        scratch_shapes=[pltpu.VMEM((D // V7X_LANES, tm, V7X_LANES), F32)],
        compiler_params=_params(("parallel",), V7X_VMEM_REQUEST_LN),
        name="res_ln_class_major",
    )(h, branch, g.reshape(1, D), b.reshape(1, D), ffn_in, *tail_args)
    return outs[0], outs[1], [cm.reshape(T, D) for cm in outs[2:]]


def _ln_presummed_kernel(y_ref, g_ref, b_ref, *o_refs):
    out = _layer_norm_rows(y_ref[...], g_ref, b_ref)
    for o_ref in o_refs:
        o_ref[...] = out.astype(o_ref.dtype)


def _ln_presummed(y, g, b, *, with_bf16=True, tm=256):
    T, D = y.shape
    row = pl.BlockSpec((tm, D), lambda i: (i, 0))
    vec = pl.BlockSpec((1, D), lambda i: (0, 0))
    f32_out = jax.ShapeDtypeStruct((T, D), F32)
    return pl.pallas_call(
        _ln_presummed_kernel,
        grid=(T // tm,),
        in_specs=[row, vec, vec],
        out_specs=[row, row] if with_bf16 else row,
        out_shape=[f32_out, jax.ShapeDtypeStruct((T, D), BF16)] if with_bf16 else f32_out,
        compiler_params=_params(("parallel",)),
        name="ln",
    )(y, g.reshape(1, D), b.reshape(1, D))


def _res_ln(h, branch, g, b, ffn_in, w_in_bf, w_out_bf, *, alpha, scale, tm=256):
    T, D = h.shape
    row = pl.BlockSpec((tm, D), lambda i: (i, 0))
    vec = pl.BlockSpec((1, D), lambda i: (0, 0))
    tail_specs, tail_args = _ffn_tail_specs(w_in_bf, w_out_bf, tm)
    return pl.pallas_call(
        functools.partial(_ln_kernel, alpha=alpha, scale=scale),
        grid=(T // tm,),
        in_specs=[row, row, vec, vec] + tail_specs,
        out_specs=[row, row],
        out_shape=[jax.ShapeDtypeStruct((T, D), F32), jax.ShapeDtypeStruct((T, D), BF16)],
        compiler_params=_params(("parallel",)),
        name="res_ln",
    )(h, branch, g.reshape(1, D), b.reshape(1, D), ffn_in, *tail_args)


def _matmul_kernel(*refs, scaled, residual_alpha, rider_blocks, n_col_tiles):
    refs = list(refs)
    n_riders = len(rider_blocks)
    x_ref, w_ref = refs[:2]
    del refs[:2]
    s_ref = refs.pop(0) if scaled else None
    h_ref = refs.pop(0) if residual_alpha is not None else None
    rider_in, o_ref, rider_out = refs[:n_riders], refs[n_riders], refs[n_riders + 1:]
    acc = jnp.dot(x_ref[...], w_ref[...].astype(BF16), preferred_element_type=F32)
    if scaled:
        acc = acc * s_ref[...]
    if residual_alpha is not None:
        acc = residual_alpha * h_ref[...] + acc
    o_ref[...] = acc.astype(o_ref.dtype)
    step = pl.program_id(0) * n_col_tiles + pl.program_id(1)
    for src, dst, n_blocks in zip(rider_in, rider_out, rider_blocks):
        @pl.when(step < n_blocks)
        def _(src=src, dst=dst):
            dst[...] = src[...].astype(BF16)


def _matmul(x_bf, w, out_dtype, *, n_out, wcol, colscale=None, residual=None, riders=(),
            tm=1024, tn=512, name="matmul"):
    T, K = x_bf.shape
    n_col_tiles = n_out // tn
    tile = pl.BlockSpec((tm, tn), lambda i, j: (i, j))
    in_specs = [
        pl.BlockSpec((tm, K), lambda i, j: (i, 0)),
        pl.BlockSpec((K, tn), lambda i, j: (0, wcol(j))),
    ]
    args = [x_bf, w]
    if colscale is not None:
        in_specs.append(pl.BlockSpec((1, tn), lambda i, j: (0, j)))
        args.append(colscale.reshape(1, n_out))
    if residual is not None:
        in_specs.append(tile)
        args.append(residual[0])
    rider_specs, rider_blocks = [], []
    for arr, block, axis in riders:
        n_blocks = arr.shape[axis] // block[axis]
        assert n_blocks <= (T // tm) * n_col_tiles

        def index_map(i, j, axis=axis, n_blocks=n_blocks):
            blk = jnp.minimum(i * n_col_tiles + j, n_blocks - 1)
            return tuple(blk if a == axis else 0 for a in range(2))

        rider_specs.append(pl.BlockSpec(block, index_map))
        rider_blocks.append(n_blocks)
        args.append(arr)
    outs = pl.pallas_call(
        functools.partial(_matmul_kernel, scaled=colscale is not None,
                          residual_alpha=None if residual is None else residual[1],
                          rider_blocks=tuple(rider_blocks), n_col_tiles=n_col_tiles),
        grid=(T // tm, n_col_tiles),
        in_specs=in_specs + rider_specs,
        out_specs=[tile] + rider_specs,
        out_shape=[jax.ShapeDtypeStruct((T, n_out), out_dtype)]
        + [jax.ShapeDtypeStruct(arr.shape, BF16) for arr, _, _ in riders],
        compiler_params=_params(("arbitrary", "arbitrary")),
        name=name,
    )(*args)
    return outs if riders else outs[0]


def _main_wcol(tn):
    per_sec = DIL_OUT_W // tn
    first = OFF_LQ // tn
    n_pat = len(DIL_PATTERNS)

    def wcol(j):
        k = j - first
        dil_tile = first + (k // per_sec) * (n_pat * per_sec) + k % per_sec
        gate_tile = j + (n_pat - 1) * 3 * per_sec
        return jnp.where(j < first, j, jnp.where(k < 3 * per_sec, dil_tile, gate_tile))

    return wcol


def _group_wcol(group, tn):
    per_sec = DIL_OUT_W // tn

    def wcol(j):
        return (OFF_LQ + group * DIL_OUT_W) // tn + (j // per_sec) * (DIL_W // tn) + j % per_sec

    return wcol


def _alibi_columns(pos, coefs, q_side):
    lane = lax.broadcasted_iota(jnp.int32, pos.shape, 1)
    piece = jnp.where(lane < 3, lane, jnp.where(lane < 6, lane - 3,
                                                jnp.where(lane < 9, lane - 6, lane - 9)))
    cv = jnp.where(piece == 0, coefs[0], jnp.where(piece == 1, coefs[1], coefs[2]))
    hi = (pos - (pos & (POS_SPLIT - 1))).astype(F32)
    lo = (pos & (POS_SPLIT - 1)).astype(F32)
    n = 2 * COEF_PIECES
    if q_side:
        cols = jnp.where(lane < COEF_PIECES, -hi, jnp.where(lane < n, -lo, cv))
    else:
        cols = jnp.where(lane < n, cv, jnp.where(lane < n + COEF_PIECES, hi, lo))
    return jnp.where(lane < 2 * n, cols, 0.0)


def _diff_attn_kernel(coef_ref, lam_ref, g_ref, q_ref, k_ref, v_ref, o_ref,
                      kaug_sc, qaug_sc, corr_sc, sa_sc, sb_sc, pa_sc, pb_sc, m_sc, l_sc,
                      alphaa_sc, alphab_sc, acc_sc, *, tq, nt, rb, seq, lambda_init):
    h = pl.program_id(1)
    qt = pl.program_id(2)
    qis = [nt * qt + u for u in range(nt)]
    dh = DIFF_HEAD_DIM
    n_chunks = seq // tq
    coefs = [coef_ref[h, t] for t in range(COEF_PIECES)]
    row_iota = lax.broadcasted_iota(jnp.int32, (tq, dh), 0)

    @pl.when(qt == 0)
    def _():
        def build(t, carry):
            r0 = pl.multiple_of(t * tq, tq)
            ak = _alibi_columns(row_iota + r0, coefs, q_side=False).astype(BF16)
            for c in range(2):
                kaug_sc[c, pl.ds(r0, tq), 0:dh] = k_ref[pl.ds(r0, tq), c * dh:(c + 1) * dh]
                kaug_sc[c, pl.ds(r0, tq), dh:2 * dh] = ak
            return carry

        lax.fori_loop(0, n_chunks, build, 0)
        row = lax.broadcasted_iota(jnp.int32, (tq, tq), 0)
        col = lax.broadcasted_iota(jnp.int32, (tq, tq), 1)
        slope2 = coefs[0] + coefs[1] + coefs[2]
        corr_sc[0] = jnp.zeros((tq, tq), F32)
        corr_sc[1] = (2.0 * slope2) * jnp.minimum(row - col, 0).astype(F32)

    for u, qi in enumerate(qis):
        aq = _alibi_columns(row_iota + qi * tq, coefs, q_side=True)
        for c in range(2):
            q_c = q_ref[u * tq:(u + 1) * tq, c * dh:(c + 1) * dh]
            qaug_sc[u, 0, c, :, 0:dh] = q_c
            qaug_sc[u, 0, c, :, dh:2 * dh] = aq.astype(BF16)
            qaug_sc[u, 1, c, :, 0:dh] = q_c
            qaug_sc[u, 1, c, :, dh:2 * dh] = (-aq).astype(BF16)

    m_sc[...] = jnp.full(m_sc.shape, -jnp.inf, F32)
    l_sc[...] = jnp.zeros(l_sc.shape, F32)
    acc_sc[...] = jnp.zeros(acc_sc.shape, F32)

    n_slabs = tq // V7X_LANES
    last = n_chunks - 1

    def scores(kc, s_ref, u):
        ks = pl.multiple_of(kc * tq, tq)
        side = jnp.where(kc > qis[u], 1, 0)
        for c in range(2):
            s_ref[u, c] = lax.dot_general(
                qaug_sc[u, side, c], kaug_sc[c, pl.ds(ks, tq), :],
                (((1,), (1,)), ((), ())), preferred_element_type=F32)

    def softmax(kc, s_ref, p_ref, alpha_ref, near_diag, u):
        on_diag = jnp.where(kc == qis[u], 1, 0)
        for c in range(2):
            for b in range(tq // rb):
                rows = slice(b * rb, (b + 1) * rb)
                sb = s_ref[u, c, rows, :]
                if near_diag:
                    sb = sb + corr_sc[on_diag, rows, :]
                slabs = [sb[:, t * V7X_LANES:(t + 1) * V7X_LANES] for t in range(n_slabs)]
                mx = functools.reduce(jnp.maximum, slabs)
                m_old = m_sc[u, c, rows, :]
                m_new = jnp.maximum(m_old, jnp.broadcast_to(
                    jnp.max(mx, axis=-1, keepdims=True), (rb, V7X_LANES)))
                a = jnp.exp2(m_old - m_new)
                ps = [jnp.exp2(sl - m_new) for sl in slabs]
                l_sc[u, c, rows, :] = a * l_sc[u, c, rows, :] + functools.reduce(jnp.add, ps)
                p_ref[u, c, rows, :] = jnp.concatenate(ps, axis=-1).astype(BF16)
                alpha_ref[u, c, rows, :] = a
                m_sc[u, c, rows, :] = m_new

    def accumulate(kc, p_ref, alpha_ref, u):
        ks = pl.multiple_of(kc * tq, tq)
        for c in range(2):
            alpha = alpha_ref[u, c]
            acc_sc[u, c] = jnp.concatenate([alpha, alpha], axis=-1) * acc_sc[u, c] + jnp.dot(
                p_ref[u, c], v_ref[pl.ds(ks, tq), :], preferred_element_type=F32)

    tiles = range(nt)

    def pair_body(kc, near_diag, first=False, final=False):
        for u in tiles:
            if not first:
                accumulate(kc - 1, pb_sc, alphab_sc, u)
            scores(kc + 1, sb_sc, u)
        for u in tiles:
            softmax(kc, sa_sc, pa_sc, alphaa_sc, near_diag, u)
        for u in tiles:
            accumulate(kc, pa_sc, alphaa_sc, u)
            if not final:
                scores(kc + 2, sa_sc, u)
        for u in tiles:
            softmax(kc + 1, sb_sc, pb_sc, alphab_sc, near_diag, u)

    def pair(t, first=False, final=False):
        lax.cond(t == qt,
                 functools.partial(pair_body, 2 * t, True, first, final),
                 functools.partial(pair_body, 2 * t, False, first, final))

    def middle_pair(t, carry):
        pair(t)
        return carry

    n_pairs = n_chunks // 2
    for u in tiles:
        scores(0, sa_sc, u)
    pair(0, first=True)
    lax.fori_loop(1, n_pairs - 1, middle_pair, 0)
    pair(n_pairs - 1, final=True)
    for u in tiles:
        accumulate(last, pb_sc, alphab_sc, u)

    lam_rows = lam_ref[...]
    lam = (jnp.exp(jnp.sum(lam_rows[0:1] * lam_rows[1:2], axis=-1, keepdims=True))
           - jnp.exp(jnp.sum(lam_rows[2:3] * lam_rows[3:4], axis=-1, keepdims=True))
           + lambda_init)
    for u in tiles:
        l0 = jnp.sum(l_sc[u, 0], axis=-1, keepdims=True)
        l1 = jnp.sum(l_sc[u, 1], axis=-1, keepdims=True)
        o = acc_sc[u, 0] / l0 - lam * (acc_sc[u, 1] / l1)
        ms = jnp.mean(o * o, axis=-1, keepdims=True)
        o = o * lax.rsqrt(ms + SUBLN_EPS) * g_ref[...] * (1.0 - lambda_init)
        o_ref[u * tq:(u + 1) * tq, :] = o.astype(o_ref.dtype)


def _diff_attention(z, coefs, lam_rows, subln_g, *, batch, seq, lambda_init, tq=512, nt=2, rb=32):
    assert nt == 2
    T = z.shape[0]
    e = 2 * DIFF_HEAD_DIM
    nq = seq // (nt * tq)
    qspec = pl.BlockSpec((nt * tq, e), lambda b, h, i: (b * nq + i, Z_DQ // e + h))
    kspec = pl.BlockSpec((seq, e), lambda b, h, i: (b, Z_DK // e + h))
    vspec = pl.BlockSpec((seq, e), lambda b, h, i: (b, Z_DV // e + h))
    return pl.pallas_call(
        functools.partial(_diff_attn_kernel, tq=tq, nt=nt, rb=rb, seq=seq,
                          lambda_init=lambda_init),
        grid=(batch, DIFF_HEADS, nq),
        in_specs=[
            pl.BlockSpec(memory_space=pltpu.SMEM),
            pl.BlockSpec((4, DIFF_HEAD_DIM), lambda b, h, i: (0, 0)),
            pl.BlockSpec((1, e), lambda b, h, i: (0, 0)),
            qspec, kspec, vspec,
        ],
        out_specs=pl.BlockSpec((nt * tq, e), lambda b, h, i: (b * nq + i, h)),
        out_shape=jax.ShapeDtypeStruct((T, DIFF_V_W), BF16),
        scratch_shapes=[
            pltpu.VMEM((2, seq, e), BF16),
            pltpu.VMEM((nt, 2, 2, tq, e), BF16),
            pltpu.VMEM((2, tq, tq), F32),
            pltpu.VMEM((nt, 2, tq, tq), F32),
            pltpu.VMEM((nt, 2, tq, tq), F32),
            pltpu.VMEM((nt, 2, tq, tq), BF16),
            pltpu.VMEM((nt, 2, tq, tq), BF16),
            pltpu.VMEM((nt, 2, tq, V7X_LANES), F32),
            pltpu.VMEM((nt, 2, tq, V7X_LANES), F32),
            pltpu.VMEM((nt, 2, tq, V7X_LANES), F32),
            pltpu.VMEM((nt, 2, tq, V7X_LANES), F32),
            pltpu.VMEM((nt, 2, tq, e), F32),
        ],
        compiler_params=_params(("parallel", "parallel", "arbitrary")),
        name="diff_attn",
    )(coefs, lam_rows, subln_g.reshape(1, e), z, z, z)


def _dil_attn_kernel(slopes_ref, q_ref, k_ref, v_ref, o_ref, lse_ref, *, seq, dil, tq, group):
    h = pl.program_id(1)
    slope = slopes_ref[h]
    w = DIL_HALF_KEYS
    nk = tq + 2 * w
    length = seq // dil
    blocks_per_class = length // tq
    scale = DIL_HEAD_DIM ** -0.5
    row = lax.broadcasted_iota(jnp.int32, (tq, nk), 0)
    col = lax.broadcasted_iota(jnp.int32, (tq, nk), 1)
    col_minus_row = col - row

    def block_offsets(i):
        cls = i // blocks_per_class
        qs_local = (i % blocks_per_class) * tq
        ks_local = jnp.clip(qs_local - w, 0, length - nk)
        return cls, qs_local, ks_local

    def body(it, carry):
        blocks = [block_offsets(it * group + g) for g in range(group)]
        scores = []
        for cls, qs_local, ks_local in blocks:
            q = q_ref[pl.ds(pl.multiple_of(cls * length + qs_local, tq), tq), :]
            k = k_ref[pl.ds(pl.multiple_of(cls * length + ks_local, w), nk), :]
            scores.append(lax.dot_general(q, k, (((1,), (1,)), ((), ())),
                                          preferred_element_type=F32))
        for (cls, qs_local, ks_local), s in zip(blocks, scores):
            rel = jnp.abs(col_minus_row + (ks_local - qs_local))
            s = jnp.where(rel <= w, s * scale - slope * rel.astype(F32), NEG_INF)
            m = jnp.max(s, axis=-1, keepdims=True)
            e = jnp.exp(s - m)
            z = jnp.sum(e, axis=-1, keepdims=True)
            v = v_ref[pl.ds(pl.multiple_of(cls * length + ks_local, w), nk), :]
            o = jnp.dot(e.astype(BF16), v, preferred_element_type=F32) / z
            if dil == 1:
                rows = pl.ds(pl.multiple_of(qs_local, tq), tq)
            else:
                rows = pl.ds(qs_local * dil + cls, tq, stride=dil)
            o_ref[rows, :] = o
            lse_ref[rows, :] = jnp.broadcast_to(m + jnp.log(z), (tq, DIL_HEAD_DIM))
        return carry

    lax.fori_loop(0, seq // (tq * group), body, 0)


def _dilated_attention(qkv, offsets, slopes_g, *, dil, batch, seq, tq=128, group=16):
    T = qkv.shape[0]
    hd = DIL_HEAD_DIM

    def in_spec(off):
        return pl.BlockSpec((seq, hd), lambda b, h: (b, off // hd + h))

    out_spec = pl.BlockSpec((seq, hd), lambda b, h: (b, h))
    out_sds = jax.ShapeDtypeStruct((T, DIL_OUT_W), F32)
    return pl.pallas_call(
        functools.partial(_dil_attn_kernel, seq=seq, dil=dil, tq=tq, group=group),
        grid=(batch, DIL_GROUP_HEADS),
        in_specs=[pl.BlockSpec(memory_space=pltpu.SMEM)] + [in_spec(o) for o in offsets],
        out_specs=[out_spec, out_spec],
        out_shape=[out_sds, out_sds],
        compiler_params=_params(("parallel", "parallel")),
        name=f"dil_attn_{dil}",
    )(slopes_g, qkv, qkv, qkv)


def _dil_mix_kernel(o0_ref, o1_ref, o2_ref, l0_ref, l1_ref, l2_ref, ob_ref):
    l0, l1, l2 = l0_ref[...], l1_ref[...], l2_ref[...]
    m = jnp.maximum(jnp.maximum(l0, l1), l2)
    w0, w1, w2 = jnp.exp(l0 - m), jnp.exp(l1 - m), jnp.exp(l2 - m)
    ob = (w0 * o0_ref[...] + w1 * o1_ref[...] + w2 * o2_ref[...]) / (w0 + w1 + w2)
    ob_ref[...] = ob.astype(ob_ref.dtype)


def _dil_mix(dil_outs, dil_lses, *, tm=512):
    T, W = dil_outs[0].shape
    row = pl.BlockSpec((tm, W), lambda i: (i, 0))
    return pl.pallas_call(
        _dil_mix_kernel,
        grid=(T // tm,),
        in_specs=[row] * 6,
        out_specs=row,
        out_shape=jax.ShapeDtypeStruct((T, W), BF16),
        compiler_params=_params(("parallel",)),
        name="dil_mix",
    )(*dil_outs, *dil_lses)


def _merge_kernel(oa_ref, ob_ref, ga_ref, gb_ref, wa_ref, wb_ref, y_ref):
    ya = jnp.dot(oa_ref[...], wa_ref[...].astype(BF16), preferred_element_type=F32)
    yb = jnp.dot(ob_ref[...], wb_ref[...].astype(BF16), preferred_element_type=F32)
    y = (jax.nn.sigmoid(ga_ref[...].astype(F32)) * ya
         + jax.nn.sigmoid(gb_ref[...].astype(F32)) * yb)
    y_ref[...] = y.astype(y_ref.dtype)


def _merge(oa, ob, z, wa, wb, *, tm=1024, tn=512):
    T = oa.shape[0]
    return pl.pallas_call(
        _merge_kernel,
        grid=(T // tm, D_MODEL // tn),
        in_specs=[pl.BlockSpec((tm, DIFF_V_W), lambda i, j: (i, 0)),
                  pl.BlockSpec((tm, DIL_OUT_W), lambda i, j: (i, 0)),
                  pl.BlockSpec((tm, tn), lambda i, j: (i, Z_GA // tn + j)),
                  pl.BlockSpec((tm, tn), lambda i, j: (i, Z_GB // tn + j)),
                  pl.BlockSpec((DIFF_V_W, tn), lambda i, j: (0, j)),
                  pl.BlockSpec((DIL_OUT_W, tn), lambda i, j: (0, j))],
        out_specs=pl.BlockSpec((tm, tn), lambda i, j: (i, j)),
        out_shape=jax.ShapeDtypeStruct((T, D_MODEL), BF16),
        compiler_params=_params(("parallel", "parallel")),
        name="merge",
    )(oa, ob, z, z, wa, wb)


def _ple_kernel(x_ref, p_ref, wg_ref, wp_ref, h_ref, o_ref, *, alpha):
    gate = jnp.dot(x_ref[...], wg_ref[...].astype(BF16), preferred_element_type=F32)
    proj = jnp.dot(p_ref[...].astype(BF16), wp_ref[...].astype(BF16),
                   preferred_element_type=F32)
    o_ref[...] = alpha * h_ref[...] + jax.nn.sigmoid(gate) * proj


def _ple(x_bf, p, wg, wp, h, *, alpha, tm=1024, tn=512):
    T, D = x_bf.shape
    tile = pl.BlockSpec((tm, tn), lambda i, j: (i, j))
    return pl.pallas_call(
        functools.partial(_ple_kernel, alpha=alpha),
        grid=(T // tm, D // tn),
        in_specs=[
            pl.BlockSpec((tm, D), lambda i, j: (i, 0)),
            pl.BlockSpec((tm, PLE_DIM), lambda i, j: (i, 0)),
            pl.BlockSpec((D, tn), lambda i, j: (0, j)),
            pl.BlockSpec((PLE_DIM, tn), lambda i, j: (0, j)),
            tile,
        ],
        out_specs=tile,
        out_shape=jax.ShapeDtypeStruct((T, D), F32),
        compiler_params=_params(("parallel", "parallel")),
        name="ple",
    )(x_bf, p, wg, wp, h)


def _alibi_slopes(n):
    return jnp.exp2(-8.0 * jnp.arange(1, n + 1, dtype=F32) / n)


def _bf16_pieces(v):
    pieces, rest = [], v
    for _ in range(COEF_PIECES):
        piece = rest.astype(BF16).astype(F32)
        pieces.append(piece)
        rest = rest - piece
    return jnp.stack(pieces, axis=-1)


def kernel(x, p, ffn1_w_in, ffn1_w_out, ln1_g, ln1_b, w_in, lam_q1, lam_k1, lam_q2, lam_k2,
           subln_g, w_branch_diff, w_branch_dil, w_mix_out, ln2_g, ln2_b, ffn2_w_in,
           ffn2_w_out, ln3_g, ln3_b, w_ple_gate, w_ple_proj, ln4_g, ln4_b):
    batch, seq, d = x.shape
    depth = ffn1_w_in.shape[0]
    T = batch * seq
    alpha = (2 * depth) ** 0.25
    n_pat = len(DIL_PATTERNS)
    tn = 512

    h = x.reshape(T, d)
    h_bf = h.astype(BF16)
    diff_coefs = _bf16_pieces(_alibi_slopes(DIFF_HEADS) * LOG2E)
    dil_slopes = _alibi_slopes(n_pat * DIL_GROUP_HEADS).reshape(DIL_GROUP_HEADS, n_pat)
    colscale = jnp.ones((Z_W,), F32).at[Z_DQ:Z_DK].set(DIFF_HEAD_DIM ** -0.5 * LOG2E)

    for i in range(depth):
        lambda_init = 0.8 - 0.6 * math.exp(-0.3 * i)

        ffn1_w_in_bf, ffn1_w_out_bf = ffn1_w_in[i].astype(BF16), ffn1_w_out[i].astype(BF16)
        f = _ffn(h_bf, ffn1_w_in_bf, ffn1_w_out_bf)
        dils = tuple(dil for _window, dil in DIL_PATTERNS if dil > 1)
        h, h_bf, h_cm = _res_ln_class_major(h, f, ln1_g[i], ln1_b[i], h_bf, ffn1_w_in_bf,
                                            ffn1_w_out_bf, alpha=alpha, scale=0.5,
                                            batch=batch, seq=seq, dils=dils)
        h_class_major = dict(zip(dils, h_cm))

        z, ffn2_w_in_bf, ffn2_w_out_bf = _matmul(
            h_bf, w_in[i], BF16, n_out=Z_W, wcol=_main_wcol(tn), colscale=colscale, tn=tn,
            riders=((ffn2_w_in[i], (d, V7X_LANES), 1), (ffn2_w_out[i], (RIDER_ROWS, d), 0)),
            name="in_proj")
        lam_rows = jnp.stack([lam_q1[i], lam_k1[i], lam_q2[i], lam_k2[i]]).astype(F32)
        oa = _diff_attention(z, diff_coefs, lam_rows, subln_g[i].astype(F32),
                             batch=batch, seq=seq, lambda_init=lambda_init)

        dil_outs, dil_lses = [], []
        for g, (_window, dil) in enumerate(DIL_PATTERNS):
            slopes_g = dil_slopes[:, g] * dil
            if dil == 1:
                o_g, lse_g = _dilated_attention(z, (Z_LQ, Z_LK, Z_LV), slopes_g, dil=dil,
                                                batch=batch, seq=seq)
            else:
                qkv = _matmul(h_class_major[dil], w_in[i], BF16,
                              n_out=3 * DIL_OUT_W, wcol=_group_wcol(g, tn), tn=tn,
                              name=f"in_proj_dil{dil}")
                o_g, lse_g = _dilated_attention(qkv, (0, DIL_OUT_W, 2 * DIL_OUT_W), slopes_g,
                                                dil=dil, batch=batch, seq=seq)
            dil_outs.append(o_g)
            dil_lses.append(lse_g)

        y = _merge(oa, _dil_mix(dil_outs, dil_lses), z, w_branch_diff[i], w_branch_dil[i])
        pre = _matmul(y, w_mix_out[i], F32, n_out=d, wcol=lambda j: j, residual=(h, alpha),
                      tn=tn, name="mix_out")
        h, h_bf = _ln_presummed(pre, ln2_g[i], ln2_b[i])

        f = _ffn(h_bf, ffn2_w_in_bf, ffn2_w_out_bf)
        h, h_bf = _res_ln(h, f, ln3_g[i], ln3_b[i], h_bf, ffn2_w_in_bf, ffn2_w_out_bf,
                          alpha=alpha, scale=0.5)

        pre = _ple(h_bf, p[i].reshape(T, PLE_DIM), w_ple_gate[i], w_ple_proj[i], h, alpha=alpha)
        if i + 1 < depth:
            h, h_bf = _ln_presummed(pre, ln4_g[i], ln4_b[i])
        else:
            h = _ln_presummed(pre, ln4_g[i], ln4_b[i], with_bf16=False)

    return h.reshape(batch, seq, d)
```
